```python
import jax, jax.numpy as jnp
from jax import lax
import numpy as np

D_MODEL = 1024
BATCH = 8
SEQ = 2048
DEPTH = 4

GRID_W = 64
CTX_LEN = 256
N_MIXERS = 2
ML_INNER = 2 * D_MODEL
ML_HEADS = 4
ML_HEAD_DIM = ML_INNER // ML_HEADS
ML_QKV_BLOCK = 4
ML_CONV_W = 5
ML_CHUNK = 64
NA_HEAD_DIM = 64
NA_HEADS = D_MODEL // NA_HEAD_DIM
NA_KH = 8
NA_KW = 16
FF_HIDDEN = ((8 * D_MODEL // 3 + 127) // 128) * 128
FF_CONV_W = 3
EPS = 1e-6

kernel_name = "hybrid_mlstm_natten_convffn_diffusion"


def rmsnorm(x, g):
    xf = x.astype(jnp.float32)
    y = xf * lax.rsqrt(jnp.mean(xf * xf, axis=-1, keepdims=True) + EPS)
    return (y * g.astype(jnp.float32)).astype(x.dtype)


def modulate(h, shift, scale):
    return h * (1 + scale) + shift


def dwconv(x, w, b):
    k = w.shape[0]
    y = lax.conv_general_dilated(x, w[:, None, :].astype(x.dtype), window_strides=(1,),
                                 padding=[(k // 2, k // 2)],
                                 dimension_numbers=('NWC', 'WIO', 'NWC'),
                                 feature_group_count=x.shape[-1])
    return y + b


def split_heads(t, n_heads):
    bsz, t_len, _ = t.shape
    return t.reshape(bsz, t_len, n_heads, -1).transpose(0, 2, 1, 3)


def merge_heads(t):
    bsz, n_heads, t_len, dh = t.shape
    return t.transpose(0, 2, 1, 3).reshape(bsz, t_len, n_heads * dh)


def blockdiag(u, w):
    bsz, t_len, _ = u.shape
    ub = u.reshape(bsz, t_len, w.shape[0], w.shape[1])
    return jnp.einsum('btni,nio->btno', ub, w).reshape(bsz, t_len, -1)


def mlstm_chunkwise(q, k, v, log_i, log_f, state):
    bsz, n_heads, t_len, dh = q.shape
    nc = t_len // ML_CHUNK

    def chunks(a):
        return jnp.moveaxis(a.reshape(bsz, n_heads, nc, ML_CHUNK, *a.shape[3:]), 2, 0)

    tril = jnp.tril(jnp.ones((ML_CHUNK, ML_CHUNK), dtype=bool))

    def step(carry, inp):
        c_mem, n_mem, m_prev = carry
        qc, kc, vc, li, lf = inp
        b = jnp.cumsum(lf, axis=-1)
        log_d = jnp.where(tril, b[..., :, None] - b[..., None, :] + li[..., None, :], -jnp.inf)
        inter = b + m_prev[..., None]
        m_t = jnp.maximum(inter, jnp.max(log_d, axis=-1))
        s = jnp.einsum('bhtd,bhsd->bhts', qc, kc) * jnp.exp(log_d - m_t[..., None])
        w_inter = jnp.exp(inter - m_t)
        num = (jnp.einsum('bhts,bhsd->bhtd', s, vc)
               + w_inter[..., None] * jnp.einsum('bhvk,bhtk->bhtv', c_mem, qc))
        den = jnp.sum(s, axis=-1) + w_inter * jnp.einsum('bhk,bhtk->bht', n_mem, qc)
        h = num / jnp.maximum(jnp.abs(den), jnp.exp(-m_t))[..., None]
        g = b[..., -1:] - b + li
        m_new = jnp.maximum(b[..., -1] + m_prev, jnp.max(g, axis=-1))
        decay = jnp.exp(b[..., -1] + m_prev - m_new)
        wk = jnp.exp(g - m_new[..., None])
        c_new = decay[..., None, None] * c_mem + jnp.einsum('bhsv,bhsk->bhvk', vc * wk[..., None], kc)
        n_new = decay[..., None] * n_mem + jnp.einsum('bhs,bhsk->bhk', wk, kc)
        return (c_new, n_new, m_new), h

    state, h = lax.scan(step, state, (chunks(q), chunks(k), chunks(v), chunks(log_i), chunks(log_f)))
    return jnp.moveaxis(h, 0, 2).reshape(bsz, n_heads, t_len, dh), state


def mlstm_mixer(hc, hx, w_in, conv_w, conv_b, wq, wk, wv, gate_w, gate_b, skip, hnorm_g, w_out, need_ctx):
    f32 = jnp.float32

    def project(h):
        u, z = jnp.split(h @ w_in, 2, axis=-1)
        uc = jax.nn.silu(dwconv(u, conv_w, conv_b))
        q = blockdiag(uc, wq)
        k = blockdiag(uc, wk) * (ML_HEAD_DIM ** -0.5)
        v = blockdiag(u, wv)
        gates = jnp.einsum('btc,zcg->zbtg', jnp.concatenate([q, k, v], axis=-1), gate_w) + gate_b[:, None, None, :]
        gates = jnp.moveaxis(gates.astype(f32), -1, 2)
        log_i = gates[:, :, :ML_HEADS]
        log_f = jax.nn.log_sigmoid(gates[:, :, ML_HEADS:])
        heads = lambda t: split_heads(t, ML_HEADS).astype(f32)
        return uc, z, heads(q), heads(k), heads(v), log_i, log_f

    pc = project(hc)
    px = project(hx)
    bsz = hx.shape[0]
    dh = ML_HEAD_DIM
    hs_c, hs_x = [], []
    for d in range(2):
        fl = (lambda a: jnp.flip(a, axis=2)) if d == 1 else (lambda a: a)
        state = (jnp.zeros((bsz, ML_HEADS, dh, dh), f32), jnp.zeros((bsz, ML_HEADS, dh), f32),
                 jnp.zeros((bsz, ML_HEADS), f32))
        h_c, state = mlstm_chunkwise(fl(pc[2]), fl(pc[3]), fl(pc[4]), fl(pc[5][d]), fl(pc[6][d]), state)
        h_x, _ = mlstm_chunkwise(fl(px[2]), fl(px[3]), fl(px[4]), fl(px[5][d]), fl(px[6][d]), state)
        hs_c.append(fl(h_c))
        hs_x.append(fl(h_x))

    def output(hsum, uc, z):
        mu = jnp.mean(hsum, axis=-1, keepdims=True)
        var = jnp.mean(jnp.square(hsum - mu), axis=-1, keepdims=True)
        hn = merge_heads((hsum - mu) * lax.rsqrt(var + EPS)).astype(uc.dtype) * hnorm_g
        return ((hn + skip * uc) * jax.nn.silu(z)) @ w_out

    yx = output(hs_x[0] + hs_x[1], px[0], px[1])
    yc = output(hs_c[0] + hs_c[1], pc[0], pc[1]) if need_ctx else None
    return yc, yx


def na_mixer(hc, hx, w_qkv, b_qkv, rpb, w_out, b_out, need_ctx):
    f32 = jnp.float32
    scale = NA_HEAD_DIM ** -0.5

    def qkv(h):
        return [split_heads(a, NA_HEADS) for a in jnp.split(h @ w_qkv + b_qkv, 3, axis=-1)]

    q_c, k_c, v_c = qkv(hc)
    q_x, k_x, v_x = qkv(hx)
    yc = None
    if need_ctx:
        s = jnp.einsum('bhqd,bhkd->bhqk', q_c, k_c).astype(f32) * scale
        yc = merge_heads(jnp.einsum('bhqk,bhkd->bhqd', jax.nn.softmax(s, axis=-1).astype(v_c.dtype), v_c)) @ w_out + b_out

    bsz, n_heads, s_len, dh = q_x.shape
    rows = s_len // GRID_W
    kh = min(NA_KH, rows)
    kg = k_x.reshape(bsz, n_heads, rows, GRID_W, dh)
    vg = v_x.reshape(bsz, n_heads, rows, GRID_W, dh)
    qg = jnp.moveaxis(q_x.reshape(bsz, n_heads, rows, GRID_W, dh), 2, 0)
    col = jnp.arange(GRID_W)
    col_start = jnp.clip(col - NA_KW // 2, 0, GRID_W - NA_KW)
    col_ok = (col[None, :] >= col_start[:, None]) & (col[None, :] < col_start[:, None] + NA_KW)
    dc_idx = jnp.clip(col[None, :] - col[:, None] + NA_KW - 1, 0, 2 * NA_KW - 2)
    rpb_cols = rpb.astype(f32)[:, :, dc_idx]
    n_loc = kh * GRID_W

    def row_block(args):
        r, q_r = args
        rs = jnp.clip(r - kh // 2, 0, rows - kh)
        kb = lax.dynamic_slice_in_dim(kg, rs, kh, axis=2)
        vb = lax.dynamic_slice_in_dim(vg, rs, kh, axis=2)
        bias = jnp.take(rpb_cols, rs + jnp.arange(kh) - r + NA_KH - 1, axis=1)
        s_loc = (jnp.einsum('bhqd,bhjkd->bhqjk', q_r, kb).astype(f32) * scale
                 + jnp.transpose(bias, (0, 2, 1, 3))[None])
        s_loc = jnp.where(col_ok[:, None, :], s_loc, -jnp.inf).reshape(bsz, n_heads, GRID_W, n_loc)
        s_ctx = jnp.einsum('bhqd,bhcd->bhqc', q_r, k_c).astype(f32) * scale
        p = jax.nn.softmax(jnp.concatenate([s_loc, s_ctx], axis=-1), axis=-1).astype(vb.dtype)
        return (jnp.einsum('bhqn,bhnd->bhqd', p[..., :n_loc], vb.reshape(bsz, n_heads, n_loc, dh))
                + jnp.einsum('bhqc,bhcd->bhqd', p[..., n_loc:], v_c))

    o = lax.map(row_block, (jnp.arange(rows), qg))
    o = jnp.transpose(o, (1, 0, 3, 2, 4)).reshape(bsz, s_len, n_heads * dh)
    return yc, o @ w_out + b_out


def conv_ffn(h, w_up, b_up, conv_w, conv_b, w_down, b_down):
    u = dwconv(h @ w_up + b_up, conv_w, conv_b)
    a, g = jnp.split(u, 2, axis=-1)
    return (a * jax.nn.silu(g)) @ w_down + b_down


def setup_inputs(seed: int = 0) -> dict:
    key = jax.random.key(seed)
    ks = jax.random.split(key, 40)
    n_a = (DEPTH + 1) // 2
    n_b = DEPTH // 2
    nrm = lambda i, shape, s: jax.random.normal(ks[i], shape, jnp.float32) * s
    d = D_MODEL
    gate_b_i = nrm(20, (n_a, 2, ML_HEADS), 0.1)
    gate_b_f = jnp.linspace(3.0, 6.0, ML_HEADS, dtype=jnp.float32) + nrm(21, (n_a, 2, ML_HEADS), 0.1)
    return {
        'x': nrm(0, (BATCH, SEQ, d), 1.0),
        'c': nrm(1, (BATCH, d), 1.0),
        'ctx': nrm(2, (BATCH, CTX_LEN, d), 1.0),
        'c_ctx': nrm(3, (d,), 1.0),
        'ada_w': nrm(4, (DEPTH, d, 6 * d), 0.5 * d ** -0.5),
        'ada_b': nrm(5, (DEPTH, 6 * d), 0.02),
        'norm1_g': 1.0 + nrm(6, (DEPTH, d), 0.02),
        'norm2_g': 1.0 + nrm(7, (DEPTH, d), 0.02),
        'final_g': 1.0 + nrm(8, (d,), 0.02),
        'ml_w_in': nrm(9, (n_a, d, 2 * ML_INNER), d ** -0.5),
        'ml_conv_w': nrm(10, (n_a, ML_CONV_W, ML_INNER), ML_CONV_W ** -0.5),
        'ml_conv_b': nrm(11, (n_a, ML_INNER), 0.02),
        'ml_wq': nrm(12, (n_a, ML_INNER // ML_QKV_BLOCK, ML_QKV_BLOCK, ML_QKV_BLOCK), ML_QKV_BLOCK ** -0.5),
        'ml_wk': nrm(13, (n_a, ML_INNER // ML_QKV_BLOCK, ML_QKV_BLOCK, ML_QKV_BLOCK), ML_QKV_BLOCK ** -0.5),
        'ml_wv': nrm(14, (n_a, ML_INNER // ML_QKV_BLOCK, ML_QKV_BLOCK, ML_QKV_BLOCK), ML_QKV_BLOCK ** -0.5),
        'ml_gate_w': nrm(15, (n_a, 2, 3 * ML_INNER, 2 * ML_HEADS), 0.5 * (3 * ML_INNER) ** -0.5),
        'ml_gate_b': jnp.concatenate([gate_b_i, gate_b_f], axis=-1),
        'ml_skip': 1.0 + nrm(16, (n_a, ML_INNER), 0.02),
        'ml_hnorm_g': 1.0 + nrm(17, (n_a, ML_INNER), 0.02),
        'ml_w_out': nrm(18, (n_a, ML_INNER, d), ML_INNER ** -0.5),
        'na_w_qkv': nrm(22, (n_b, d, 3 * d), d ** -0.5),
        'na_b_qkv': nrm(23, (n_b, 3 * d), 0.02),
        'na_rpb': nrm(24, (n_b, NA_HEADS, 2 * NA_KH - 1, 2 * NA_KW - 1), 0.1),
        'na_w_out': nrm(25, (n_b, d, d), d ** -0.5),
        'na_b_out': nrm(26, (n_b, d), 0.02),
        'ff_w_up': nrm(27, (DEPTH, d, 2 * FF_HIDDEN), d ** -0.5),
        'ff_b_up': nrm(28, (DEPTH, 2 * FF_HIDDEN), 0.02),
        'ff_conv_w': nrm(29, (DEPTH, FF_CONV_W, 2 * FF_HIDDEN), FF_CONV_W ** -0.5),
        'ff_conv_b': nrm(30, (DEPTH, 2 * FF_HIDDEN), 0.02),
        'ff_w_down': nrm(31, (DEPTH, FF_HIDDEN, d), FF_HIDDEN ** -0.5),
        'ff_b_down': nrm(32, (DEPTH, d), 0.02),
    }


def reference(x, c, ctx, c_ctx, ada_w, ada_b, norm1_g, norm2_g, final_g,
              ml_w_in, ml_conv_w, ml_conv_b, ml_wq, ml_wk, ml_wv, ml_gate_w, ml_gate_b,
              ml_skip, ml_hnorm_g, ml_w_out,
              na_w_qkv, na_b_qkv, na_rpb, na_w_out, na_b_out,
              ff_w_up, ff_b_up, ff_conv_w, ff_conv_b, ff_w_down, ff_b_down):
    xx, xc = x, ctx
    silu_c = jax.nn.silu(c)
    silu_cc = jax.nn.silu(c_ctx)
    for i in range(DEPTH):
        need_ctx = i < DEPTH - 1
        mod_x = jnp.split((silu_c @ ada_w[i] + ada_b[i])[:, None, :], 6, axis=-1)
        mod_c = jnp.split(silu_cc @ ada_w[i] + ada_b[i], 6, axis=-1)
        hx = modulate(rmsnorm(xx, norm1_g[i]), mod_x[0], mod_x[1])
        hc = modulate(rmsnorm(xc, norm1_g[i]), mod_c[0], mod_c[1])
        j = i // N_MIXERS
        if i % N_MIXERS == 0:
            yc, yx = mlstm_mixer(hc, hx, ml_w_in[j], ml_conv_w[j], ml_conv_b[j], ml_wq[j], ml_wk[j],
                                 ml_wv[j], ml_gate_w[j], ml_gate_b[j], ml_skip[j], ml_hnorm_g[j],
                                 ml_w_out[j], need_ctx)
        else:
            yc, yx = na_mixer(hc, hx, na_w_qkv[j], na_b_qkv[j], na_rpb[j], na_w_out[j], na_b_out[j], need_ctx)
        xx = xx + mod_x[2] * yx
        hx = modulate(rmsnorm(xx, norm2_g[i]), mod_x[3], mod_x[4])
        xx = xx + mod_x[5] * conv_ffn(hx, ff_w_up[i], ff_b_up[i], ff_conv_w[i], ff_conv_b[i],
                                      ff_w_down[i], ff_b_down[i])
        if need_ctx:
            xc = xc + mod_c[2] * yc
            hc = modulate(rmsnorm(xc, norm2_g[i]), mod_c[3], mod_c[4])
            xc = xc + mod_c[5] * conv_ffn(hc, ff_w_up[i], ff_b_up[i], ff_conv_w[i], ff_conv_b[i],
                                          ff_w_down[i], ff_b_down[i])
    return rmsnorm(xx, final_g)
```

```python
import functools

import jax
import jax.numpy as jnp
from jax import lax
from jax.experimental import pallas as pl
from jax.experimental.pallas import tpu as pltpu

F32 = jnp.float32
BF16 = jnp.bfloat16
EPS = 1e-6

GRID_W = 64
ML_HEADS = 4
ML_QKV_BLOCK = 4
NA_HEAD_DIM = 64
NA_KH = 8
NA_KW = 16

ROW_BLOCK = 256
HALO = 8
BD_TILE = 256
LANE = 128
FF_CHUNK = 256
MOD_ROWS = 16
VMEM_LIMIT = 56 * 1024 * 1024


def _dot(a, b):
    return jnp.dot(a, b, preferred_element_type=F32)


def _dot_nt(a, b):
    return lax.dot_general(a, b, (((1,), (1,)), ((), ())), preferred_element_type=F32)


def _dot_tn(a, b):
    return lax.dot_general(a, b, (((0,), (0,)), ((), ())), preferred_element_type=F32)


def _silu(x):
    return x * jax.nn.sigmoid(x)


def _log_sigmoid(x):
    return jnp.minimum(x, 0.0) - jnp.log1p(jnp.exp(-jnp.abs(x)))


def _norm_mod(x, g, shift, scale):
    y = x * lax.rsqrt(jnp.mean(x * x, axis=-1, keepdims=True) + EPS) * g
    return y * (1.0 + scale) + shift


def _params(sem):
    return pltpu.CompilerParams(dimension_semantics=sem, vmem_limit_bytes=VMEM_LIMIT)


def _resident(shape, index_map):
    return pl.BlockSpec(shape, index_map, pipeline_mode=pl.Buffered(1))


def _ada_kernel(c_ref, w_ref, b_ref, o_ref):
    s = _silu(c_ref[...]).astype(BF16)
    o_ref[...] = _dot(s, w_ref[...].astype(BF16)) + b_ref[...]


def _ada(c_all, ada_w, ada_b):
    depth, d, d6 = ada_w.shape
    n = d6 // d
    return pl.pallas_call(
        _ada_kernel,
        grid=(depth, n),
        in_specs=[
            pl.BlockSpec((MOD_ROWS, d), lambda l, j: (0, 0)),
            pl.BlockSpec((None, d, d), lambda l, j: (l, 0, j)),
            pl.BlockSpec((None, 1, d), lambda l, j: (l, 0, j)),
        ],
        out_specs=pl.BlockSpec((None, MOD_ROWS, d), lambda l, j: (l, 0, j)),
        out_shape=jax.ShapeDtypeStruct((depth, MOD_ROWS, d6), F32),
        compiler_params=_params(("arbitrary", "arbitrary")),
        name="ada",
    )(c_all, ada_w, ada_b.reshape(depth, 1, d6))


class _Rows:
    def __init__(self, bsz, nblk_total, layer, first_blk):
        self.bsz = bsz
        self.nblk_total = nblk_total
        self.layer = layer
        self.first = first_blk
        self.grid = (bsz, nblk_total - first_blk)

    def rows(self, width, rows=ROW_BLOCK):
        f = self.first
        return pl.BlockSpec((None, rows, width), lambda b, j: (b, j + f, 0))

    def mod(self, d):
        f, bsz, layer = self.first, self.bsz, self.layer
        return pl.BlockSpec((None, None, 6, d), lambda b, j: (layer, jnp.where(j + f == 0, bsz, b), 0, 0))


def _const2(shape):
    return _resident(shape, lambda b, j: (0,) * len(shape))


def _proj_kernel(x_ref, mod_ref, g_ref, w_ref, *rest, shift, scale, chunk, splits, has_bias, out_scale):
    if has_bias:
        b_ref, out_refs = rest[0], rest[1:]
    else:
        b_ref, out_refs = None, rest
    h = _norm_mod(x_ref[...], g_ref[...], mod_ref[shift:shift + 1, :], mod_ref[scale:scale + 1, :]).astype(BF16)
    n = w_ref.shape[1]
    per_out = n // len(out_refs)
    for c in range(n // chunk):
        y = _dot(h, w_ref[:, c * chunk:(c + 1) * chunk])
        if has_bias:
            y = y + b_ref[:, c * chunk:(c + 1) * chunk]
        oi, off = divmod(c * chunk, per_out)
        if out_scale[oi] != 1.0:
            y = y * out_scale[oi]
        y = y.astype(BF16)
        o_ref = out_refs[oi]
        if splits:
            for t in range(chunk // LANE):
                o_ref[(off + t * LANE) // LANE] = y[:, t * LANE:(t + 1) * LANE]
        else:
            o_ref[:, off:off + chunk] = y


def _ml_in(rows, r, mods, g, w_in):
    bsz, t, d = r.shape
    n = w_in.shape[1]
    half = n // 2
    kern = functools.partial(_proj_kernel, shift=0, scale=1, chunk=512, splits=False, has_bias=False,
                             out_scale=(1.0, 1.0))
    return pl.pallas_call(
        kern,
        grid=rows.grid,
        in_specs=[rows.rows(d), rows.mod(d), _const2((1, d)), _const2((d, n))],
        out_specs=[rows.rows(half), rows.rows(half)],
        out_shape=[jax.ShapeDtypeStruct((bsz, t, half), BF16)] * 2,
        compiler_params=_params(("arbitrary", "arbitrary")),
        name="ml_in",
    )(r, mods, g, w_in)


def _na_qkv(rows, r, mods, g, w, b, q_scale):
    bsz, t, d = r.shape
    groups = d // LANE
    f = rows.first
    out_spec = pl.BlockSpec((None, groups, ROW_BLOCK, LANE), lambda bb, j: (bb, 0, j + f, 0))
    kern = functools.partial(_proj_kernel, shift=0, scale=1, chunk=256, splits=True, has_bias=True,
                             out_scale=(q_scale, 1.0, 1.0))
    return pl.pallas_call(
        kern,
        grid=rows.grid,
        in_specs=[rows.rows(d), rows.mod(d), _const2((1, d)), _const2((d, 3 * d)), _const2((1, 3 * d))],
        out_specs=[out_spec] * 3,
        out_shape=[jax.ShapeDtypeStruct((bsz, groups, t, LANE), BF16)] * 3,
        compiler_params=_params(("arbitrary", "arbitrary")),
        name="na_qkv",
    )(r, mods, g, w, b)


def _ml_proj_kernel(u_ref, cw_ref, cb_ref, wq_ref, wk_ref, wv_ref, gq_ref, gk_ref, gv_ref, gb_ref,
                    uc_ref, q_ref, k_ref, v_ref, g_ref, *, nblk, kscale):
    @pl.when(pl.program_id(1) == 0)
    def _():
        g_ref[...] = jnp.broadcast_to(gb_ref[...], g_ref.shape)

    cw = cw_ref[...]
    cb = cb_ref[...]
    taps = cw.shape[0]
    width = u_ref.shape[1]
    pad = 2 * HALO
    for j in range(nblk):
        r0 = j * ROW_BLOCK
        cur_b = u_ref[r0:r0 + ROW_BLOCK, :]
        cur = cur_b.astype(F32)
        if j in (0, 1):
            prev = jnp.zeros((HALO, width), F32)
        else:
            prev = u_ref[r0 - pad:r0, :].astype(F32)[HALO:, :]
        if j in (0, nblk - 1):
            nxt = jnp.zeros((HALO, width), F32)
        else:
            nxt = u_ref[r0 + ROW_BLOCK:r0 + ROW_BLOCK + pad, :].astype(F32)[:HALO, :]
        ext = jnp.concatenate([prev, cur, nxt], axis=0)
        acc = cb
        for t in range(taps):
            lo = HALO + t - taps // 2
            acc = acc + cw[t:t + 1, :] * ext[lo:lo + ROW_BLOCK, :]
        uc_b = _silu(acc).astype(BF16)
        uc_ref[r0:r0 + ROW_BLOCK, :] = uc_b
        qs, ks, vs = [], [], []
        for t in range(width // BD_TILE):
            sl = slice(t * BD_TILE, (t + 1) * BD_TILE)
            qs.append(_dot(uc_b[:, sl], wq_ref[t]))
            ks.append(_dot(uc_b[:, sl], wk_ref[t]) * kscale)
            vs.append(_dot(cur_b[:, sl], wv_ref[t]))
        q_b = jnp.concatenate(qs, axis=1).astype(BF16)
        k_b = jnp.concatenate(ks, axis=1).astype(BF16)
        v_b = jnp.concatenate(vs, axis=1).astype(BF16)
        q_ref[r0:r0 + ROW_BLOCK, :] = q_b
        k_ref[r0:r0 + ROW_BLOCK, :] = k_b
        v_ref[r0:r0 + ROW_BLOCK, :] = v_b
        g_ref[r0:r0 + ROW_BLOCK, :] += _dot(q_b, gq_ref[...]) + _dot(k_b, gk_ref[...]) + _dot(v_b, gv_ref[...])


def _ml_proj(u, conv_w, conv_b, wq_bd, wk_bd, wv_bd, gq, gk, gv, gb, dh, kscale):
    bsz, t, inner = u.shape
    heads = inner // dh
    tiles = dh // BD_TILE
    taps = conv_w.shape[0]
    col = lambda shape: pl.BlockSpec(shape, lambda b, h: (0, h))
    seq = pl.BlockSpec((None, t, dh), lambda b, h: (b, 0, h))
    bd = pl.BlockSpec((tiles, BD_TILE, BD_TILE), lambda b, h: (h, 0, 0))
    gw = pl.BlockSpec((dh, LANE), lambda b, h: (h, 0))
    kern = functools.partial(_ml_proj_kernel, nblk=t // ROW_BLOCK, kscale=kscale)
    return pl.pallas_call(
        kern,
        grid=(bsz, heads),
        in_specs=[seq, col((taps, dh)), col((1, dh)), bd, bd, bd, gw, gw, gw,
                  pl.BlockSpec((1, LANE), lambda b, h: (0, 0))],
        out_specs=[seq, seq, seq, seq, pl.BlockSpec((None, t, LANE), lambda b, h: (b, 0, 0))],
        out_shape=[jax.ShapeDtypeStruct((bsz, t, inner), BF16)] * 4 + [jax.ShapeDtypeStruct((bsz, t, LANE), F32)],
        compiler_params=_params(("arbitrary", "arbitrary")),
        name="ml_proj",
    )(u, conv_w, conv_b, wq_bd, wk_bd, wv_bd, gq, gk, gv, gb)


def _chunk_index(d, s, nblk):
    return jnp.where(d == 0, s, jnp.where(s == 0, 0, nblk - s))


def _ml_core_kernel(q_ref, k_ref, v_ref, g_ref, o_ref, ct_ref, n_ref, m_ref, hacc_ref, *, heads, nblk):
    h = pl.program_id(1)
    d = pl.program_id(2)
    s = pl.program_id(3)
    L = q_ref.shape[0]

    @pl.when(s == 0)
    def _():
        ct_ref[...] = jnp.zeros_like(ct_ref)
        n_ref[...] = jnp.zeros_like(n_ref)
        m_ref[...] = jnp.zeros_like(m_ref)

    g = g_ref[...]
    gt = g.T
    ci = d * 2 * heads + h
    cf = ci + heads
    lane = lax.broadcasted_iota(jnp.int32, g.shape, 1)
    sub = lax.broadcasted_iota(jnp.int32, gt.shape, 0)
    li_col = jnp.sum(jnp.where(lane == ci, g, 0.0), axis=1, keepdims=True)
    lf_col = _log_sigmoid(jnp.sum(jnp.where(lane == cf, g, 0.0), axis=1, keepdims=True))
    li_row = jnp.sum(jnp.where(sub == ci, gt, 0.0), axis=0, keepdims=True)
    lf_row = _log_sigmoid(jnp.sum(jnp.where(sub == cf, gt, 0.0), axis=0, keepdims=True))

    tt = lax.broadcasted_iota(jnp.int32, (L, L), 0)
    ss = lax.broadcasted_iota(jnp.int32, (L, L), 1)
    sgn = 1 - 2 * d
    seen = (ss - tt) * sgn <= 0
    b_col = jnp.sum(jnp.where(seen, lf_row, 0.0), axis=1, keepdims=True)
    b_row = jnp.sum(jnp.where((tt - ss) * sgn <= 0, lf_col, 0.0), axis=0, keepdims=True)
    b_last = jnp.sum(lf_row, axis=1, keepdims=True)

    m_prev = m_ref[...]
    log_d = jnp.where(seen, b_col - b_row + li_row, -jnp.inf)
    inter = b_col + m_prev
    m_t = jnp.maximum(inter, jnp.max(log_d, axis=1, keepdims=True))
    q = q_ref[...]
    k = k_ref[...]
    v = v_ref[...]
    sc = _dot_nt(q, k) * jnp.exp(log_d - m_t)
    w_inter = jnp.exp(inter - m_t)
    num = _dot(sc.astype(BF16), v) + w_inter * _dot(q, ct_ref[...].astype(BF16))
    qn = jnp.sum(q.astype(F32) * n_ref[...], axis=1, keepdims=True)
    den = jnp.sum(sc, axis=1, keepdims=True) + w_inter * qn
    hh = num / jnp.maximum(jnp.abs(den), jnp.exp(-m_t))

    g_row = b_last - b_row + li_row
    g_col = b_last - b_col + li_col
    m_new = jnp.maximum(b_last + m_prev, jnp.max(g_row, axis=1, keepdims=True))
    decay = jnp.exp(b_last + m_prev - m_new)
    wk = jnp.exp(g_col - m_new)
    vw = (v.astype(F32) * wk).astype(BF16)
    ct_ref[...] = decay * ct_ref[...] + _dot_tn(k, vw)
    n_ref[...] = decay * n_ref[...] + jnp.sum(k.astype(F32) * wk, axis=0, keepdims=True)
    m_ref[...] = m_new

    r0 = pl.multiple_of(_chunk_index(d, s, nblk) * L, L)

    @pl.when(d == 0)
    def _():
        hacc_ref[pl.ds(r0, L), :] = hh

    @pl.when(d == 1)
    def _():
        tot = hacc_ref[pl.ds(r0, L), :] + hh
        mu = jnp.mean(tot, axis=-1, keepdims=True)
        cen = tot - mu
        var = jnp.mean(cen * cen, axis=-1, keepdims=True)
        o_ref[...] = (cen * lax.rsqrt(var + EPS)).astype(BF16)


def _ml_core(q, k, v, gates, dh):
    bsz, t, inner = q.shape
    heads = inner // dh
    nblk = t // ROW_BLOCK
    blk = pl.BlockSpec((None, ROW_BLOCK, dh), lambda b, h, d, s: (b, _chunk_index(d, s, nblk), h))
    gate = pl.BlockSpec((None, ROW_BLOCK, LANE), lambda b, h, d, s: (b, _chunk_index(d, s, nblk), 0))
    out = pl.BlockSpec((None, ROW_BLOCK, dh), lambda b, h, d, s: (b, _chunk_index(1, s * d, nblk), h))
    kern = functools.partial(_ml_core_kernel, heads=heads, nblk=nblk)
    return pl.pallas_call(
        kern,
        grid=(bsz, heads, 2, nblk),
        in_specs=[blk, blk, blk, gate],
        out_specs=out,
        out_shape=jax.ShapeDtypeStruct((bsz, t, inner), BF16),
        scratch_shapes=[pltpu.VMEM((dh, dh), F32), pltpu.VMEM((1, dh), F32), pltpu.VMEM((1, 1), F32),
                        pltpu.VMEM((t, dh), F32)],
        compiler_params=_params(("arbitrary",) * 4),
        name="ml_core",
    )(q, k, v, gates)


def _ml_out_kernel(hn_ref, uc_ref, z_ref, r_ref, mod_ref, hg_ref, sk_ref, w_ref, o_ref):
    z = z_ref[...].astype(F32)
    y = (hn_ref[...].astype(F32) * hg_ref[...] + sk_ref[...] * uc_ref[...].astype(F32)) * _silu(z)
    o_ref[...] = r_ref[...] + mod_ref[2:3, :] * _dot(y.astype(BF16), w_ref[...])


def _ml_out(rows, hn, uc, z, r, mods, hnorm_g, skip, w_out):
    bsz, t, d = r.shape
    inner = hn.shape[-1]
    return pl.pallas_call(
        _ml_out_kernel,
        grid=rows.grid,
        in_specs=[rows.rows(inner), rows.rows(inner), rows.rows(inner), rows.rows(d), rows.mod(d),
                  _const2((1, inner)), _const2((1, inner)), _const2((inner, d))],
        out_specs=rows.rows(d),
        out_shape=jax.ShapeDtypeStruct((bsz, t, d), F32),
        compiler_params=_params(("arbitrary", "arbitrary")),
        name="ml_out",
    )(hn, uc, z, r, mods, hnorm_g, skip, w_out)


def _na_out_kernel(a_ref, r_ref, mod_ref, w_ref, b_ref, o_ref):
    o_ref[...] = r_ref[...] + mod_ref[2:3, :] * (_dot(a_ref[...], w_ref[...]) + b_ref[...])


def _na_out(rows, a, r, mods, w_out, b_out):
    bsz, t, d = r.shape
    return pl.pallas_call(
        _na_out_kernel,
        grid=rows.grid,
        in_specs=[rows.rows(d), rows.rows(d), rows.mod(d), _const2((d, d)), _const2((1, d))],
        out_specs=rows.rows(d),
        out_shape=jax.ShapeDtypeStruct((bsz, t, d), F32),
        compiler_params=_params(("arbitrary", "arbitrary")),
        name="na_out",
    )(a, r, mods, w_out, b_out)


def _na_attn_kernel(q_ref, k_ref, v_ref, b_ref, o_ref, *, first, ctx_len, rows, pairs):
    gw = q_ref.shape[1]
    t = pl.program_id(1) + first
    r = t - ctx_len // gw
    rs = jnp.clip(r - NA_KH // 2, 0, rows - NA_KH)
    kstart = pl.multiple_of(ctx_len + rs * gw, gw)
    nloc = NA_KH * gw
    lane = lax.broadcasted_iota(jnp.int32, (gw, LANE), 1)
    for hp in range(pairs):
        qp = q_ref[hp].astype(F32)
        kl = k_ref[hp, pl.ds(kstart, nloc), :]
        vl = v_ref[hp, pl.ds(kstart, nloc), :]
        kc = k_ref[hp, 0:ctx_len, :]
        vc = v_ref[hp, 0:ctx_len, :]
        outs = []
        for hh in range(LANE // NA_HEAD_DIM):
            mine = (lane >= hh * NA_HEAD_DIM) & (lane < (hh + 1) * NA_HEAD_DIM)
            qm = jnp.where(mine, qp, 0.0).astype(BF16)
            s_loc = _dot_nt(qm, kl) + b_ref[hp, :, hh * nloc:(hh + 1) * nloc]
            s_ctx = _dot_nt(qm, kc)
            m = jnp.maximum(jnp.max(s_loc, axis=1, keepdims=True), jnp.max(s_ctx, axis=1, keepdims=True))
            p_loc = jnp.exp(s_loc - m)
            p_ctx = jnp.exp(s_ctx - m)
            den = jnp.sum(p_loc, axis=1, keepdims=True) + jnp.sum(p_ctx, axis=1, keepdims=True)
            o = _dot(p_loc.astype(BF16), vl) + _dot(p_ctx.astype(BF16), vc)
            outs.append((mine, o / den))
        o_pair = outs[0][1]
        for mine, o in outs[1:]:
            o_pair = jnp.where(mine, o, o_pair)
        o_ref[:, hp * LANE:(hp + 1) * LANE] = o_pair.astype(BF16)


def _na_attn(q, k, v, bias, need_ctx, ctx_len):
    bsz, pairs, t, _ = q.shape
    d = pairs * LANE
    rows = (t - ctx_len) // GRID_W
    first = 0 if need_ctx else ctx_len // GRID_W
    steps = t // GRID_W - first
    nvar = bias.shape[0]

    def variant(tq):
        r = tq - ctx_len // GRID_W
        rs = jnp.clip(r - NA_KH // 2, 0, rows - NA_KH)
        return jnp.where(r < 0, nvar - 1, rs - r + NA_KH - 1)

    whole = pl.BlockSpec((None, pairs, t, LANE), lambda b, j: (b, 0, 0, 0))
    kern = functools.partial(_na_attn_kernel, first=first, ctx_len=ctx_len, rows=rows, pairs=pairs)
    return pl.pallas_call(
        kern,
        grid=(bsz, steps),
        in_specs=[pl.BlockSpec((None, pairs, GRID_W, LANE), lambda b, j: (b, 0, j + first, 0)), whole, whole,
                  pl.BlockSpec((None, pairs, GRID_W, bias.shape[-1]), lambda b, j: (variant(j + first), 0, 0, 0))],
        out_specs=pl.BlockSpec((None, GRID_W, d), lambda b, j: (b, j + first, 0)),
        out_shape=jax.ShapeDtypeStruct((bsz, t, d), BF16),
        compiler_params=_params(("arbitrary", "arbitrary")),
        name="na_attn",
    )(q, k, v, bias)


def _na_bias_table(rpb):
    heads = rpb.shape[0]
    col = jnp.arange(GRID_W)
    col_start = jnp.clip(col - NA_KW // 2, 0, GRID_W - NA_KW)
    col_ok = (col[None, :] >= col_start[:, None]) & (col[None, :] < col_start[:, None] + NA_KW)
    dc_idx = jnp.clip(col[None, :] - col[:, None] + NA_KW - 1, 0, 2 * NA_KW - 2)
    full = jnp.where(col_ok, rpb.astype(F32)[:, :, dc_idx], -jnp.inf)
    var = jnp.stack([full[:, o:o + NA_KH] for o in range(NA_KH)])
    var = jnp.concatenate([var, jnp.full_like(var[:1], -jnp.inf)], axis=0)
    var = var.transpose(0, 1, 3, 2, 4).reshape(NA_KH + 1, heads, GRID_W, NA_KH * GRID_W)
    per = LANE // NA_HEAD_DIM
    var = var.reshape(NA_KH + 1, heads // per, per, GRID_W, NA_KH * GRID_W).transpose(0, 1, 3, 2, 4)
    return var.reshape(NA_KH + 1, heads // per, GRID_W, per * NA_KH * GRID_W)


def _ffn_kernel(x_ref, xp_ref, xn_ref, mod_ref, g_ref, wu_ref, bu_ref, cw_ref, cb_ref, wd_ref, bd_ref,
                *rest, first, nblk_total, final):
    if final:
        fg_ref, o_ref = rest
    else:
        (o_ref,) = rest
    j = pl.program_id(1) + first
    x = x_ref[...]
    ext = jnp.concatenate([xp_ref[...], x, xn_ref[...]], axis=0)
    h = _norm_mod(ext, g_ref[...], mod_ref[3:4, :], mod_ref[4:5, :]).astype(BF16)
    row = lax.broadcasted_iota(jnp.int32, (ext.shape[0], 1), 0)
    has_prev = j >= 2
    has_next = (j >= 1) & (j < nblk_total - 1)
    keep = ((row >= HALO) | has_prev) & ((row < HALO + ROW_BLOCK) | has_next)
    hidden = wd_ref.shape[0]
    taps = cw_ref.shape[0]
    acc = jnp.zeros(x.shape, F32)
    for c in range(hidden // FF_CHUNK):
        halves = []
        for base in (c * FF_CHUNK, hidden + c * FF_CHUNK):
            sl = slice(base, base + FF_CHUNK)
            u = jnp.where(keep, _dot(h, wu_ref[:, sl]) + bu_ref[:, sl], 0.0)
            y = cb_ref[:, sl]
            for t in range(taps):
                lo = HALO + t - taps // 2
                y = y + cw_ref[t:t + 1, sl] * u[lo:lo + ROW_BLOCK, :]
            halves.append(y)
        act = (halves[0] * _silu(halves[1])).astype(BF16)
        acc = acc + _dot(act, wd_ref[c * FF_CHUNK:(c + 1) * FF_CHUNK, :])
    o = x + mod_ref[5:6, :] * (acc + bd_ref[...])
    if final:
        o = o * lax.rsqrt(jnp.mean(o * o, axis=-1, keepdims=True) + EPS) * fg_ref[...]
    o_ref[...] = o


def _ffn(rows, r, mods, g, w_up, b_up, conv_w, conv_b, w_down, b_down, final_g=None):
    bsz, t, d = r.shape
    f = rows.first
    nblk_total = rows.nblk_total
    per = ROW_BLOCK // HALO
    last = t // HALO - 1
    prev = pl.BlockSpec((None, HALO, d), lambda b, j: (b, jnp.maximum((j + f) * per - 1, f * per), 0))
    nxt = pl.BlockSpec((None, HALO, d), lambda b, j: (b, jnp.minimum((j + f + 1) * per, last), 0))
    hidden2 = w_up.shape[1]
    final = final_g is not None
    in_specs = [rows.rows(d), prev, nxt, rows.mod(d), _const2((1, d)), _const2((d, hidden2)), _const2((1, hidden2)),
                _const2((conv_w.shape[0], hidden2)), _const2((1, hidden2)), _const2((hidden2 // 2, d)),
                _const2((1, d))]
    args = [r, r, r, mods, g, w_up, b_up, conv_w, conv_b, w_down, b_down]
    if final:
        in_specs.append(_const2((1, d)))
        args.append(final_g)
        out_spec = pl.BlockSpec((None, ROW_BLOCK, d), lambda b, j: (b, j, 0))
        out_shape = jax.ShapeDtypeStruct((bsz, t - f * ROW_BLOCK, d), F32)
    else:
        out_spec = rows.rows(d)
        out_shape = jax.ShapeDtypeStruct((bsz, t, d), F32)
    kern = functools.partial(_ffn_kernel, first=f, nblk_total=nblk_total, final=final)
    return pl.pallas_call(
        kern,
        grid=rows.grid,
        in_specs=in_specs,
        out_specs=out_spec,
        out_shape=out_shape,
        compiler_params=_params(("arbitrary", "arbitrary")),
        name="ffn",
    )(*args)


def _block_diag_tiles(w):
    nb, bs, _ = w.shape
    per = BD_TILE // bs
    wt = w.reshape(nb // per, per, bs, bs)
    eye = jnp.eye(per, dtype=w.dtype)
    dense = wt[:, :, :, None, :] * eye[None, :, None, :, None]
    return dense.reshape(nb // per, BD_TILE, BD_TILE).astype(BF16)


def _gate_cols(gate_w, gate_b, inner):
    dirs, _, ng = gate_w.shape
    g = gate_w.transpose(1, 0, 2).reshape(3 * inner, dirs * ng)
    g = jnp.pad(g, ((0, 0), (0, LANE - dirs * ng))).astype(BF16)
    gb = jnp.pad(gate_b.reshape(1, dirs * ng), ((0, 0), (0, LANE - dirs * ng)))
    return g[:inner], g[inner:2 * inner], g[2 * inner:], gb


def kernel(x, c, ctx, c_ctx, ada_w, ada_b, norm1_g, norm2_g, final_g, ml_w_in, ml_conv_w, ml_conv_b, ml_wq, ml_wk, ml_wv, ml_gate_w, ml_gate_b, ml_skip, ml_hnorm_g, ml_w_out, na_w_qkv, na_b_qkv, na_rpb, na_w_out, na_b_out, ff_w_up, ff_b_up, ff_conv_w, ff_conv_b, ff_w_down, ff_b_down):
    bsz, seq, d = x.shape
    ctx_len = ctx.shape[1]
    depth = ada_w.shape[0]
    assert ctx_len == ROW_BLOCK and seq % ROW_BLOCK == 0 and seq % GRID_W == 0
    assert bsz + 1 <= MOD_ROWS
    t = ctx_len + seq
    nblk = t // ROW_BLOCK

    c_all = jnp.zeros((MOD_ROWS, d), F32).at[:bsz].set(c).at[bsz].set(c_ctx)
    mods = _ada(c_all, ada_w, ada_b).reshape(depth, MOD_ROWS, 6, d)

    r = jnp.concatenate([ctx, x], axis=1)
    row2 = lambda a: a.reshape(1, -1)
    out = None
    for i in range(depth):
        need_ctx = i < depth - 1
        j = i // 2
        all_rows = _Rows(bsz, nblk, i, 0)
        live_rows = all_rows if need_ctx else _Rows(bsz, nblk, i, 1)
        if i % 2 == 0:
            inner = ml_w_in.shape[2] // 2
            dh = inner // ML_HEADS
            u, z = _ml_in(all_rows, r, mods, row2(norm1_g[i]), ml_w_in[j].astype(BF16))
            gq, gk, gv, gb = _gate_cols(ml_gate_w[j], ml_gate_b[j], inner)
            uc, q, k, v, gates = _ml_proj(u, ml_conv_w[j], row2(ml_conv_b[j]), _block_diag_tiles(ml_wq[j]),
                                          _block_diag_tiles(ml_wk[j]), _block_diag_tiles(ml_wv[j]),
                                          gq, gk, gv, gb, dh, dh ** -0.5)
            hn = _ml_core(q, k, v, gates, dh)
            r = _ml_out(live_rows, hn, uc, z, r, mods, row2(ml_hnorm_g[j]), row2(ml_skip[j]),
                        ml_w_out[j].astype(BF16))
        else:
            q, k, v = _na_qkv(all_rows, r, mods, row2(norm1_g[i]), na_w_qkv[j].astype(BF16), row2(na_b_qkv[j]),
                              NA_HEAD_DIM ** -0.5)
            a = _na_attn(q, k, v, _na_bias_table(na_rpb[j]), need_ctx, ctx_len)
            r = _na_out(live_rows, a, r, mods, na_w_out[j].astype(BF16), row2(na_b_out[j]))
        ffn_args = (live_rows, r, mods, row2(norm2_g[i]), ff_w_up[i].astype(BF16), row2(ff_b_up[i]), ff_conv_w[i],
                    row2(ff_conv_b[i]), ff_w_down[i].astype(BF16), row2(ff_b_down[i]))
        if need_ctx:
            r = _ffn(*ffn_args)
        else:
            out = _ffn(*ffn_args, final_g=row2(final_g))
    return out
```

```python
import functools

import jax
import jax.numpy as jnp
from jax import lax
from jax.experimental import pallas as pl
from jax.experimental.pallas import tpu as pltpu

F32 = jnp.float32
BF16 = jnp.bfloat16
EPS = 1e-6

GRID_W = 64
ML_HEADS = 4
ML_QKV_BLOCK = 4
NA_HEAD_DIM = 64
NA_KH = 8
NA_KW = 16

ROW_BLOCK = 256
HALO = 8
BD_TILE = 256
LANE = 128
NA_GROUP = 256
FF_CHUNK = 256
MOD_ROWS = 16
VMEM_LIMIT = 56 * 1024 * 1024


def _dot(a, b):
    return jnp.dot(a, b, preferred_element_type=F32)


def _dot_nt(a, b):
    return lax.dot_general(a, b, (((1,), (1,)), ((), ())), preferred_element_type=F32)


def _dot_tn(a, b):
    return lax.dot_general(a, b, (((0,), (0,)), ((), ())), preferred_element_type=F32)


def _silu(x):
    return x * jax.nn.sigmoid(x)


def _log_sigmoid(x):
    return jnp.minimum(x, 0.0) - jnp.log1p(jnp.exp(-jnp.abs(x)))


def _norm_mod(x, g, shift, scale):
    y = x * lax.rsqrt(jnp.mean(x * x, axis=-1, keepdims=True) + EPS) * g
    return y * (1.0 + scale) + shift


def _params(sem):
    return pltpu.CompilerParams(dimension_semantics=sem, vmem_limit_bytes=VMEM_LIMIT)


def _resident(shape, index_map):
    return pl.BlockSpec(shape, index_map, pipeline_mode=pl.Buffered(1))


def _ada_kernel(c_ref, w_ref, b_ref, o_ref):
    s = _silu(c_ref[...]).astype(BF16)
    o_ref[...] = _dot(s, w_ref[...].astype(BF16)) + b_ref[...]


def _ada(c_all, ada_w, ada_b):
    depth, d, d6 = ada_w.shape
    n = d6 // d
    return pl.pallas_call(
        _ada_kernel,
        grid=(depth, n),
        in_specs=[
            pl.BlockSpec((MOD_ROWS, d), lambda l, j: (0, 0)),
            pl.BlockSpec((None, d, d), lambda l, j: (l, 0, j)),
            pl.BlockSpec((None, 1, d), lambda l, j: (l, 0, j)),
        ],
        out_specs=pl.BlockSpec((None, MOD_ROWS, d), lambda l, j: (l, 0, j)),
        out_shape=jax.ShapeDtypeStruct((depth, MOD_ROWS, d6), F32),
        compiler_params=_params(("arbitrary", "arbitrary")),
        name="ada",
    )(c_all, ada_w, ada_b.reshape(depth, 1, d6))


class _Rows:
    def __init__(self, bsz, nblk_total, layer, first_blk):
        self.bsz = bsz
        self.nblk_total = nblk_total
        self.layer = layer
        self.first = first_blk
        self.grid = (bsz, nblk_total - first_blk)

    def rows(self, width, rows=ROW_BLOCK):
        f = self.first
        return pl.BlockSpec((None, rows, width), lambda b, j: (b, j + f, 0))

    def mod(self, d):
        f, bsz, layer = self.first, self.bsz, self.layer
        return pl.BlockSpec((None, None, 6, d), lambda b, j: (layer, jnp.where(j + f == 0, bsz, b), 0, 0))


def _const2(shape):
    return _resident(shape, lambda b, j: (0,) * len(shape))


def _proj_kernel(x_ref, mod_ref, g_ref, w_ref, *rest, shift, scale, chunk, splits, has_bias, out_scale):
    if has_bias:
        b_ref, out_refs = rest[0], rest[1:]
    else:
        b_ref, out_refs = None, rest
    h = _norm_mod(x_ref[...], g_ref[...], mod_ref[shift:shift + 1, :], mod_ref[scale:scale + 1, :]).astype(BF16)
    n = w_ref.shape[1]
    per_out = n // len(out_refs)
    for c in range(n // chunk):
        y = _dot(h, w_ref[:, c * chunk:(c + 1) * chunk])
        if has_bias:
            y = y + b_ref[:, c * chunk:(c + 1) * chunk]
        oi, off = divmod(c * chunk, per_out)
        if out_scale[oi] != 1.0:
            y = y * out_scale[oi]
        y = y.astype(BF16)
        o_ref = out_refs[oi]
        if splits:
            for t in range(chunk // splits):
                o_ref[(off + t * splits) // splits] = y[:, t * splits:(t + 1) * splits]
        else:
            o_ref[:, off:off + chunk] = y


def _ml_in(rows, r, mods, g, w_in):
    bsz, t, d = r.shape
    n = w_in.shape[1]
    half = n // 2
    kern = functools.partial(_proj_kernel, shift=0, scale=1, chunk=512, splits=0, has_bias=False,
                             out_scale=(1.0, 1.0))
    return pl.pallas_call(
        kern,
        grid=rows.grid,
        in_specs=[rows.rows(d), rows.mod(d), _const2((1, d)), _const2((d, n))],
        out_specs=[rows.rows(half), rows.rows(half)],
        out_shape=[jax.ShapeDtypeStruct((bsz, t, half), BF16)] * 2,
        compiler_params=_params(("arbitrary", "arbitrary")),
        name="ml_in",
    )(r, mods, g, w_in)


def _na_qkv(rows, r, mods, g, w, b, q_scale):
    bsz, t, d = r.shape
    groups = d // NA_GROUP
    f = rows.first
    out_spec = pl.BlockSpec((None, groups, ROW_BLOCK, NA_GROUP), lambda bb, j: (bb, 0, j + f, 0))
    kern = functools.partial(_proj_kernel, shift=0, scale=1, chunk=NA_GROUP, splits=NA_GROUP, has_bias=True,
                             out_scale=(q_scale, 1.0, 1.0))
    return pl.pallas_call(
        kern,
        grid=rows.grid,
        in_specs=[rows.rows(d), rows.mod(d), _const2((1, d)), _const2((d, 3 * d)), _const2((1, 3 * d))],
        out_specs=[out_spec] * 3,
        out_shape=[jax.ShapeDtypeStruct((bsz, groups, t, NA_GROUP), BF16)] * 3,
        compiler_params=_params(("arbitrary", "arbitrary")),
        name="na_qkv",
    )(r, mods, g, w, b)


def _ml_proj_kernel(u_ref, cw_ref, cb_ref, wq_ref, wk_ref, wv_ref, gq_ref, gk_ref, gv_ref, gb_ref,
                    uc_ref, q_ref, k_ref, v_ref, g_ref, *, nblk, kscale):
    @pl.when(pl.program_id(1) == 0)
    def _():
        g_ref[...] = jnp.broadcast_to(gb_ref[...], g_ref.shape)

    cw = cw_ref[...]
    cb = cb_ref[...]
    taps = cw.shape[0]
    width = u_ref.shape[1]
    pad = 2 * HALO
    for j in range(nblk):
        r0 = j * ROW_BLOCK
        cur_b = u_ref[r0:r0 + ROW_BLOCK, :]
        cur = cur_b.astype(F32)
        if j in (0, 1):
            prev = jnp.zeros((HALO, width), F32)
        else:
            prev = u_ref[r0 - pad:r0, :].astype(F32)[HALO:, :]
        if j in (0, nblk - 1):
            nxt = jnp.zeros((HALO, width), F32)
        else:
            nxt = u_ref[r0 + ROW_BLOCK:r0 + ROW_BLOCK + pad, :].astype(F32)[:HALO, :]
        ext = jnp.concatenate([prev, cur, nxt], axis=0)
        acc = cb
        for t in range(taps):
            lo = HALO + t - taps // 2
            acc = acc + cw[t:t + 1, :] * ext[lo:lo + ROW_BLOCK, :]
        uc_b = _silu(acc).astype(BF16)
        uc_ref[r0:r0 + ROW_BLOCK, :] = uc_b
        qs, ks, vs = [], [], []
        for t in range(width // BD_TILE):
            sl = slice(t * BD_TILE, (t + 1) * BD_TILE)
            qs.append(_dot(uc_b[:, sl], wq_ref[t]))
            ks.append(_dot(uc_b[:, sl], wk_ref[t]) * kscale)
            vs.append(_dot(cur_b[:, sl], wv_ref[t]))
        q_b = jnp.concatenate(qs, axis=1).astype(BF16)
        k_b = jnp.concatenate(ks, axis=1).astype(BF16)
        v_b = jnp.concatenate(vs, axis=1).astype(BF16)
        q_ref[r0:r0 + ROW_BLOCK, :] = q_b
        k_ref[r0:r0 + ROW_BLOCK, :] = k_b
        v_ref[r0:r0 + ROW_BLOCK, :] = v_b
        g_ref[r0:r0 + ROW_BLOCK, :] += _dot(q_b, gq_ref[...]) + _dot(k_b, gk_ref[...]) + _dot(v_b, gv_ref[...])


def _ml_proj(u, conv_w, conv_b, wq_bd, wk_bd, wv_bd, gq, gk, gv, gb, dh, kscale):
    bsz, t, inner = u.shape
    heads = inner // dh
    tiles = dh // BD_TILE
    taps = conv_w.shape[0]
    col = lambda shape: pl.BlockSpec(shape, lambda b, h: (0, h))
    seq = pl.BlockSpec((None, t, dh), lambda b, h: (b, 0, h))
    bd = pl.BlockSpec((tiles, BD_TILE, BD_TILE), lambda b, h: (h, 0, 0))
    gw = pl.BlockSpec((dh, LANE), lambda b, h: (h, 0))
    kern = functools.partial(_ml_proj_kernel, nblk=t // ROW_BLOCK, kscale=kscale)
    return pl.pallas_call(
        kern,
        grid=(bsz, heads),
        in_specs=[seq, col((taps, dh)), col((1, dh)), bd, bd, bd, gw, gw, gw,
                  pl.BlockSpec((1, LANE), lambda b, h: (0, 0))],
        out_specs=[seq, seq, seq, seq, pl.BlockSpec((None, t, LANE), lambda b, h: (b, 0, 0))],
        out_shape=[jax.ShapeDtypeStruct((bsz, t, inner), BF16)] * 4 + [jax.ShapeDtypeStruct((bsz, t, LANE), F32)],
        compiler_params=_params(("arbitrary", "arbitrary")),
        name="ml_proj",
    )(u, conv_w, conv_b, wq_bd, wk_bd, wv_bd, gq, gk, gv, gb)


def _chunk_index(d, s, nblk):
    return jnp.where(d == 0, s, jnp.where(s == 0, 0, nblk - s))


def _ml_core_kernel(q_ref, k_ref, v_ref, g_ref, o_ref, ct_ref, n_ref, m_ref, hacc_ref, *, heads, nblk):
    h = pl.program_id(1)
    d = pl.program_id(2)
    s = pl.program_id(3)
    L = q_ref.shape[0]

    @pl.when(s == 0)
    def _():
        ct_ref[...] = jnp.zeros_like(ct_ref)
        n_ref[...] = jnp.zeros_like(n_ref)
        m_ref[...] = jnp.zeros_like(m_ref)

    g = g_ref[...]
    gt = g.T
    ci = d * 2 * heads + h
    cf = ci + heads
    lane = lax.broadcasted_iota(jnp.int32, g.shape, 1)
    sub = lax.broadcasted_iota(jnp.int32, gt.shape, 0)
    li_col = jnp.sum(jnp.where(lane == ci, g, 0.0), axis=1, keepdims=True)
    lf_col = _log_sigmoid(jnp.sum(jnp.where(lane == cf, g, 0.0), axis=1, keepdims=True))
    li_row = jnp.sum(jnp.where(sub == ci, gt, 0.0), axis=0, keepdims=True)
    lf_row = _log_sigmoid(jnp.sum(jnp.where(sub == cf, gt, 0.0), axis=0, keepdims=True))

    tt = lax.broadcasted_iota(jnp.int32, (L, L), 0)
    ss = lax.broadcasted_iota(jnp.int32, (L, L), 1)
    sgn = 1 - 2 * d
    seen = (ss - tt) * sgn <= 0
    b_col = jnp.sum(jnp.where(seen, lf_row, 0.0), axis=1, keepdims=True)
    b_row = jnp.sum(jnp.where((tt - ss) * sgn <= 0, lf_col, 0.0), axis=0, keepdims=True)
    b_last = jnp.sum(lf_row, axis=1, keepdims=True)

    m_prev = m_ref[...]
    log_d = jnp.where(seen, b_col - b_row + li_row, -jnp.inf)
    inter = b_col + m_prev
    m_t = jnp.maximum(inter, jnp.max(log_d, axis=1, keepdims=True))
    q = q_ref[...]
    k = k_ref[...]
    v = v_ref[...]
    sc = _dot_nt(q, k) * jnp.exp(log_d - m_t)
    w_inter = jnp.exp(inter - m_t)
    num = _dot(sc.astype(BF16), v) + w_inter * _dot(q, ct_ref[...].astype(BF16))
    qn = jnp.sum(q.astype(F32) * n_ref[...], axis=1, keepdims=True)
    den = jnp.sum(sc, axis=1, keepdims=True) + w_inter * qn
    hh = num / jnp.maximum(jnp.abs(den), jnp.exp(-m_t))

    g_row = b_last - b_row + li_row
    g_col = b_last - b_col + li_col
    m_new = jnp.maximum(b_last + m_prev, jnp.max(g_row, axis=1, keepdims=True))
    decay = jnp.exp(b_last + m_prev - m_new)
    wk = jnp.exp(g_col - m_new)
    vw = (v.astype(F32) * wk).astype(BF16)
    ct_ref[...] = decay * ct_ref[...] + _dot_tn(k, vw)
    n_ref[...] = decay * n_ref[...] + jnp.sum(k.astype(F32) * wk, axis=0, keepdims=True)
    m_ref[...] = m_new

    r0 = pl.multiple_of(_chunk_index(d, s, nblk) * L, L)

    @pl.when(d == 0)
    def _():
        hacc_ref[pl.ds(r0, L), :] = hh

    @pl.when(d == 1)
    def _():
        tot = hacc_ref[pl.ds(r0, L), :] + hh
        mu = jnp.mean(tot, axis=-1, keepdims=True)
        cen = tot - mu
        var = jnp.mean(cen * cen, axis=-1, keepdims=True)
        o_ref[...] = (cen * lax.rsqrt(var + EPS)).astype(BF16)


def _ml_core(q, k, v, gates, dh):
    bsz, t, inner = q.shape
    heads = inner // dh
    nblk = t // ROW_BLOCK
    blk = pl.BlockSpec((None, ROW_BLOCK, dh), lambda b, h, d, s: (b, _chunk_index(d, s, nblk), h))
    gate = pl.BlockSpec((None, ROW_BLOCK, LANE), lambda b, h, d, s: (b, _chunk_index(d, s, nblk), 0))
    out = pl.BlockSpec((None, ROW_BLOCK, dh), lambda b, h, d, s: (b, _chunk_index(1, s * d, nblk), h))
    kern = functools.partial(_ml_core_kernel, heads=heads, nblk=nblk)
    return pl.pallas_call(
        kern,
        grid=(bsz, heads, 2, nblk),
        in_specs=[blk, blk, blk, gate],
        out_specs=out,
        out_shape=jax.ShapeDtypeStruct((bsz, t, inner), BF16),
        scratch_shapes=[pltpu.VMEM((dh, dh), F32), pltpu.VMEM((1, dh), F32), pltpu.VMEM((1, 1), F32),
                        pltpu.VMEM((t, dh), F32)],
        compiler_params=_params(("arbitrary",) * 4),
        name="ml_core",
    )(q, k, v, gates)


def _ml_out_kernel(hn_ref, uc_ref, z_ref, r_ref, mod_ref, hg_ref, sk_ref, w_ref, o_ref):
    z = z_ref[...].astype(F32)
    y = (hn_ref[...].astype(F32) * hg_ref[...] + sk_ref[...] * uc_ref[...].astype(F32)) * _silu(z)
    o_ref[...] = r_ref[...] + mod_ref[2:3, :] * _dot(y.astype(BF16), w_ref[...])


def _ml_out(rows, hn, uc, z, r, mods, hnorm_g, skip, w_out):
    bsz, t, d = r.shape
    inner = hn.shape[-1]
    return pl.pallas_call(
        _ml_out_kernel,
        grid=rows.grid,
        in_specs=[rows.rows(inner), rows.rows(inner), rows.rows(inner), rows.rows(d), rows.mod(d),
                  _const2((1, inner)), _const2((1, inner)), _const2((inner, d))],
        out_specs=rows.rows(d),
        out_shape=jax.ShapeDtypeStruct((bsz, t, d), F32),
        compiler_params=_params(("arbitrary", "arbitrary")),
        name="ml_out",
    )(hn, uc, z, r, mods, hnorm_g, skip, w_out)


def _na_out_kernel(a_ref, r_ref, mod_ref, w_ref, b_ref, o_ref):
    o_ref[...] = r_ref[...] + mod_ref[2:3, :] * (_dot(a_ref[...], w_ref[...]) + b_ref[...])


def _na_out(rows, a, r, mods, w_out, b_out):
    bsz, t, d = r.shape
    return pl.pallas_call(
        _na_out_kernel,
        grid=rows.grid,
        in_specs=[rows.rows(d), rows.rows(d), rows.mod(d), _const2((d, d)), _const2((1, d))],
        out_specs=rows.rows(d),
        out_shape=jax.ShapeDtypeStruct((bsz, t, d), F32),
        compiler_params=_params(("arbitrary", "arbitrary")),
        name="na_out",
    )(a, r, mods, w_out, b_out)


def _na_attn_kernel(q_ref, k_ref, v_ref, b_ref, o_ref, *, first, ctx_len, rows, groups):
    gw = q_ref.shape[1]
    per = NA_GROUP // NA_HEAD_DIM
    t = pl.program_id(1) + first
    r = t - ctx_len // gw
    rs = jnp.clip(r - NA_KH // 2, 0, rows - NA_KH)
    kstart = pl.multiple_of(ctx_len + rs * gw, gw)
    nloc = NA_KH * gw
    lane = lax.broadcasted_iota(jnp.int32, (gw, NA_GROUP), 1)
    mine = [(lane >= h * NA_HEAD_DIM) & (lane < (h + 1) * NA_HEAD_DIM) for h in range(per)]

    def scores(g):
        qg = q_ref[g].astype(F32)
        qm = jnp.concatenate([jnp.where(mine[h], qg, 0.0) for h in range(per)], axis=0).astype(BF16)
        return _dot_nt(qm, k_ref[g, pl.ds(kstart, nloc), :]), _dot_nt(qm, k_ref[g, 0:ctx_len, :])

    ahead = scores(0)
    for g in range(groups):
        s_loc, s_ctx = ahead
        if g + 1 < groups:
            ahead = scores(g + 1)
        s_loc = s_loc + b_ref[g]
        m = jnp.maximum(jnp.max(s_loc, axis=1, keepdims=True), jnp.max(s_ctx, axis=1, keepdims=True))
        p_loc = jnp.exp(s_loc - m)
        p_ctx = jnp.exp(s_ctx - m)
        den = jnp.sum(p_loc, axis=1, keepdims=True) + jnp.sum(p_ctx, axis=1, keepdims=True)
        o = (_dot(p_loc.astype(BF16), v_ref[g, pl.ds(kstart, nloc), :])
             + _dot(p_ctx.astype(BF16), v_ref[g, 0:ctx_len, :])) / den
        og = jnp.where(mine[0], o[0:gw], 0.0)
        for h in range(1, per):
            og = jnp.where(mine[h], o[h * gw:(h + 1) * gw], og)
        o_ref[:, g * NA_GROUP:(g + 1) * NA_GROUP] = og.astype(BF16)


def _na_attn(q, k, v, bias, need_ctx, ctx_len):
    bsz, groups, t, _ = q.shape
    d = groups * NA_GROUP
    rows = (t - ctx_len) // GRID_W
    first = 0 if need_ctx else ctx_len // GRID_W
    steps = t // GRID_W - first
    nvar = bias.shape[0]

    def variant(tq):
        r = tq - ctx_len // GRID_W
        rs = jnp.clip(r - NA_KH // 2, 0, rows - NA_KH)
        return jnp.where(r < 0, nvar - 1, rs - r + NA_KH - 1)

    whole = pl.BlockSpec((None, groups, t, NA_GROUP), lambda b, j: (b, 0, 0, 0))
    kern = functools.partial(_na_attn_kernel, first=first, ctx_len=ctx_len, rows=rows, groups=groups)
    return pl.pallas_call(
        kern,
        grid=(bsz, steps),
        in_specs=[pl.BlockSpec((None, groups, GRID_W, NA_GROUP), lambda b, j: (b, 0, j + first, 0)), whole, whole,
                  pl.BlockSpec((None,) + bias.shape[1:], lambda b, j: (variant(j + first), 0, 0, 0))],
        out_specs=pl.BlockSpec((None, GRID_W, d), lambda b, j: (b, j + first, 0)),
        out_shape=jax.ShapeDtypeStruct((bsz, t, d), BF16),
        compiler_params=_params(("arbitrary", "arbitrary")),
        name="na_attn",
    )(q, k, v, bias)


def _na_bias_table(rpb):
    heads = rpb.shape[0]
    col = jnp.arange(GRID_W)
    col_start = jnp.clip(col - NA_KW // 2, 0, GRID_W - NA_KW)
    col_ok = (col[None, :] >= col_start[:, None]) & (col[None, :] < col_start[:, None] + NA_KW)
    dc_idx = jnp.clip(col[None, :] - col[:, None] + NA_KW - 1, 0, 2 * NA_KW - 2)
    full = jnp.where(col_ok, rpb.astype(F32)[:, :, dc_idx], -jnp.inf)
    var = jnp.stack([full[:, o:o + NA_KH] for o in range(NA_KH)])
    var = jnp.concatenate([var, jnp.full_like(var[:1], -jnp.inf)], axis=0)
    var = var.transpose(0, 1, 3, 2, 4).reshape(NA_KH + 1, heads, GRID_W, NA_KH * GRID_W)
    per = NA_GROUP // NA_HEAD_DIM
    return var.reshape(NA_KH + 1, heads // per, per * GRID_W, NA_KH * GRID_W)


def _ffn_kernel(x_ref, xp_ref, xn_ref, mod_ref, g_ref, wu_ref, bu_ref, cw_ref, cb_ref, wd_ref, bd_ref,
                *rest, first, nblk_total, final):
    if final:
        fg_ref, o_ref, h_buf, acc_ref, *u_bufs = rest
    else:
        o_ref, h_buf, acc_ref, *u_bufs = rest
    j = pl.program_id(1) + first
    x = x_ref[...]
    ext = jnp.concatenate([xp_ref[...], x, xn_ref[...]], axis=0)
    h_buf[...] = _norm_mod(ext, g_ref[...], mod_ref[3:4, :], mod_ref[4:5, :]).astype(BF16)
    row = lax.broadcasted_iota(jnp.int32, (ext.shape[0], 1), 0)
    has_prev = j >= 2
    has_next = (j >= 1) & (j < nblk_total - 1)
    keep = ((row >= HALO) | has_prev) & ((row < HALO + ROW_BLOCK) | has_next)
    nchunk = wd_ref.shape[0]
    taps = cw_ref.shape[1]
    acc_ref[...] = jnp.zeros_like(acc_ref)

    def up(c, slot):
        for half in range(2):
            cc = half * nchunk + c
            u_bufs[2 * slot + half][...] = jnp.where(keep, _dot(h_buf[...], wu_ref[cc]) + bu_ref[cc], 0.0)

    def down(c, slot):
        halves = []
        for half in range(2):
            cc = half * nchunk + c
            y = cb_ref[cc]
            for t in range(taps):
                lo = HALO + t - taps // 2
                y = y + cw_ref[cc, t:t + 1, :] * u_bufs[2 * slot + half][lo:lo + ROW_BLOCK, :]
            halves.append(y)
        act = (halves[0] * _silu(halves[1])).astype(BF16)
        acc_ref[...] += _dot(act, wd_ref[c])

    assert nchunk % 2 == 1
    up(0, 0)

    def body(i, carry):
        c = 2 * i
        up(c + 1, 1)
        down(c, 0)
        up(c + 2, 0)
        down(c + 1, 1)
        return carry

    lax.fori_loop(0, nchunk // 2, body, 0)
    down(nchunk - 1, 0)
    o = x + mod_ref[5:6, :] * (acc_ref[...] + bd_ref[...])
    if final:
        o = o * lax.rsqrt(jnp.mean(o * o, axis=-1, keepdims=True) + EPS) * fg_ref[...]
    o_ref[...] = o


def _ffn(rows, r, mods, g, w_up, b_up, conv_w, conv_b, w_down, b_down, final_g=None):
    bsz, t, d = r.shape
    f = rows.first
    nblk_total = rows.nblk_total
    per = ROW_BLOCK // HALO
    last = t // HALO - 1
    prev = pl.BlockSpec((None, HALO, d), lambda b, j: (b, jnp.maximum((j + f) * per - 1, f * per), 0))
    nxt = pl.BlockSpec((None, HALO, d), lambda b, j: (b, jnp.minimum((j + f + 1) * per, last), 0))
    hidden2 = w_up.shape[1]
    nc2 = hidden2 // FF_CHUNK
    taps = conv_w.shape[0]
    final = final_g is not None
    w_up = w_up.reshape(d, nc2, FF_CHUNK).transpose(1, 0, 2)
    b_up = b_up.reshape(nc2, 1, FF_CHUNK)
    conv_w = conv_w.reshape(taps, nc2, FF_CHUNK).transpose(1, 0, 2)
    conv_b = conv_b.reshape(nc2, 1, FF_CHUNK)
    w_down = w_down.reshape(nc2 // 2, FF_CHUNK, d)
    in_specs = [rows.rows(d), prev, nxt, rows.mod(d), _const2((1, d)), _const2(w_up.shape), _const2(b_up.shape),
                _const2(conv_w.shape), _const2(conv_b.shape), _const2(w_down.shape), _const2((1, d))]
    args = [r, r, r, mods, g, w_up, b_up, conv_w, conv_b, w_down, b_down]
    if final:
        in_specs.append(_const2((1, d)))
        args.append(final_g)
        out_spec = pl.BlockSpec((None, ROW_BLOCK, d), lambda b, j: (b, j, 0))
        out_shape = jax.ShapeDtypeStruct((bsz, t - f * ROW_BLOCK, d), F32)
    else:
        out_spec = rows.rows(d)
        out_shape = jax.ShapeDtypeStruct((bsz, t, d), F32)
    kern = functools.partial(_ffn_kernel, first=f, nblk_total=nblk_total, final=final)
    return pl.pallas_call(
        kern,
        grid=rows.grid,
        in_specs=in_specs,
        out_specs=out_spec,
        out_shape=out_shape,
        scratch_shapes=[pltpu.VMEM((ROW_BLOCK + 2 * HALO, d), BF16), pltpu.VMEM((ROW_BLOCK, d), F32)]
        + [pltpu.VMEM((ROW_BLOCK + 2 * HALO, FF_CHUNK), F32)] * 4,
        compiler_params=_params(("arbitrary", "arbitrary")),
        name="ffn",
    )(*args)


def _block_diag_tiles(w):
    nb, bs, _ = w.shape
    per = BD_TILE // bs
    wt = w.reshape(nb // per, per, bs, bs)
    eye = jnp.eye(per, dtype=w.dtype)
    dense = wt[:, :, :, None, :] * eye[None, :, None, :, None]
    return dense.reshape(nb // per, BD_TILE, BD_TILE).astype(BF16)


def _gate_cols(gate_w, gate_b, inner):
    dirs, _, ng = gate_w.shape
    g = gate_w.transpose(1, 0, 2).reshape(3 * inner, dirs * ng)
    g = jnp.pad(g, ((0, 0), (0, LANE - dirs * ng))).astype(BF16)
    gb = jnp.pad(gate_b.reshape(1, dirs * ng), ((0, 0), (0, LANE - dirs * ng)))
    return g[:inner], g[inner:2 * inner], g[2 * inner:], gb


def kernel(x, c, ctx, c_ctx, ada_w, ada_b, norm1_g, norm2_g, final_g, ml_w_in, ml_conv_w, ml_conv_b, ml_wq, ml_wk, ml_wv, ml_gate_w, ml_gate_b, ml_skip, ml_hnorm_g, ml_w_out, na_w_qkv, na_b_qkv, na_rpb, na_w_out, na_b_out, ff_w_up, ff_b_up, ff_conv_w, ff_conv_b, ff_w_down, ff_b_down):
    bsz, seq, d = x.shape
    ctx_len = ctx.shape[1]
    depth = ada_w.shape[0]
    assert ctx_len == ROW_BLOCK and seq % ROW_BLOCK == 0 and seq % GRID_W == 0
    assert bsz + 1 <= MOD_ROWS
    t = ctx_len + seq
    nblk = t // ROW_BLOCK

    c_all = jnp.zeros((MOD_ROWS, d), F32).at[:bsz].set(c).at[bsz].set(c_ctx)
    mods = _ada(c_all, ada_w, ada_b).reshape(depth, MOD_ROWS, 6, d)

    r = jnp.concatenate([ctx, x], axis=1)
    row2 = lambda a: a.reshape(1, -1)
    out = None
    for i in range(depth):
        need_ctx = i < depth - 1
        j = i // 2
        all_rows = _Rows(bsz, nblk, i, 0)
        live_rows = all_rows if need_ctx else _Rows(bsz, nblk, i, 1)
        if i % 2 == 0:
            inner = ml_w_in.shape[2] // 2
            dh = inner // ML_HEADS
            u, z = _ml_in(all_rows, r, mods, row2(norm1_g[i]), ml_w_in[j].astype(BF16))
            gq, gk, gv, gb = _gate_cols(ml_gate_w[j], ml_gate_b[j], inner)
            uc, q, k, v, gates = _ml_proj(u, ml_conv_w[j], row2(ml_conv_b[j]), _block_diag_tiles(ml_wq[j]),
                                          _block_diag_tiles(ml_wk[j]), _block_diag_tiles(ml_wv[j]),
                                          gq, gk, gv, gb, dh, dh ** -0.5)
            hn = _ml_core(q, k, v, gates, dh)
            r = _ml_out(live_rows, hn, uc, z, r, mods, row2(ml_hnorm_g[j]), row2(ml_skip[j]),
                        ml_w_out[j].astype(BF16))
        else:
            q, k, v = _na_qkv(all_rows, r, mods, row2(norm1_g[i]), na_w_qkv[j].astype(BF16), row2(na_b_qkv[j]),
                              NA_HEAD_DIM ** -0.5)
            a = _na_attn(q, k, v, _na_bias_table(na_rpb[j]), need_ctx, ctx_len)
            r = _na_out(live_rows, a, r, mods, na_w_out[j].astype(BF16), row2(na_b_out[j]))
        ffn_args = (live_rows, r, mods, row2(norm2_g[i]), ff_w_up[i].astype(BF16), row2(ff_b_up[i]), ff_conv_w[i],
                    row2(ff_conv_b[i]), ff_w_down[i].astype(BF16), row2(ff_b_down[i]))
        if need_ctx:
            r = _ffn(*ffn_args)
        else:
            out = _ffn(*ffn_args, final_g=row2(final_g))
    return out
```

```python
import functools

import jax
import jax.numpy as jnp
from jax import lax
from jax.experimental import pallas as pl
from jax.experimental.pallas import tpu as pltpu

F32 = jnp.float32
BF16 = jnp.bfloat16
EPS = 1e-6

GRID_W = 64
ML_HEADS = 4
ML_QKV_BLOCK = 4
NA_HEAD_DIM = 64
NA_KH = 8
NA_KW = 16

ROW_BLOCK = 256
HALO = 8
BD_TILE = 256
LANE = 128
NA_GROUP = 256
FF_CHUNK = 256
FF_UNROLL = 4
MOD_ROWS = 16
VMEM_LIMIT = 56 * 1024 * 1024


def _dot(a, b):
    return jnp.dot(a, b, preferred_element_type=F32)


def _dot_nt(a, b):
    return lax.dot_general(a, b, (((1,), (1,)), ((), ())), preferred_element_type=F32)


def _dot_tn(a, b):
    return lax.dot_general(a, b, (((0,), (0,)), ((), ())), preferred_element_type=F32)


def _silu(x):
    return x * jax.nn.sigmoid(x)


def _log_sigmoid(x):
    return jnp.minimum(x, 0.0) - jnp.log1p(jnp.exp(-jnp.abs(x)))


def _norm_mod(x, g, shift, scale):
    y = x * lax.rsqrt(jnp.mean(x * x, axis=-1, keepdims=True) + EPS) * g
    return y * (1.0 + scale) + shift


def _params(sem):
    return pltpu.CompilerParams(dimension_semantics=sem, vmem_limit_bytes=VMEM_LIMIT)


def _resident(shape, index_map):
    return pl.BlockSpec(shape, index_map, pipeline_mode=pl.Buffered(1))


def _ada_kernel(c_ref, w_ref, b_ref, o_ref):
    s = _silu(c_ref[...]).astype(BF16)
    o_ref[...] = _dot(s, w_ref[...].astype(BF16)) + b_ref[...]


def _ada(c_all, ada_w, ada_b):
    depth, d, d6 = ada_w.shape
    n = d6 // d
    return pl.pallas_call(
        _ada_kernel,
        grid=(depth, n),
        in_specs=[
            pl.BlockSpec((MOD_ROWS, d), lambda l, j: (0, 0)),
            pl.BlockSpec((None, d, d), lambda l, j: (l, 0, j)),
            pl.BlockSpec((None, 1, d), lambda l, j: (l, 0, j)),
        ],
        out_specs=pl.BlockSpec((None, MOD_ROWS, d), lambda l, j: (l, 0, j)),
        out_shape=jax.ShapeDtypeStruct((depth, MOD_ROWS, d6), F32),
        compiler_params=_params(("arbitrary", "arbitrary")),
        name="ada",
    )(c_all, ada_w, ada_b.reshape(depth, 1, d6))


class _Rows:
    def __init__(self, bsz, nblk_total, layer, first_blk):
        self.bsz = bsz
        self.nblk_total = nblk_total
        self.layer = layer
        self.first = first_blk
        self.grid = (bsz, nblk_total - first_blk)

    def rows(self, width, rows=ROW_BLOCK):
        f = self.first
        return pl.BlockSpec((None, rows, width), lambda b, j: (b, j + f, 0))

    def mod(self, d):
        f, bsz, layer = self.first, self.bsz, self.layer
        return pl.BlockSpec((None, None, 6, d), lambda b, j: (layer, jnp.where(j + f == 0, bsz, b), 0, 0))


def _const2(shape):
    return _resident(shape, lambda b, j: (0,) * len(shape))


def _proj_kernel(x_ref, mod_ref, g_ref, w_ref, *rest, shift, scale, chunk, splits, has_bias, out_scale):
    if has_bias:
        b_ref, out_refs = rest[0], rest[1:]
    else:
        b_ref, out_refs = None, rest
    h = _norm_mod(x_ref[...], g_ref[...], mod_ref[shift:shift + 1, :], mod_ref[scale:scale + 1, :]).astype(BF16)
    n = w_ref.shape[1]
    per_out = n // len(out_refs)
    for c in range(n // chunk):
        y = _dot(h, w_ref[:, c * chunk:(c + 1) * chunk])
        if has_bias:
            y = y + b_ref[:, c * chunk:(c + 1) * chunk]
        oi, off = divmod(c * chunk, per_out)
        if out_scale[oi] != 1.0:
            y = y * out_scale[oi]
        y = y.astype(BF16)
        o_ref = out_refs[oi]
        if splits:
            for t in range(chunk // splits):
                o_ref[(off + t * splits) // splits] = y[:, t * splits:(t + 1) * splits]
        else:
            o_ref[:, off:off + chunk] = y


def _ml_in(rows, r, mods, g, w_in):
    bsz, t, d = r.shape
    n = w_in.shape[1]
    half = n // 2
    kern = functools.partial(_proj_kernel, shift=0, scale=1, chunk=512, splits=0, has_bias=False,
                             out_scale=(1.0, 1.0))
    return pl.pallas_call(
        kern,
        grid=rows.grid,
        in_specs=[rows.rows(d), rows.mod(d), _const2((1, d)), _const2((d, n))],
        out_specs=[rows.rows(half), rows.rows(half)],
        out_shape=[jax.ShapeDtypeStruct((bsz, t, half), BF16)] * 2,
        compiler_params=_params(("arbitrary", "arbitrary")),
        name="ml_in",
    )(r, mods, g, w_in)


def _na_qkv(rows, r, mods, g, w, b, q_scale):
    bsz, t, d = r.shape
    groups = d // NA_GROUP
    f = rows.first
    out_spec = pl.BlockSpec((None, groups, ROW_BLOCK, NA_GROUP), lambda bb, j: (bb, 0, j + f, 0))
    kern = functools.partial(_proj_kernel, shift=0, scale=1, chunk=NA_GROUP, splits=NA_GROUP, has_bias=True,
                             out_scale=(q_scale, 1.0, 1.0))
    return pl.pallas_call(
        kern,
        grid=rows.grid,
        in_specs=[rows.rows(d), rows.mod(d), _const2((1, d)), _const2((d, 3 * d)), _const2((1, 3 * d))],
        out_specs=[out_spec] * 3,
        out_shape=[jax.ShapeDtypeStruct((bsz, groups, t, NA_GROUP), BF16)] * 3,
        compiler_params=_params(("arbitrary", "arbitrary")),
        name="na_qkv",
    )(r, mods, g, w, b)


def _ml_proj_kernel(u_ref, cw_ref, cb_ref, wq_ref, wk_ref, wv_ref, gq_ref, gk_ref, gv_ref, gb_ref,
                    uc_ref, q_ref, k_ref, v_ref, g_ref, *, nblk, kscale):
    @pl.when(pl.program_id(1) == 0)
    def _():
        g_ref[...] = jnp.broadcast_to(gb_ref[...], g_ref.shape)

    cw = cw_ref[...]
    cb = cb_ref[...]
    taps = cw.shape[0]
    width = u_ref.shape[1]
    pad = 2 * HALO
    for j in range(nblk):
        r0 = j * ROW_BLOCK
        cur_b = u_ref[r0:r0 + ROW_BLOCK, :]
        cur = cur_b.astype(F32)
        if j in (0, 1):
            prev = jnp.zeros((HALO, width), F32)
        else:
            prev = u_ref[r0 - pad:r0, :].astype(F32)[HALO:, :]
        if j in (0, nblk - 1):
            nxt = jnp.zeros((HALO, width), F32)
        else:
            nxt = u_ref[r0 + ROW_BLOCK:r0 + ROW_BLOCK + pad, :].astype(F32)[:HALO, :]
        ext = jnp.concatenate([prev, cur, nxt], axis=0)
        acc = cb
        for t in range(taps):
            shifted = ext if t == taps // 2 else pltpu.roll(ext, (taps // 2 - t) % ext.shape[0], axis=0)
            acc = acc + cw[t:t + 1, :] * shifted[HALO:HALO + ROW_BLOCK, :]
        uc_b = _silu(acc).astype(BF16)
        uc_ref[r0:r0 + ROW_BLOCK, :] = uc_b
        qs, ks, vs = [], [], []
        for t in range(width // BD_TILE):
            sl = slice(t * BD_TILE, (t + 1) * BD_TILE)
            qs.append(_dot(uc_b[:, sl], wq_ref[t]))
            ks.append(_dot(uc_b[:, sl], wk_ref[t]) * kscale)
            vs.append(_dot(cur_b[:, sl], wv_ref[t]))
        q_b = jnp.concatenate(qs, axis=1).astype(BF16)
        k_b = jnp.concatenate(ks, axis=1).astype(BF16)
        v_b = jnp.concatenate(vs, axis=1).astype(BF16)
        q_ref[r0:r0 + ROW_BLOCK, :] = q_b
        k_ref[r0:r0 + ROW_BLOCK, :] = k_b
        v_ref[r0:r0 + ROW_BLOCK, :] = v_b
        g_ref[r0:r0 + ROW_BLOCK, :] += _dot(q_b, gq_ref[...]) + _dot(k_b, gk_ref[...]) + _dot(v_b, gv_ref[...])


def _ml_proj(u, conv_w, conv_b, wq_bd, wk_bd, wv_bd, gq, gk, gv, gb, dh, kscale):
    bsz, t, inner = u.shape
    heads = inner // dh
    tiles = dh // BD_TILE
    taps = conv_w.shape[0]
    col = lambda shape: pl.BlockSpec(shape, lambda b, h: (0, h))
    seq = pl.BlockSpec((None, t, dh), lambda b, h: (b, 0, h))
    bd = pl.BlockSpec((tiles, BD_TILE, BD_TILE), lambda b, h: (h, 0, 0))
    gw = pl.BlockSpec((dh, LANE), lambda b, h: (h, 0))
    kern = functools.partial(_ml_proj_kernel, nblk=t // ROW_BLOCK, kscale=kscale)
    return pl.pallas_call(
        kern,
        grid=(bsz, heads),
        in_specs=[seq, col((taps, dh)), col((1, dh)), bd, bd, bd, gw, gw, gw,
                  pl.BlockSpec((1, LANE), lambda b, h: (0, 0))],
        out_specs=[seq, seq, seq, seq, pl.BlockSpec((None, t, LANE), lambda b, h: (b, 0, 0))],
        out_shape=[jax.ShapeDtypeStruct((bsz, t, inner), BF16)] * 4 + [jax.ShapeDtypeStruct((bsz, t, LANE), F32)],
        compiler_params=_params(("arbitrary", "arbitrary")),
        name="ml_proj",
    )(u, conv_w, conv_b, wq_bd, wk_bd, wv_bd, gq, gk, gv, gb)


def _ml_chunk(d, q, k, v, g, ct_ref, n_ref, m_ref, ci, cf):
    L = q.shape[0]
    gt = g.T
    lane = lax.broadcasted_iota(jnp.int32, g.shape, 1)
    sub = lax.broadcasted_iota(jnp.int32, gt.shape, 0)
    li_col = jnp.sum(jnp.where(lane == ci, g, 0.0), axis=1, keepdims=True)
    lf_col = _log_sigmoid(jnp.sum(jnp.where(lane == cf, g, 0.0), axis=1, keepdims=True))
    li_row = jnp.sum(jnp.where(sub == ci, gt, 0.0), axis=0, keepdims=True)
    lf_row = _log_sigmoid(jnp.sum(jnp.where(sub == cf, gt, 0.0), axis=0, keepdims=True))

    tt = lax.broadcasted_iota(jnp.int32, (L, L), 0)
    ss = lax.broadcasted_iota(jnp.int32, (L, L), 1)
    seen = (ss >= tt) if d else (ss <= tt)
    seen_t = (tt >= ss) if d else (tt <= ss)
    b_col = jnp.sum(jnp.where(seen, lf_row, 0.0), axis=1, keepdims=True)
    b_row = jnp.sum(jnp.where(seen_t, lf_col, 0.0), axis=0, keepdims=True)
    b_last = jnp.sum(lf_row, axis=1, keepdims=True)

    m_prev = m_ref[...]
    r_row = li_row - b_row
    m_t = b_col + jnp.maximum(m_prev, jnp.max(jnp.where(seen, r_row, -jnp.inf), axis=1, keepdims=True))
    sc = _dot_nt(q, k) * jnp.exp(jnp.where(seen, (b_col - m_t) + r_row, -jnp.inf))
    w_inter = jnp.exp(b_col + m_prev - m_t)
    n_rows = jnp.broadcast_to(n_ref[...], (2 * HALO, n_ref.shape[1])).astype(BF16)
    qn = _dot_nt(q, n_rows)[:, :1]
    num = _dot(sc.astype(BF16), v) + w_inter * _dot(q, ct_ref[...].astype(BF16))
    den = jnp.sum(sc, axis=1, keepdims=True) + w_inter * qn
    hh = num * (1.0 / jnp.maximum(jnp.abs(den), jnp.exp(-m_t)))

    g_row = b_last + r_row
    g_col = b_last - b_col + li_col
    m_new = jnp.maximum(b_last + m_prev, jnp.max(g_row, axis=1, keepdims=True))
    decay = jnp.exp(b_last + m_prev - m_new)
    vw = (v.astype(F32) * jnp.exp(g_col - m_new)).astype(BF16)
    wk_rows = jnp.broadcast_to(jnp.exp(g_row - m_new), (2 * HALO, L)).astype(BF16)
    ct_ref[...] = decay * ct_ref[...] + _dot_tn(k, vw)
    n_ref[...] = decay * n_ref[...] + _dot(wk_rows, k)[:1, :]
    m_ref[...] = m_new
    return hh


def _bwd_block(s, nblk):
    return jnp.where(s == 0, 0, nblk - s)


def _ml_core_kernel(qf_ref, kf_ref, vf_ref, gf_ref, qb_ref, kb_ref, vb_ref, gb_ref, o_ref,
                    ctf_ref, nf_ref, mf_ref, ctb_ref, nb_ref, mb_ref, hacc_ref, *, heads, nblk):
    h = pl.program_id(1)
    s = pl.program_id(2)
    L = qf_ref.shape[0]

    @pl.when(s == 0)
    def _():
        for ref in (ctf_ref, nf_ref, mf_ref, ctb_ref, nb_ref, mb_ref, hacc_ref):
            ref[...] = jnp.zeros_like(ref)

    hf = _ml_chunk(0, qf_ref[...], kf_ref[...], vf_ref[...], gf_ref[...], ctf_ref, nf_ref, mf_ref, h, h + heads)
    hacc_ref[pl.ds(pl.multiple_of(s * L, L), L), :] += hf
    hb = _ml_chunk(1, qb_ref[...], kb_ref[...], vb_ref[...], gb_ref[...], ctb_ref, nb_ref, mb_ref,
                   2 * heads + h, 3 * heads + h)
    hacc_ref[pl.ds(pl.multiple_of(_bwd_block(s, nblk) * L, L), L), :] += hb

    @pl.when(s == nblk - 1)
    def _():
        for j in range(nblk):
            tot = hacc_ref[j * L:(j + 1) * L, :]
            cen = tot - jnp.mean(tot, axis=-1, keepdims=True)
            var = jnp.mean(cen * cen, axis=-1, keepdims=True)
            o_ref[j * L:(j + 1) * L, :] = (cen * lax.rsqrt(var + EPS)).astype(BF16)


def _ml_core(q, k, v, gates, dh):
    bsz, t, inner = q.shape
    heads = inner // dh
    nblk = t // ROW_BLOCK
    fwd = pl.BlockSpec((None, ROW_BLOCK, dh), lambda b, h, s: (b, s, h))
    bwd = pl.BlockSpec((None, ROW_BLOCK, dh), lambda b, h, s: (b, _bwd_block(s, nblk), h))
    gate_f = pl.BlockSpec((None, ROW_BLOCK, LANE), lambda b, h, s: (b, s, 0))
    gate_b = pl.BlockSpec((None, ROW_BLOCK, LANE), lambda b, h, s: (b, _bwd_block(s, nblk), 0))
    state = [pltpu.VMEM((dh, dh), F32), pltpu.VMEM((1, dh), F32), pltpu.VMEM((1, 1), F32)]
    kern = functools.partial(_ml_core_kernel, heads=heads, nblk=nblk)
    return pl.pallas_call(
        kern,
        grid=(bsz, heads, nblk),
        in_specs=[fwd, fwd, fwd, gate_f, bwd, bwd, bwd, gate_b],
        out_specs=pl.BlockSpec((None, t, dh), lambda b, h, s: (b, 0, h)),
        out_shape=jax.ShapeDtypeStruct((bsz, t, inner), BF16),
        scratch_shapes=state + state + [pltpu.VMEM((t, dh), F32)],
        compiler_params=_params(("arbitrary",) * 3),
        name="ml_core",
    )(q, k, v, gates, q, k, v, gates)


def _ml_out_kernel(hn_ref, uc_ref, z_ref, r_ref, mod_ref, hg_ref, sk_ref, w_ref, o_ref):
    z = z_ref[...].astype(F32)
    y = (hn_ref[...].astype(F32) * hg_ref[...] + sk_ref[...] * uc_ref[...].astype(F32)) * _silu(z)
    o_ref[...] = r_ref[...] + mod_ref[2:3, :] * _dot(y.astype(BF16), w_ref[...])


def _ml_out(rows, hn, uc, z, r, mods, hnorm_g, skip, w_out):
    bsz, t, d = r.shape
    inner = hn.shape[-1]
    return pl.pallas_call(
        _ml_out_kernel,
        grid=rows.grid,
        in_specs=[rows.rows(inner), rows.rows(inner), rows.rows(inner), rows.rows(d), rows.mod(d),
                  _const2((1, inner)), _const2((1, inner)), _const2((inner, d))],
        out_specs=rows.rows(d),
        out_shape=jax.ShapeDtypeStruct((bsz, t, d), F32),
        compiler_params=_params(("arbitrary", "arbitrary")),
        name="ml_out",
    )(hn, uc, z, r, mods, hnorm_g, skip, w_out)


def _na_out_kernel(a_ref, r_ref, mod_ref, w_ref, b_ref, o_ref):
    o_ref[...] = r_ref[...] + mod_ref[2:3, :] * (_dot(a_ref[...], w_ref[...]) + b_ref[...])


def _na_out(rows, a, r, mods, w_out, b_out):
    bsz, t, d = r.shape
    return pl.pallas_call(
        _na_out_kernel,
        grid=rows.grid,
        in_specs=[rows.rows(d), rows.rows(d), rows.mod(d), _const2((d, d)), _const2((1, d))],
        out_specs=rows.rows(d),
        out_shape=jax.ShapeDtypeStruct((bsz, t, d), F32),
        compiler_params=_params(("arbitrary", "arbitrary")),
        name="na_out",
    )(a, r, mods, w_out, b_out)


def _na_attn_kernel(q_ref, k_ref, v_ref, b_ref, o_ref, *, first, ctx_len, rows, groups):
    gw = q_ref.shape[1]
    per = NA_GROUP // NA_HEAD_DIM
    t = pl.program_id(1) + first
    r = t - ctx_len // gw
    rs = jnp.clip(r - NA_KH // 2, 0, rows - NA_KH)
    kstart = pl.multiple_of(ctx_len + rs * gw, gw)
    nloc = NA_KH * gw
    lane = lax.broadcasted_iota(jnp.int32, (gw, NA_GROUP), 1)
    mine = [(lane >= h * NA_HEAD_DIM) & (lane < (h + 1) * NA_HEAD_DIM) for h in range(per)]

    def scores(g):
        qg = q_ref[g].astype(F32)
        qm = jnp.concatenate([jnp.where(mine[h], qg, 0.0) for h in range(per)], axis=0).astype(BF16)
        return _dot_nt(qm, k_ref[g, pl.ds(kstart, nloc), :]), _dot_nt(qm, k_ref[g, 0:ctx_len, :])

    ahead = scores(0)
    for g in range(groups):
        s_loc, s_ctx = ahead
        if g + 1 < groups:
            ahead = scores(g + 1)
        s_loc = s_loc + b_ref[g]
        m = jnp.maximum(jnp.max(s_loc, axis=1, keepdims=True), jnp.max(s_ctx, axis=1, keepdims=True))
        p_loc = jnp.exp(s_loc - m)
        p_ctx = jnp.exp(s_ctx - m)
        den = jnp.sum(p_loc, axis=1, keepdims=True) + jnp.sum(p_ctx, axis=1, keepdims=True)
        o = (_dot(p_loc.astype(BF16), v_ref[g, pl.ds(kstart, nloc), :])
             + _dot(p_ctx.astype(BF16), v_ref[g, 0:ctx_len, :])) * (1.0 / den)
        og = jnp.where(mine[0], o[0:gw], 0.0)
        for h in range(1, per):
            og = jnp.where(mine[h], o[h * gw:(h + 1) * gw], og)
        o_ref[:, g * NA_GROUP:(g + 1) * NA_GROUP] = og.astype(BF16)


def _na_attn(q, k, v, bias, need_ctx, ctx_len):
    bsz, groups, t, _ = q.shape
    d = groups * NA_GROUP
    rows = (t - ctx_len) // GRID_W
    first = 0 if need_ctx else ctx_len // GRID_W
    steps = t // GRID_W - first
    nvar = bias.shape[0]

    def variant(tq):
        r = tq - ctx_len // GRID_W
        rs = jnp.clip(r - NA_KH // 2, 0, rows - NA_KH)
        return jnp.where(r < 0, nvar - 1, rs - r + NA_KH - 1)

    whole = pl.BlockSpec((None, groups, t, NA_GROUP), lambda b, j: (b, 0, 0, 0))
    kern = functools.partial(_na_attn_kernel, first=first, ctx_len=ctx_len, rows=rows, groups=groups)
    return pl.pallas_call(
        kern,
        grid=(bsz, steps),
        in_specs=[pl.BlockSpec((None, groups, GRID_W, NA_GROUP), lambda b, j: (b, 0, j + first, 0)), whole, whole,
                  pl.BlockSpec((None,) + bias.shape[1:], lambda b, j: (variant(j + first), 0, 0, 0))],
        out_specs=pl.BlockSpec((None, GRID_W, d), lambda b, j: (b, j + first, 0)),
        out_shape=jax.ShapeDtypeStruct((bsz, t, d), BF16),
        compiler_params=_params(("arbitrary", "arbitrary")),
        name="na_attn",
    )(q, k, v, bias)


def _na_bias_table(rpb):
    heads = rpb.shape[0]
    col = jnp.arange(GRID_W)
    col_start = jnp.clip(col - NA_KW // 2, 0, GRID_W - NA_KW)
    col_ok = (col[None, :] >= col_start[:, None]) & (col[None, :] < col_start[:, None] + NA_KW)
    dc_idx = jnp.clip(col[None, :] - col[:, None] + NA_KW - 1, 0, 2 * NA_KW - 2)
    full = jnp.where(col_ok, rpb.astype(F32)[:, :, dc_idx], -jnp.inf)
    var = jnp.stack([full[:, o:o + NA_KH] for o in range(NA_KH)])
    var = jnp.concatenate([var, jnp.full_like(var[:1], -jnp.inf)], axis=0)
    var = var.transpose(0, 1, 3, 2, 4).reshape(NA_KH + 1, heads, GRID_W, NA_KH * GRID_W)
    per = NA_GROUP // NA_HEAD_DIM
    return var.reshape(NA_KH + 1, heads // per, per * GRID_W, NA_KH * GRID_W)


def _ffn_kernel(x_ref, xp_ref, xn_ref, mod_ref, g_ref, wu_ref, bu_ref, cw_ref, cb_ref, wd_ref, bd_ref,
                *rest, first, nblk_total, final):
    if final:
        fg_ref, o_ref, h_buf, acc_ref, *u_bufs = rest
    else:
        o_ref, h_buf, acc_ref, *u_bufs = rest
    j = pl.program_id(1) + first
    x = x_ref[...]
    ext = jnp.concatenate([xp_ref[...], x, xn_ref[...]], axis=0)
    h_buf[...] = _norm_mod(ext, g_ref[...], mod_ref[3:4, :], mod_ref[4:5, :]).astype(BF16)
    row = lax.broadcasted_iota(jnp.int32, (ext.shape[0], 1), 0)
    has_prev = j >= 2
    has_next = (j >= 1) & (j < nblk_total - 1)
    keep = ((row >= HALO) | has_prev) & ((row < HALO + ROW_BLOCK) | has_next)
    nchunk = wd_ref.shape[0]
    taps = cw_ref.shape[1]
    acc_ref[...] = jnp.zeros_like(acc_ref)

    def up(c, slot):
        for half in range(2):
            cc = half * nchunk + c
            u_bufs[2 * slot + half][...] = jnp.where(keep, _dot(h_buf[...], wu_ref[cc]) + bu_ref[cc], 0.0)

    def down(c, slot):
        halves = []
        for half in range(2):
            cc = half * nchunk + c
            y = cb_ref[cc]
            u = u_bufs[2 * slot + half][...]
            for t in range(taps):
                shifted = u if t == taps // 2 else pltpu.roll(u, (taps // 2 - t) % u.shape[0], axis=0)
                y = y + cw_ref[cc, t:t + 1, :] * shifted[HALO:HALO + ROW_BLOCK, :]
            halves.append(y)
        act = (halves[0] * _silu(halves[1])).astype(BF16)
        acc_ref[...] += _dot(act, wd_ref[c])

    def stage(c, slot):
        up(c + 1, 1 - slot)
        down(c, slot)

    unroll = FF_UNROLL
    assert unroll % 2 == 0
    up(0, 0)

    def body(i, carry):
        for k in range(unroll):
            stage(unroll * i + k, k % 2)
        return carry

    looped = (nchunk - 1) // unroll
    lax.fori_loop(0, looped, body, 0)
    for c in range(looped * unroll, nchunk - 1):
        stage(c, c % 2)
    down(nchunk - 1, (nchunk - 1) % 2)
    o = x + mod_ref[5:6, :] * (acc_ref[...] + bd_ref[...])
    if final:
        o = o * lax.rsqrt(jnp.mean(o * o, axis=-1, keepdims=True) + EPS) * fg_ref[...]
    o_ref[...] = o


def _ffn(rows, r, mods, g, w_up, b_up, conv_w, conv_b, w_down, b_down, final_g=None):
    bsz, t, d = r.shape
    f = rows.first
    nblk_total = rows.nblk_total
    per = ROW_BLOCK // HALO
    last = t // HALO - 1
    prev = pl.BlockSpec((None, HALO, d), lambda b, j: (b, jnp.maximum((j + f) * per - 1, f * per), 0))
    nxt = pl.BlockSpec((None, HALO, d), lambda b, j: (b, jnp.minimum((j + f + 1) * per, last), 0))
    hidden2 = w_up.shape[1]
    nc2 = hidden2 // FF_CHUNK
    taps = conv_w.shape[0]
    final = final_g is not None
    w_up = w_up.reshape(d, nc2, FF_CHUNK).transpose(1, 0, 2)
    b_up = b_up.reshape(nc2, 1, FF_CHUNK)
    conv_w = conv_w.reshape(taps, nc2, FF_CHUNK).transpose(1, 0, 2)
    conv_b = conv_b.reshape(nc2, 1, FF_CHUNK)
    w_down = w_down.reshape(nc2 // 2, FF_CHUNK, d)
    in_specs = [rows.rows(d), prev, nxt, rows.mod(d), _const2((1, d)), _const2(w_up.shape), _const2(b_up.shape),
                _const2(conv_w.shape), _const2(conv_b.shape), _const2(w_down.shape), _const2((1, d))]
    args = [r, r, r, mods, g, w_up, b_up, conv_w, conv_b, w_down, b_down]
    if final:
        in_specs.append(_const2((1, d)))
        args.append(final_g)
        out_spec = pl.BlockSpec((None, ROW_BLOCK, d), lambda b, j: (b, j, 0))
        out_shape = jax.ShapeDtypeStruct((bsz, t - f * ROW_BLOCK, d), F32)
    else:
        out_spec = rows.rows(d)
        out_shape = jax.ShapeDtypeStruct((bsz, t, d), F32)
    kern = functools.partial(_ffn_kernel, first=f, nblk_total=nblk_total, final=final)
    return pl.pallas_call(
        kern,
        grid=rows.grid,
        in_specs=in_specs,
        out_specs=out_spec,
        out_shape=out_shape,
        scratch_shapes=[pltpu.VMEM((ROW_BLOCK + 2 * HALO, d), BF16), pltpu.VMEM((ROW_BLOCK, d), F32)]
        + [pltpu.VMEM((ROW_BLOCK + 2 * HALO, FF_CHUNK), F32)] * 4,
        compiler_params=_params(("arbitrary", "arbitrary")),
        name="ffn",
    )(*args)


def _block_diag_tiles(w):
    nb, bs, _ = w.shape
    per = BD_TILE // bs
    wt = w.reshape(nb // per, per, bs, bs)
    eye = jnp.eye(per, dtype=w.dtype)
    dense = wt[:, :, :, None, :] * eye[None, :, None, :, None]
    return dense.reshape(nb // per, BD_TILE, BD_TILE).astype(BF16)


def _gate_cols(gate_w, gate_b, inner):
    dirs, _, ng = gate_w.shape
    g = gate_w.transpose(1, 0, 2).reshape(3 * inner, dirs * ng)
    g = jnp.pad(g, ((0, 0), (0, LANE - dirs * ng))).astype(BF16)
    gb = jnp.pad(gate_b.reshape(1, dirs * ng), ((0, 0), (0, LANE - dirs * ng)))
    return g[:inner], g[inner:2 * inner], g[2 * inner:], gb


def kernel(x, c, ctx, c_ctx, ada_w, ada_b, norm1_g, norm2_g, final_g, ml_w_in, ml_conv_w, ml_conv_b, ml_wq, ml_wk, ml_wv, ml_gate_w, ml_gate_b, ml_skip, ml_hnorm_g, ml_w_out, na_w_qkv, na_b_qkv, na_rpb, na_w_out, na_b_out, ff_w_up, ff_b_up, ff_conv_w, ff_conv_b, ff_w_down, ff_b_down):
    bsz, seq, d = x.shape
    ctx_len = ctx.shape[1]
    depth = ada_w.shape[0]
    assert ctx_len == ROW_BLOCK and seq % ROW_BLOCK == 0 and seq % GRID_W == 0
    assert bsz + 1 <= MOD_ROWS
    t = ctx_len + seq
    nblk = t // ROW_BLOCK

    c_all = jnp.zeros((MOD_ROWS, d), F32).at[:bsz].set(c).at[bsz].set(c_ctx)
    mods = _ada(c_all, ada_w, ada_b).reshape(depth, MOD_ROWS, 6, d)

    r = jnp.concatenate([ctx, x], axis=1)
    row2 = lambda a: a.reshape(1, -1)
    out = None
    for i in range(depth):
        need_ctx = i < depth - 1
        j = i // 2
        all_rows = _Rows(bsz, nblk, i, 0)
        live_rows = all_rows if need_ctx else _Rows(bsz, nblk, i, 1)
        if i % 2 == 0:
            inner = ml_w_in.shape[2] // 2
            dh = inner // ML_HEADS
            u, z = _ml_in(all_rows, r, mods, row2(norm1_g[i]), ml_w_in[j].astype(BF16))
            gq, gk, gv, gb = _gate_cols(ml_gate_w[j], ml_gate_b[j], inner)
            uc, q, k, v, gates = _ml_proj(u, ml_conv_w[j], row2(ml_conv_b[j]), _block_diag_tiles(ml_wq[j]),
                                          _block_diag_tiles(ml_wk[j]), _block_diag_tiles(ml_wv[j]),
                                          gq, gk, gv, gb, dh, dh ** -0.5)
            hn = _ml_core(q, k, v, gates, dh)
            r = _ml_out(live_rows, hn, uc, z, r, mods, row2(ml_hnorm_g[j]), row2(ml_skip[j]),
                        ml_w_out[j].astype(BF16))
        else:
            q, k, v = _na_qkv(all_rows, r, mods, row2(norm1_g[i]), na_w_qkv[j].astype(BF16), row2(na_b_qkv[j]),
                              NA_HEAD_DIM ** -0.5)
            a = _na_attn(q, k, v, _na_bias_table(na_rpb[j]), need_ctx, ctx_len)
            r = _na_out(live_rows, a, r, mods, na_w_out[j].astype(BF16), row2(na_b_out[j]))
        ffn_args = (live_rows, r, mods, row2(norm2_g[i]), ff_w_up[i].astype(BF16), row2(ff_b_up[i]), ff_conv_w[i],
                    row2(ff_conv_b[i]), ff_w_down[i].astype(BF16), row2(ff_b_down[i]))
        if need_ctx:
            r = _ffn(*ffn_args)
        else:
            out = _ffn(*ffn_args, final_g=row2(final_g))
    return out
```

```python
import functools

import jax
import jax.numpy as jnp
import numpy as np
from jax import lax
from jax.experimental import pallas as pl
from jax.experimental.pallas import tpu as pltpu

F32 = jnp.float32
BF16 = jnp.bfloat16
EPS = 1e-6

GRID_W = 64
ML_HEADS = 4
ML_QKV_BLOCK = 4
NA_HEAD_DIM = 64
NA_KH = 8
NA_KW = 16

ROW_BLOCK = 256
HALO = 8
BD_TILE = 256
LANE = 128
NA_GROUP = 256
NA_ROWS_PER_STEP = 2
FF_CHUNK = 256
FF_UNROLL = 4
MOD_ROWS = 16
VMEM_LIMIT = 56 * 1024 * 1024


def _dot(a, b):
    return jnp.dot(a, b, preferred_element_type=F32)


def _dot_nt(a, b):
    return lax.dot_general(a, b, (((1,), (1,)), ((), ())), preferred_element_type=F32)


def _dot_tn(a, b):
    return lax.dot_general(a, b, (((0,), (0,)), ((), ())), preferred_element_type=F32)


def _silu(x):
    return x * jax.nn.sigmoid(x)


def _log_sigmoid(x):
    return jnp.minimum(x, 0.0) - jnp.log1p(jnp.exp(-jnp.abs(x)))


def _norm_mod(x, g, shift, scale):
    y = x * lax.rsqrt(jnp.mean(x * x, axis=-1, keepdims=True) + EPS) * g
    return y * (1.0 + scale) + shift


def _params(sem):
    return pltpu.CompilerParams(dimension_semantics=sem, vmem_limit_bytes=VMEM_LIMIT)


def _resident(shape, index_map):
    return pl.BlockSpec(shape, index_map, pipeline_mode=pl.Buffered(1))


def _ada_kernel(c_ref, w_ref, b_ref, o_ref):
    s = _silu(c_ref[...]).astype(BF16)
    o_ref[...] = _dot(s, w_ref[...].astype(BF16)) + b_ref[...]


def _ada(c_all, ada_w, ada_b):
    depth, d, d6 = ada_w.shape
    n = d6 // d
    return pl.pallas_call(
        _ada_kernel,
        grid=(depth, n),
        in_specs=[
            pl.BlockSpec((MOD_ROWS, d), lambda l, j: (0, 0)),
            pl.BlockSpec((None, d, d), lambda l, j: (l, 0, j)),
            pl.BlockSpec((None, 1, d), lambda l, j: (l, 0, j)),
        ],
        out_specs=pl.BlockSpec((None, MOD_ROWS, d), lambda l, j: (l, 0, j)),
        out_shape=jax.ShapeDtypeStruct((depth, MOD_ROWS, d6), F32),
        compiler_params=_params(("arbitrary", "arbitrary")),
        name="ada",
    )(c_all, ada_w, ada_b.reshape(depth, 1, d6))


class _Rows:
    def __init__(self, bsz, nblk_total, layer, first_blk):
        self.bsz = bsz
        self.nblk_total = nblk_total
        self.layer = layer
        self.first = first_blk
        self.grid = (bsz, nblk_total - first_blk)

    def rows(self, width, rows=ROW_BLOCK):
        f = self.first
        return pl.BlockSpec((None, rows, width), lambda b, j: (b, j + f, 0))

    def residual(self, r):
        if not isinstance(r, tuple):
            return [self.rows(r.shape[-1])], [r]
        assert self.first == 0
        d = r[1].shape[-1]
        return ([pl.BlockSpec((None, ROW_BLOCK, d), lambda b, j: (b, 0, 0)),
                 pl.BlockSpec((None, ROW_BLOCK, d), lambda b, j: (b, jnp.maximum(j - 1, 0), 0))], list(r))

    def mod(self, d):
        f, bsz, layer = self.first, self.bsz, self.layer
        return pl.BlockSpec((None, None, 6, d), lambda b, j: (layer, jnp.where(j + f == 0, bsz, b), 0, 0))


def _const2(shape):
    return _resident(shape, lambda b, j: (0,) * len(shape))


def _residual_block(refs):
    if len(refs) == 1:
        return refs[0][...]
    return jnp.where(pl.program_id(1) == 0, refs[0][...], refs[1][...])


def _residual_shape(r):
    if isinstance(r, tuple):
        return r[1].shape[0], r[0].shape[1] + r[1].shape[1], r[1].shape[2]
    return r.shape


def _proj_kernel(*refs, nres, shift, scale, chunk, splits, has_bias, out_scale):
    res, (mod_ref, g_ref, w_ref), rest = refs[:nres], refs[nres:nres + 3], refs[nres + 3:]
    if has_bias:
        b_ref, out_refs = rest[0], rest[1:]
    else:
        b_ref, out_refs = None, rest
    h = _norm_mod(_residual_block(res), g_ref[...], mod_ref[shift:shift + 1, :],
                  mod_ref[scale:scale + 1, :]).astype(BF16)
    n = w_ref.shape[1]
    per_out = n // len(out_refs)
    for c in range(n // chunk):
        y = _dot(h, w_ref[:, c * chunk:(c + 1) * chunk])
        if has_bias:
            y = y + b_ref[:, c * chunk:(c + 1) * chunk]
        oi, off = divmod(c * chunk, per_out)
        if out_scale[oi] != 1.0:
            y = y * out_scale[oi]
        y = y.astype(BF16)
        o_ref = out_refs[oi]
        if splits:
            for t in range(chunk // splits):
                o_ref[(off + t * splits) // splits] = y[:, t * splits:(t + 1) * splits]
        else:
            o_ref[:, off:off + chunk] = y


def _ml_in(rows, r, mods, g, w_in):
    bsz, t, d = _residual_shape(r)
    n = w_in.shape[1]
    half = n // 2
    res_specs, res_args = rows.residual(r)
    kern = functools.partial(_proj_kernel, nres=len(res_args), shift=0, scale=1, chunk=512, splits=0, has_bias=False,
                             out_scale=(1.0, 1.0))
    return pl.pallas_call(
        kern,
        grid=rows.grid,
        in_specs=res_specs + [rows.mod(d), _const2((1, d)), _const2((d, n))],
        out_specs=[rows.rows(half), rows.rows(half)],
        out_shape=[jax.ShapeDtypeStruct((bsz, t, half), BF16)] * 2,
        compiler_params=_params(("arbitrary", "arbitrary")),
        name="ml_in",
    )(*res_args, mods, g, w_in)


def _na_qkv(rows, r, mods, g, w, b, q_scale):
    bsz, t, d = r.shape
    groups = d // NA_GROUP
    f = rows.first
    out_spec = pl.BlockSpec((None, groups, ROW_BLOCK, NA_GROUP), lambda bb, j: (bb, 0, j + f, 0))
    kern = functools.partial(_proj_kernel, nres=1, shift=0, scale=1, chunk=NA_GROUP, splits=NA_GROUP, has_bias=True,
                             out_scale=(q_scale, 1.0, 1.0))
    return pl.pallas_call(
        kern,
        grid=rows.grid,
        in_specs=[rows.rows(d), rows.mod(d), _const2((1, d)), _const2((d, 3 * d)), _const2((1, 3 * d))],
        out_specs=[out_spec] * 3,
        out_shape=[jax.ShapeDtypeStruct((bsz, groups, t, NA_GROUP), BF16)] * 3,
        compiler_params=_params(("arbitrary", "arbitrary")),
        name="na_qkv",
    )(r, mods, g, w, b)


def _ml_proj_kernel(u_ref, cw_ref, cb_ref, wq_ref, wk_ref, wv_ref, gq_ref, gk_ref, gv_ref, gb_ref,
                    uc_ref, q_ref, k_ref, v_ref, g_ref, *, nblk, kscale):
    @pl.when(pl.program_id(1) == 0)
    def _():
        g_ref[...] = jnp.broadcast_to(gb_ref[...], g_ref.shape)

    cw = cw_ref[...]
    cb = cb_ref[...]
    taps = cw.shape[0]
    width = u_ref.shape[1]
    pad = 2 * HALO
    for j in range(nblk):
        r0 = j * ROW_BLOCK
        cur_b = u_ref[r0:r0 + ROW_BLOCK, :]
        cur = cur_b.astype(F32)
        if j in (0, 1):
            prev = jnp.zeros((HALO, width), F32)
        else:
            prev = u_ref[r0 - pad:r0, :].astype(F32)[HALO:, :]
        if j in (0, nblk - 1):
            nxt = jnp.zeros((HALO, width), F32)
        else:
            nxt = u_ref[r0 + ROW_BLOCK:r0 + ROW_BLOCK + pad, :].astype(F32)[:HALO, :]
        ext = jnp.concatenate([prev, cur, nxt], axis=0)
        acc = cb
        for t in range(taps):
            shifted = ext if t == taps // 2 else pltpu.roll(ext, (taps // 2 - t) % ext.shape[0], axis=0)
            acc = acc + cw[t:t + 1, :] * shifted[HALO:HALO + ROW_BLOCK, :]
        uc_b = _silu(acc).astype(BF16)
        uc_ref[r0:r0 + ROW_BLOCK, :] = uc_b
        qs, ks, vs = [], [], []
        for t in range(width // BD_TILE):
            sl = slice(t * BD_TILE, (t + 1) * BD_TILE)
            qs.append(_dot(uc_b[:, sl], wq_ref[t]))
            ks.append(_dot(uc_b[:, sl], wk_ref[t]) * kscale)
            vs.append(_dot(cur_b[:, sl], wv_ref[t]))
        q_b = jnp.concatenate(qs, axis=1).astype(BF16)
        k_b = jnp.concatenate(ks, axis=1).astype(BF16)
        v_b = jnp.concatenate(vs, axis=1).astype(BF16)
        q_ref[r0:r0 + ROW_BLOCK, :] = q_b
        k_ref[r0:r0 + ROW_BLOCK, :] = k_b
        v_ref[r0:r0 + ROW_BLOCK, :] = v_b
        g_ref[r0:r0 + ROW_BLOCK, :] += _dot(q_b, gq_ref[...]) + _dot(k_b, gk_ref[...]) + _dot(v_b, gv_ref[...])


def _ml_proj(u, conv_w, conv_b, wq_bd, wk_bd, wv_bd, gq, gk, gv, gb, dh, kscale):
    bsz, t, inner = u.shape
    heads = inner // dh
    tiles = dh // BD_TILE
    taps = conv_w.shape[0]
    col = lambda shape: pl.BlockSpec(shape, lambda b, h: (0, h))
    seq = pl.BlockSpec((None, t, dh), lambda b, h: (b, 0, h))
    bd = pl.BlockSpec((tiles, BD_TILE, BD_TILE), lambda b, h: (h, 0, 0))
    gw = pl.BlockSpec((dh, LANE), lambda b, h: (h, 0))
    kern = functools.partial(_ml_proj_kernel, nblk=t // ROW_BLOCK, kscale=kscale)
    return pl.pallas_call(
        kern,
        grid=(bsz, heads),
        in_specs=[seq, col((taps, dh)), col((1, dh)), bd, bd, bd, gw, gw, gw,
                  pl.BlockSpec((1, LANE), lambda b, h: (0, 0))],
        out_specs=[seq, seq, seq, seq, pl.BlockSpec((None, t, LANE), lambda b, h: (b, 0, 0))],
        out_shape=[jax.ShapeDtypeStruct((bsz, t, inner), BF16)] * 4 + [jax.ShapeDtypeStruct((bsz, t, LANE), F32)],
        compiler_params=_params(("arbitrary", "arbitrary")),
        name="ml_proj",
    )(u, conv_w, conv_b, wq_bd, wk_bd, wv_bd, gq, gk, gv, gb)


def _ml_chunk(d, q, k, v, g, ct_ref, n_ref, m_ref, ci, cf):
    L = q.shape[0]
    gt = g.T
    lane = lax.broadcasted_iota(jnp.int32, g.shape, 1)
    sub = lax.broadcasted_iota(jnp.int32, gt.shape, 0)
    li_col = jnp.sum(jnp.where(lane == ci, g, 0.0), axis=1, keepdims=True)
    lf_col = _log_sigmoid(jnp.sum(jnp.where(lane == cf, g, 0.0), axis=1, keepdims=True))
    li_row = jnp.sum(jnp.where(sub == ci, gt, 0.0), axis=0, keepdims=True)
    lf_row = _log_sigmoid(jnp.sum(jnp.where(sub == cf, gt, 0.0), axis=0, keepdims=True))

    tt = lax.broadcasted_iota(jnp.int32, (L, L), 0)
    ss = lax.broadcasted_iota(jnp.int32, (L, L), 1)
    seen = (ss >= tt) if d else (ss <= tt)
    seen_t = (tt >= ss) if d else (tt <= ss)
    b_col = jnp.sum(jnp.where(seen, lf_row, 0.0), axis=1, keepdims=True)
    b_row = jnp.sum(jnp.where(seen_t, lf_col, 0.0), axis=0, keepdims=True)
    b_last = jnp.sum(lf_row, axis=1, keepdims=True)

    m_prev = m_ref[...]
    r_row = li_row - b_row
    m_t = b_col + jnp.maximum(m_prev, jnp.max(jnp.where(seen, r_row, -jnp.inf), axis=1, keepdims=True))
    sc = _dot_nt(q, k) * jnp.exp(jnp.where(seen, (b_col - m_t) + r_row, -jnp.inf))
    w_inter = jnp.exp(b_col + m_prev - m_t)
    n_rows = jnp.broadcast_to(n_ref[...], (2 * HALO, n_ref.shape[1])).astype(BF16)
    qn = _dot_nt(q, n_rows)[:, :1]
    num = _dot(sc.astype(BF16), v) + w_inter * _dot(q, ct_ref[...].astype(BF16))
    den = jnp.sum(sc, axis=1, keepdims=True) + w_inter * qn
    hh = num * (1.0 / jnp.maximum(jnp.abs(den), jnp.exp(-m_t)))

    g_row = b_last + r_row
    g_col = b_last - b_col + li_col
    m_new = jnp.maximum(b_last + m_prev, jnp.max(g_row, axis=1, keepdims=True))
    decay = jnp.exp(b_last + m_prev - m_new)
    vw = (v.astype(F32) * jnp.exp(g_col - m_new)).astype(BF16)
    wk_rows = jnp.broadcast_to(jnp.exp(g_row - m_new), (2 * HALO, L)).astype(BF16)
    ct_ref[...] = decay * ct_ref[...] + _dot_tn(k, vw)
    n_ref[...] = decay * n_ref[...] + _dot(wk_rows, k)[:1, :]
    m_ref[...] = m_new
    return hh


def _bwd_block(s, nblk):
    return jnp.where(s == 0, 0, nblk - s)


def _ml_core_kernel(qf_ref, kf_ref, vf_ref, gf_ref, qb_ref, kb_ref, vb_ref, gb_ref, o_ref,
                    ctf_ref, nf_ref, mf_ref, ctb_ref, nb_ref, mb_ref, hacc_ref, *, heads, nblk):
    h = pl.program_id(1)
    s = pl.program_id(2)
    L = qf_ref.shape[0]

    @pl.when(s == 0)
    def _():
        for ref in (ctf_ref, nf_ref, mf_ref, ctb_ref, nb_ref, mb_ref, hacc_ref):
            ref[...] = jnp.zeros_like(ref)

    hf = _ml_chunk(0, qf_ref[...], kf_ref[...], vf_ref[...], gf_ref[...], ctf_ref, nf_ref, mf_ref, h, h + heads)
    hacc_ref[pl.ds(pl.multiple_of(s * L, L), L), :] += hf
    hb = _ml_chunk(1, qb_ref[...], kb_ref[...], vb_ref[...], gb_ref[...], ctb_ref, nb_ref, mb_ref,
                   2 * heads + h, 3 * heads + h)
    hacc_ref[pl.ds(pl.multiple_of(_bwd_block(s, nblk) * L, L), L), :] += hb

    @pl.when(s == nblk - 1)
    def _():
        for j in range(nblk):
            tot = hacc_ref[j * L:(j + 1) * L, :]
            cen = tot - jnp.mean(tot, axis=-1, keepdims=True)
            var = jnp.mean(cen * cen, axis=-1, keepdims=True)
            o_ref[j * L:(j + 1) * L, :] = (cen * lax.rsqrt(var + EPS)).astype(BF16)


def _ml_core(q, k, v, gates, dh):
    bsz, t, inner = q.shape
    heads = inner // dh
    nblk = t // ROW_BLOCK
    fwd = pl.BlockSpec((None, ROW_BLOCK, dh), lambda b, h, s: (b, s, h))
    bwd = pl.BlockSpec((None, ROW_BLOCK, dh), lambda b, h, s: (b, _bwd_block(s, nblk), h))
    gate_f = pl.BlockSpec((None, ROW_BLOCK, LANE), lambda b, h, s: (b, s, 0))
    gate_b = pl.BlockSpec((None, ROW_BLOCK, LANE), lambda b, h, s: (b, _bwd_block(s, nblk), 0))
    state = [pltpu.VMEM((dh, dh), F32), pltpu.VMEM((1, dh), F32), pltpu.VMEM((1, 1), F32)]
    kern = functools.partial(_ml_core_kernel, heads=heads, nblk=nblk)
    return pl.pallas_call(
        kern,
        grid=(bsz, heads, nblk),
        in_specs=[fwd, fwd, fwd, gate_f, bwd, bwd, bwd, gate_b],
        out_specs=pl.BlockSpec((None, t, dh), lambda b, h, s: (b, 0, h)),
        out_shape=jax.ShapeDtypeStruct((bsz, t, inner), BF16),
        scratch_shapes=state + state + [pltpu.VMEM((t, dh), F32)],
        compiler_params=_params(("arbitrary",) * 3),
        name="ml_core",
    )(q, k, v, gates, q, k, v, gates)


def _ml_out_kernel(hn_ref, uc_ref, z_ref, *refs, nres):
    res, (mod_ref, hg_ref, sk_ref, w_ref, o_ref) = refs[:nres], refs[nres:]
    z = z_ref[...].astype(F32)
    y = (hn_ref[...].astype(F32) * hg_ref[...] + sk_ref[...] * uc_ref[...].astype(F32)) * _silu(z)
    o_ref[...] = _residual_block(res) + mod_ref[2:3, :] * _dot(y.astype(BF16), w_ref[...])


def _ml_out(rows, hn, uc, z, r, mods, hnorm_g, skip, w_out):
    bsz, t, d = _residual_shape(r)
    inner = hn.shape[-1]
    res_specs, res_args = rows.residual(r)
    return pl.pallas_call(
        functools.partial(_ml_out_kernel, nres=len(res_args)),
        grid=rows.grid,
        in_specs=[rows.rows(inner), rows.rows(inner), rows.rows(inner)] + res_specs
        + [rows.mod(d), _const2((1, inner)), _const2((1, inner)), _const2((inner, d))],
        out_specs=rows.rows(d),
        out_shape=jax.ShapeDtypeStruct((bsz, t, d), F32),
        compiler_params=_params(("arbitrary", "arbitrary")),
        name="ml_out",
    )(hn, uc, z, *res_args, mods, hnorm_g, skip, w_out)


def _na_out_kernel(a_ref, r_ref, mod_ref, w_ref, b_ref, o_ref):
    o_ref[...] = r_ref[...] + mod_ref[2:3, :] * (_dot(a_ref[...], w_ref[...]) + b_ref[...])


def _na_out(rows, a, r, mods, w_out, b_out):
    bsz, t, d = r.shape
    return pl.pallas_call(
        _na_out_kernel,
        grid=rows.grid,
        in_specs=[rows.rows(d), rows.rows(d), rows.mod(d), _const2((d, d)), _const2((1, d))],
        out_specs=rows.rows(d),
        out_shape=jax.ShapeDtypeStruct((bsz, t, d), F32),
        compiler_params=_params(("arbitrary", "arbitrary")),
        name="na_out",
    )(a, r, mods, w_out, b_out)


def _na_attn_kernel(q_ref, k_ref, v_ref, *rest, first, ctx_len, rows, groups):
    b_refs, o_ref = rest[:NA_ROWS_PER_STEP], rest[NA_ROWS_PER_STEP]
    gw = GRID_W
    per = NA_GROUP // NA_HEAD_DIM
    nloc = NA_KH * gw
    lane = lax.broadcasted_iota(jnp.int32, (gw, NA_GROUP), 1)
    mine = [(lane >= h * NA_HEAD_DIM) & (lane < (h + 1) * NA_HEAD_DIM) for h in range(per)]

    def key_start(sub):
        t = (pl.program_id(1) + first) * NA_ROWS_PER_STEP + sub
        r = t - ctx_len // gw
        rs = jnp.clip(r - NA_KH // 2, 0, rows - NA_KH)
        return pl.multiple_of(ctx_len + rs * gw, gw)

    kstart = [key_start(sub) for sub in range(NA_ROWS_PER_STEP)]

    def scores(sub, g):
        qg = q_ref[g, sub * gw:(sub + 1) * gw, :].astype(F32)
        qm = jnp.concatenate([jnp.where(mine[h], qg, 0.0) for h in range(per)], axis=0).astype(BF16)
        return _dot_nt(qm, k_ref[g, pl.ds(kstart[sub], nloc), :]), _dot_nt(qm, k_ref[g, 0:ctx_len, :])

    work = [(sub, g) for sub in range(NA_ROWS_PER_STEP) for g in range(groups)]
    ahead = scores(*work[0])
    for i, (sub, g) in enumerate(work):
        s_loc, s_ctx = ahead
        if i + 1 < len(work):
            ahead = scores(*work[i + 1])
        s_loc = s_loc + b_refs[sub][per * g:per * (g + 1)].reshape(per * gw, nloc)
        m = jnp.maximum(jnp.max(s_loc, axis=1, keepdims=True), jnp.max(s_ctx, axis=1, keepdims=True))
        p_loc = jnp.exp(s_loc - m)
        p_ctx = jnp.exp(s_ctx - m)
        den = jnp.sum(p_loc, axis=1, keepdims=True) + jnp.sum(p_ctx, axis=1, keepdims=True)
        o = (_dot(p_loc.astype(BF16), v_ref[g, pl.ds(kstart[sub], nloc), :])
             + _dot(p_ctx.astype(BF16), v_ref[g, 0:ctx_len, :])) * (1.0 / den)
        og = jnp.where(mine[0], o[0:gw], 0.0)
        for h in range(1, per):
            og = jnp.where(mine[h], o[h * gw:(h + 1) * gw], og)
        o_ref[sub * gw:(sub + 1) * gw, g * NA_GROUP:(g + 1) * NA_GROUP] = og.astype(BF16)


def _na_attn(q, k, v, bias, need_ctx, ctx_len):
    bsz, groups, t, _ = q.shape
    d = groups * NA_GROUP
    rows = (t - ctx_len) // GRID_W
    blk = NA_ROWS_PER_STEP * GRID_W
    assert ctx_len % blk == 0 and t % blk == 0
    first = 0 if need_ctx else ctx_len // blk
    steps = t // blk - first
    heads, nvar = bias.shape[:2]

    def variant(tq):
        r = tq - ctx_len // GRID_W
        rs = jnp.clip(r - NA_KH // 2, 0, rows - NA_KH)
        return jnp.where(r < 0, nvar - 1, rs - r + NA_KH - 1)

    def bias_spec(sub):
        return pl.BlockSpec((heads, None) + bias.shape[2:],
                            lambda b, j: (0, variant((j + first) * NA_ROWS_PER_STEP + sub), 0, 0))

    whole = pl.BlockSpec((None, groups, t, NA_GROUP), lambda b, j: (b, 0, 0, 0))
    kern = functools.partial(_na_attn_kernel, first=first, ctx_len=ctx_len, rows=rows, groups=groups)
    return pl.pallas_call(
        kern,
        grid=(bsz, steps),
        in_specs=[pl.BlockSpec((None, groups, blk, NA_GROUP), lambda b, j: (b, 0, j + first, 0)), whole, whole]
        + [bias_spec(sub) for sub in range(NA_ROWS_PER_STEP)],
        out_specs=pl.BlockSpec((None, blk, d), lambda b, j: (b, j + first, 0)),
        out_shape=jax.ShapeDtypeStruct((bsz, t, d), BF16),
        compiler_params=_params(("arbitrary", "arbitrary")),
        name="na_attn",
    )(q, k, v, *([bias] * NA_ROWS_PER_STEP))


def _na_bias_table(rpb):
    heads, nrow, ncol = rpb.shape
    col = np.arange(GRID_W)
    col_start = np.clip(col - NA_KW // 2, 0, GRID_W - NA_KW)
    col_ok = (col[None, :] >= col_start[:, None]) & (col[None, :] < col_start[:, None] + NA_KW)
    dcol = np.clip(col[None, :] - col[:, None] + NA_KW - 1, 0, ncol - 1)
    drow = np.arange(NA_KH)[:, None] + np.arange(NA_KH)[None, :]
    idx = drow[:, None, :, None] * ncol + dcol[None, :, None, :]
    ok = np.broadcast_to(col_ok[None, :, None, :], idx.shape)
    idx = np.concatenate([idx, np.zeros_like(idx[:1])], axis=0).reshape(NA_KH + 1, GRID_W, NA_KH * GRID_W)
    ok = np.concatenate([ok, np.zeros_like(ok[:1])], axis=0).reshape(idx.shape)
    picked = jnp.take(rpb.astype(F32).reshape(heads, nrow * ncol), jnp.asarray(idx, jnp.int32), axis=1)
    return jnp.where(jnp.asarray(ok), picked, -jnp.inf)


def _ffn_kernel(x_ref, xp_ref, xn_ref, mod_ref, g_ref, wu_ref, bu_ref, cw_ref, cb_ref, wd_ref, bd_ref,
                *rest, first, nblk_total, final):
    if final:
        fg_ref, o_ref, h_buf, acc_ref, *u_bufs = rest
    else:
        o_ref, h_buf, acc_ref, *u_bufs = rest
    j = pl.program_id(1) + first
    x = x_ref[...]
    ext = jnp.concatenate([xp_ref[...], x, xn_ref[...]], axis=0)
    h_buf[...] = _norm_mod(ext, g_ref[...], mod_ref[3:4, :], mod_ref[4:5, :]).astype(BF16)
    row = lax.broadcasted_iota(jnp.int32, (ext.shape[0], 1), 0)
    has_prev = j >= 2
    has_next = (j >= 1) & (j < nblk_total - 1)
    keep = ((row >= HALO) | has_prev) & ((row < HALO + ROW_BLOCK) | has_next)
    nchunk = wd_ref.shape[0]
    taps = cw_ref.shape[1]
    acc_ref[...] = jnp.zeros_like(acc_ref)

    def up(c, slot):
        for half in range(2):
            cc = half * nchunk + c
            u_bufs[2 * slot + half][...] = jnp.where(keep, _dot(h_buf[...], wu_ref[cc]) + bu_ref[cc], 0.0)

    def down(c, slot):
        halves = []
        for half in range(2):
            cc = half * nchunk + c
            y = cb_ref[cc]
            u = u_bufs[2 * slot + half][...]
            for t in range(taps):
                shifted = u if t == taps // 2 else pltpu.roll(u, (taps // 2 - t) % u.shape[0], axis=0)
                y = y + cw_ref[cc, t:t + 1, :] * shifted[HALO:HALO + ROW_BLOCK, :]
            halves.append(y)
        act = (halves[0] * _silu(halves[1])).astype(BF16)
        acc_ref[...] += _dot(act, wd_ref[c])

    def stage(c, slot):
        up(c + 1, 1 - slot)
        down(c, slot)

    unroll = FF_UNROLL
    assert unroll % 2 == 0
    up(0, 0)

    def body(i, carry):
        for k in range(unroll):
            stage(unroll * i + k, k % 2)
        return carry

    looped = (nchunk - 1) // unroll
    lax.fori_loop(0, looped, body, 0)
    for c in range(looped * unroll, nchunk - 1):
        stage(c, c % 2)
    down(nchunk - 1, (nchunk - 1) % 2)
    o = x + mod_ref[5:6, :] * (acc_ref[...] + bd_ref[...])
    if final:
        o = o * lax.rsqrt(jnp.mean(o * o, axis=-1, keepdims=True) + EPS) * fg_ref[...]
    o_ref[...] = o


def _ffn(rows, r, mods, g, w_up, b_up, conv_w, conv_b, w_down, b_down, final_g=None):
    bsz, t, d = r.shape
    f = rows.first
    nblk_total = rows.nblk_total
    per = ROW_BLOCK // HALO
    last = t // HALO - 1
    prev = pl.BlockSpec((None, HALO, d), lambda b, j: (b, jnp.maximum((j + f) * per - 1, f * per), 0))
    nxt = pl.BlockSpec((None, HALO, d), lambda b, j: (b, jnp.minimum((j + f + 1) * per, last), 0))
    hidden2 = w_up.shape[1]
    nc2 = hidden2 // FF_CHUNK
    taps = conv_w.shape[0]
    final = final_g is not None
    w_up = w_up.reshape(d, nc2, FF_CHUNK).transpose(1, 0, 2)
    b_up = b_up.reshape(nc2, 1, FF_CHUNK)
    conv_w = conv_w.reshape(taps, nc2, FF_CHUNK).transpose(1, 0, 2)
    conv_b = conv_b.reshape(nc2, 1, FF_CHUNK)
    w_down = w_down.reshape(nc2 // 2, FF_CHUNK, d)
    in_specs = [rows.rows(d), prev, nxt, rows.mod(d), _const2((1, d)), _const2(w_up.shape), _const2(b_up.shape),
                _const2(conv_w.shape), _const2(conv_b.shape), _const2(w_down.shape), _const2((1, d))]
    args = [r, r, r, mods, g, w_up, b_up, conv_w, conv_b, w_down, b_down]
    if final:
        in_specs.append(_const2((1, d)))
        args.append(final_g)
        out_spec = pl.BlockSpec((None, ROW_BLOCK, d), lambda b, j: (b, j, 0))
        out_shape = jax.ShapeDtypeStruct((bsz, t - f * ROW_BLOCK, d), F32)
    else:
        out_spec = rows.rows(d)
        out_shape = jax.ShapeDtypeStruct((bsz, t, d), F32)
    kern = functools.partial(_ffn_kernel, first=f, nblk_total=nblk_total, final=final)
    return pl.pallas_call(
        kern,
        grid=rows.grid,
        in_specs=in_specs,
        out_specs=out_spec,
        out_shape=out_shape,
        scratch_shapes=[pltpu.VMEM((ROW_BLOCK + 2 * HALO, d), BF16), pltpu.VMEM((ROW_BLOCK, d), F32)]
        + [pltpu.VMEM((ROW_BLOCK + 2 * HALO, FF_CHUNK), F32)] * 4,
        compiler_params=_params(("arbitrary", "arbitrary")),
        name="ffn",
    )(*args)


def _block_diag_tiles(w):
    nb, bs, _ = w.shape
    per = BD_TILE // bs
    rep = jnp.tile(w.reshape(nb * bs, bs), (1, per)).reshape(nb // per, BD_TILE, BD_TILE)
    pos = np.arange(BD_TILE) // bs
    return jnp.where(jnp.asarray(pos[:, None] == pos[None, :]), rep, 0.0).astype(BF16)


def _gate_cols(gate_w, gate_b, inner):
    dirs, _, ng = gate_w.shape
    g = gate_w.transpose(1, 0, 2).reshape(3 * inner, dirs * ng)
    g = jnp.pad(g, ((0, 0), (0, LANE - dirs * ng))).astype(BF16)
    gb = jnp.pad(gate_b.reshape(1, dirs * ng), ((0, 0), (0, LANE - dirs * ng)))
    return g[:inner], g[inner:2 * inner], g[2 * inner:], gb


def kernel(x, c, ctx, c_ctx, ada_w, ada_b, norm1_g, norm2_g, final_g, ml_w_in, ml_conv_w, ml_conv_b, ml_wq, ml_wk, ml_wv, ml_gate_w, ml_gate_b, ml_skip, ml_hnorm_g, ml_w_out, na_w_qkv, na_b_qkv, na_rpb, na_w_out, na_b_out, ff_w_up, ff_b_up, ff_conv_w, ff_conv_b, ff_w_down, ff_b_down):
    bsz, seq, d = x.shape
    ctx_len = ctx.shape[1]
    depth = ada_w.shape[0]
    assert ctx_len == ROW_BLOCK and seq % ROW_BLOCK == 0 and seq % GRID_W == 0
    assert bsz + 1 <= MOD_ROWS
    t = ctx_len + seq
    nblk = t // ROW_BLOCK

    c_all = jnp.zeros((MOD_ROWS, d), F32).at[:bsz].set(c).at[bsz].set(c_ctx)
    mods = _ada(c_all, ada_w, ada_b).reshape(depth, MOD_ROWS, 6, d)

    r = (ctx, x)
    row2 = lambda a: a.reshape(1, -1)
    out = None
    for i in range(depth):
        need_ctx = i < depth - 1
        j = i // 2
        all_rows = _Rows(bsz, nblk, i, 0)
        live_rows = all_rows if need_ctx else _Rows(bsz, nblk, i, 1)
        if i % 2 == 0:
            inner = ml_w_in.shape[2] // 2
            dh = inner // ML_HEADS
            u, z = _ml_in(all_rows, r, mods, row2(norm1_g[i]), ml_w_in[j].astype(BF16))
            gq, gk, gv, gb = _gate_cols(ml_gate_w[j], ml_gate_b[j], inner)
            uc, q, k, v, gates = _ml_proj(u, ml_conv_w[j], row2(ml_conv_b[j]), _block_diag_tiles(ml_wq[j]),
                                          _block_diag_tiles(ml_wk[j]), _block_diag_tiles(ml_wv[j]),
                                          gq, gk, gv, gb, dh, dh ** -0.5)
            hn = _ml_core(q, k, v, gates, dh)
            r = _ml_out(live_rows, hn, uc, z, r, mods, row2(ml_hnorm_g[j]), row2(ml_skip[j]),
                        ml_w_out[j].astype(BF16))
        else:
            q, k, v = _na_qkv(all_rows, r, mods, row2(norm1_g[i]), na_w_qkv[j].astype(BF16), row2(na_b_qkv[j]),
                              NA_HEAD_DIM ** -0.5)
            a = _na_attn(q, k, v, _na_bias_table(na_rpb[j]), need_ctx, ctx_len)
            r = _na_out(live_rows, a, r, mods, na_w_out[j].astype(BF16), row2(na_b_out[j]))
        ffn_args = (live_rows, r, mods, row2(norm2_g[i]), ff_w_up[i].astype(BF16), row2(ff_b_up[i]), ff_conv_w[i],
                    row2(ff_conv_b[i]), ff_w_down[i].astype(BF16), row2(ff_b_down[i]))
        if need_ctx:
            r = _ffn(*ffn_args)
        else:
            out = _ffn(*ffn_args, final_g=row2(final_g))
    return out
```

```python
import functools

import jax
import jax.numpy as jnp
import numpy as np
from jax import lax
from jax.experimental import pallas as pl
from jax.experimental.pallas import tpu as pltpu

F32 = jnp.float32
BF16 = jnp.bfloat16
EPS = 1e-6

GRID_W = 64
ML_HEADS = 4
ML_QKV_BLOCK = 4
NA_HEAD_DIM = 64
NA_KH = 8
NA_KW = 16

ROW_BLOCK = 256
HALO = 8
BD_TILE = 256
LANE = 128
NA_GROUP = 256
NA_ROWS_PER_STEP = 2
FF_CHUNK = 256
FF_UNROLL = 4
MOD_ROWS = 16
VMEM_LIMIT = 56 * 1024 * 1024


def _dot(a, b):
    return jnp.dot(a, b, preferred_element_type=F32)


def _dot_nt(a, b):
    return lax.dot_general(a, b, (((1,), (1,)), ((), ())), preferred_element_type=F32)


def _dot_tn(a, b):
    return lax.dot_general(a, b, (((0,), (0,)), ((), ())), preferred_element_type=F32)


def _silu(x):
    return x * jax.nn.sigmoid(x)


def _log_sigmoid(x):
    return jnp.minimum(x, 0.0) - jnp.log1p(jnp.exp(-jnp.abs(x)))


def _norm_mod(x, g, shift, scale):
    y = x * lax.rsqrt(jnp.mean(x * x, axis=-1, keepdims=True) + EPS) * g
    return y * (1.0 + scale) + shift


def _params(sem):
    return pltpu.CompilerParams(dimension_semantics=sem, vmem_limit_bytes=VMEM_LIMIT)


def _resident(shape, index_map):
    return pl.BlockSpec(shape, index_map, pipeline_mode=pl.Buffered(1))


def _ada_kernel(c_ref, w_ref, b_ref, o_ref):
    s = _silu(c_ref[...]).astype(BF16)
    o_ref[...] = _dot(s, w_ref[...].astype(BF16)) + b_ref[...]


def _ada(c_all, ada_w, ada_b):
    depth, d, d6 = ada_w.shape
    n = d6 // d
    return pl.pallas_call(
        _ada_kernel,
        grid=(depth, n),
        in_specs=[
            pl.BlockSpec((MOD_ROWS, d), lambda l, j: (0, 0)),
            pl.BlockSpec((None, d, d), lambda l, j: (l, 0, j)),
            pl.BlockSpec((None, 1, d), lambda l, j: (l, 0, j)),
        ],
        out_specs=pl.BlockSpec((None, MOD_ROWS, d), lambda l, j: (l, 0, j)),
        out_shape=jax.ShapeDtypeStruct((depth, MOD_ROWS, d6), F32),
        compiler_params=_params(("arbitrary", "arbitrary")),
        name="ada",
    )(c_all, ada_w, ada_b.reshape(depth, 1, d6))


class _Rows:
    def __init__(self, bsz, nblk_total, layer, first_blk):
        self.bsz = bsz
        self.nblk_total = nblk_total
        self.layer = layer
        self.first = first_blk
        self.grid = (bsz, nblk_total - first_blk)

    def rows(self, width, rows=ROW_BLOCK):
        f = self.first
        return pl.BlockSpec((None, rows, width), lambda b, j: (b, j + f, 0))

    def residual(self, r):
        if not isinstance(r, tuple):
            return [self.rows(r.shape[-1])], [r]
        assert self.first == 0
        d = r[1].shape[-1]
        return ([pl.BlockSpec((None, ROW_BLOCK, d), lambda b, j: (b, 0, 0)),
                 pl.BlockSpec((None, ROW_BLOCK, d), lambda b, j: (b, jnp.maximum(j - 1, 0), 0))], list(r))

    def mod(self, d):
        f, bsz, layer = self.first, self.bsz, self.layer
        return pl.BlockSpec((None, None, 6, d), lambda b, j: (layer, jnp.where(j + f == 0, bsz, b), 0, 0))


def _const2(shape):
    return _resident(shape, lambda b, j: (0,) * len(shape))


def _residual_block(refs):
    if len(refs) == 1:
        return refs[0][...]
    return jnp.where(pl.program_id(1) == 0, refs[0][...], refs[1][...])


def _residual_shape(r):
    if isinstance(r, tuple):
        return r[1].shape[0], r[0].shape[1] + r[1].shape[1], r[1].shape[2]
    return r.shape


def _proj_kernel(*refs, nres, shift, scale, chunk, splits, has_bias, out_scale):
    res, (mod_ref, g_ref, w_ref), rest = refs[:nres], refs[nres:nres + 3], refs[nres + 3:]
    if has_bias:
        b_ref, out_refs = rest[0], rest[1:]
    else:
        b_ref, out_refs = None, rest
    h = _norm_mod(_residual_block(res), g_ref[...], mod_ref[shift:shift + 1, :],
                  mod_ref[scale:scale + 1, :]).astype(BF16)
    n = w_ref.shape[1]
    per_out = n // len(out_refs)
    for c in range(n // chunk):
        y = _dot(h, w_ref[:, c * chunk:(c + 1) * chunk])
        if has_bias:
            y = y + b_ref[:, c * chunk:(c + 1) * chunk]
        oi, off = divmod(c * chunk, per_out)
        if out_scale[oi] != 1.0:
            y = y * out_scale[oi]
        y = y.astype(BF16)
        o_ref = out_refs[oi]
        if splits:
            for t in range(chunk // splits):
                o_ref[(off + t * splits) // splits] = y[:, t * splits:(t + 1) * splits]
        else:
            o_ref[:, off:off + chunk] = y


def _ml_in(rows, r, mods, g, w_in):
    bsz, t, d = _residual_shape(r)
    n = w_in.shape[1]
    half = n // 2
    res_specs, res_args = rows.residual(r)
    kern = functools.partial(_proj_kernel, nres=len(res_args), shift=0, scale=1, chunk=512, splits=0, has_bias=False,
                             out_scale=(1.0, 1.0))
    return pl.pallas_call(
        kern,
        grid=rows.grid,
        in_specs=res_specs + [rows.mod(d), _const2((1, d)), _const2((d, n))],
        out_specs=[rows.rows(half), rows.rows(half)],
        out_shape=[jax.ShapeDtypeStruct((bsz, t, half), BF16)] * 2,
        compiler_params=_params(("arbitrary", "arbitrary")),
        name="ml_in",
    )(*res_args, mods, g, w_in)


def _na_qkv(rows, r, mods, g, w, b, q_scale):
    bsz, t, d = r.shape
    groups = d // NA_GROUP
    f = rows.first
    out_spec = pl.BlockSpec((None, groups, ROW_BLOCK, NA_GROUP), lambda bb, j: (bb, 0, j + f, 0))
    kern = functools.partial(_proj_kernel, nres=1, shift=0, scale=1, chunk=NA_GROUP, splits=NA_GROUP, has_bias=True,
                             out_scale=(q_scale, 1.0, 1.0))
    return pl.pallas_call(
        kern,
        grid=rows.grid,
        in_specs=[rows.rows(d), rows.mod(d), _const2((1, d)), _const2((d, 3 * d)), _const2((1, 3 * d))],
        out_specs=[out_spec] * 3,
        out_shape=[jax.ShapeDtypeStruct((bsz, groups, t, NA_GROUP), BF16)] * 3,
        compiler_params=_params(("arbitrary", "arbitrary")),
        name="na_qkv",
    )(r, mods, g, w, b)


def _ml_proj_kernel(u_ref, cw_ref, cb_ref, wq_ref, wk_ref, wv_ref, gq_ref, gk_ref, gv_ref, gb_ref,
                    uc_ref, q_ref, k_ref, v_ref, g_ref, *, nblk, kscale, ngates):
    @pl.when(pl.program_id(1) == 0)
    def _():
        g_ref[...] = jnp.broadcast_to(gb_ref[...], g_ref.shape)

    cw = cw_ref[...]
    cb = cb_ref[...]
    taps = cw.shape[0]
    width = u_ref.shape[1]
    pad = 2 * HALO
    for j in range(nblk):
        r0 = j * ROW_BLOCK
        cur_b = u_ref[r0:r0 + ROW_BLOCK, :]
        cur = cur_b.astype(F32)
        if j in (0, 1):
            prev = jnp.zeros((HALO, width), F32)
        else:
            prev = u_ref[r0 - pad:r0, :].astype(F32)[HALO:, :]
        if j in (0, nblk - 1):
            nxt = jnp.zeros((HALO, width), F32)
        else:
            nxt = u_ref[r0 + ROW_BLOCK:r0 + ROW_BLOCK + pad, :].astype(F32)[:HALO, :]
        ext = jnp.concatenate([prev, cur, nxt], axis=0)
        acc = cb
        for t in range(taps):
            shifted = ext if t == taps // 2 else pltpu.roll(ext, (taps // 2 - t) % ext.shape[0], axis=0)
            acc = acc + cw[t:t + 1, :] * shifted[HALO:HALO + ROW_BLOCK, :]
        uc_b = _silu(acc).astype(BF16)
        uc_ref[r0:r0 + ROW_BLOCK, :] = uc_b
        qs, ks, vs = [], [], []
        for t in range(width // BD_TILE):
            sl = slice(t * BD_TILE, (t + 1) * BD_TILE)
            qs.append(_dot(uc_b[:, sl], wq_ref[t]))
            ks.append(_dot(uc_b[:, sl], wk_ref[t]) * kscale)
            vs.append(_dot(cur_b[:, sl], wv_ref[t]))
        q_b = jnp.concatenate(qs, axis=1).astype(BF16)
        k_b = jnp.concatenate(ks, axis=1).astype(BF16)
        v_b = jnp.concatenate(vs, axis=1).astype(BF16)
        q_ref[r0:r0 + ROW_BLOCK, :] = q_b
        k_ref[r0:r0 + ROW_BLOCK, :] = k_b
        v_ref[r0:r0 + ROW_BLOCK, :] = v_b
        g_ref[r0:r0 + ROW_BLOCK, :] += _dot(q_b, gq_ref[...]) + _dot(k_b, gk_ref[...]) + _dot(v_b, gv_ref[...])

    @pl.when(pl.program_id(1) == pl.num_programs(1) - 1)
    def _():
        lane = lax.broadcasted_iota(jnp.int32, (ROW_BLOCK, g_ref.shape[1]), 1)
        forget = (lane < ngates) & (lane % (ngates // 2) >= ngates // 4)
        for j in range(nblk):
            pre = g_ref[j * ROW_BLOCK:(j + 1) * ROW_BLOCK, :]
            g_ref[j * ROW_BLOCK:(j + 1) * ROW_BLOCK, :] = jnp.where(forget, _log_sigmoid(pre), pre)


def _ml_proj(u, conv_w, conv_b, wq_bd, wk_bd, wv_bd, gq, gk, gv, gb, dh, kscale, ngates):
    bsz, t, inner = u.shape
    heads = inner // dh
    tiles = dh // BD_TILE
    taps = conv_w.shape[0]
    col = lambda shape: pl.BlockSpec(shape, lambda b, h: (0, h))
    seq = pl.BlockSpec((None, t, dh), lambda b, h: (b, 0, h))
    bd = pl.BlockSpec((tiles, BD_TILE, BD_TILE), lambda b, h: (h, 0, 0))
    gw = pl.BlockSpec((dh, LANE), lambda b, h: (h, 0))
    kern = functools.partial(_ml_proj_kernel, nblk=t // ROW_BLOCK, kscale=kscale, ngates=ngates)
    return pl.pallas_call(
        kern,
        grid=(bsz, heads),
        in_specs=[seq, col((taps, dh)), col((1, dh)), bd, bd, bd, gw, gw, gw,
                  pl.BlockSpec((1, LANE), lambda b, h: (0, 0))],
        out_specs=[seq, seq, seq, seq, pl.BlockSpec((None, t, LANE), lambda b, h: (b, 0, 0))],
        out_shape=[jax.ShapeDtypeStruct((bsz, t, inner), BF16)] * 4 + [jax.ShapeDtypeStruct((bsz, t, LANE), F32)],
        compiler_params=_params(("arbitrary", "arbitrary")),
        name="ml_proj",
    )(u, conv_w, conv_b, wq_bd, wk_bd, wv_bd, gq, gk, gv, gb)


def _ml_chunk(d, q, k, v, g, ct_ref, n_ref, m_ref, ci, cf):
    L = q.shape[0]
    gt = g.T
    lane = lax.broadcasted_iota(jnp.int32, g.shape, 1)
    sub = lax.broadcasted_iota(jnp.int32, gt.shape, 0)
    li_col = jnp.sum(jnp.where(lane == ci, g, 0.0), axis=1, keepdims=True)
    lf_col = jnp.sum(jnp.where(lane == cf, g, 0.0), axis=1, keepdims=True)
    li_row = jnp.sum(jnp.where(sub == ci, gt, 0.0), axis=0, keepdims=True)
    lf_row = jnp.sum(jnp.where(sub == cf, gt, 0.0), axis=0, keepdims=True)

    tt = lax.broadcasted_iota(jnp.int32, (L, L), 0)
    ss = lax.broadcasted_iota(jnp.int32, (L, L), 1)
    seen = (ss >= tt) if d else (ss <= tt)
    seen_t = (tt >= ss) if d else (tt <= ss)
    b_col = jnp.sum(jnp.where(seen, lf_row, 0.0), axis=1, keepdims=True)
    b_row = jnp.sum(jnp.where(seen_t, lf_col, 0.0), axis=0, keepdims=True)
    b_last = jnp.sum(lf_row, axis=1, keepdims=True)

    m_prev = m_ref[...]
    r_row = li_row - b_row
    m_t = b_col + jnp.maximum(m_prev, jnp.max(jnp.where(seen, r_row, -jnp.inf), axis=1, keepdims=True))
    sc = _dot_nt(q, k) * jnp.exp(jnp.where(seen, (b_col - m_t) + r_row, -jnp.inf))
    w_inter = jnp.exp(b_col + m_prev - m_t)
    n_rows = jnp.broadcast_to(n_ref[...], (2 * HALO, n_ref.shape[1])).astype(BF16)
    qn = _dot_nt(q, n_rows)[:, :1]
    num = _dot(sc.astype(BF16), v) + w_inter * _dot(q, ct_ref[...].astype(BF16))
    den = jnp.sum(sc, axis=1, keepdims=True) + w_inter * qn
    hh = num * (1.0 / jnp.maximum(jnp.abs(den), jnp.exp(-m_t)))

    g_row = b_last + r_row
    g_col = b_last - b_col + li_col
    m_new = jnp.maximum(b_last + m_prev, jnp.max(g_row, axis=1, keepdims=True))
    decay = jnp.exp(b_last + m_prev - m_new)
    vw = (v.astype(F32) * jnp.exp(g_col - m_new)).astype(BF16)
    wk_rows = jnp.broadcast_to(jnp.exp(g_row - m_new), (2 * HALO, L)).astype(BF16)
    ct_ref[...] = decay * ct_ref[...] + _dot_tn(k, vw)
    n_ref[...] = decay * n_ref[...] + _dot(wk_rows, k)[:1, :]
    m_ref[...] = m_new
    return hh


def _bwd_block(s, nblk):
    return jnp.where(s == 0, 0, nblk - s)


def _ml_core_kernel(qf_ref, kf_ref, vf_ref, gf_ref, qb_ref, kb_ref, vb_ref, gb_ref, o_ref,
                    ctf_ref, nf_ref, mf_ref, ctb_ref, nb_ref, mb_ref, hacc_ref, *, heads, nblk):
    h = pl.program_id(1)
    s = pl.program_id(2)
    L = qf_ref.shape[0]

    @pl.when(s == 0)
    def _():
        for ref in (ctf_ref, nf_ref, mf_ref, ctb_ref, nb_ref, mb_ref, hacc_ref):
            ref[...] = jnp.zeros_like(ref)

    hf = _ml_chunk(0, qf_ref[...], kf_ref[...], vf_ref[...], gf_ref[...], ctf_ref, nf_ref, mf_ref, h, h + heads)
    hacc_ref[pl.ds(pl.multiple_of(s * L, L), L), :] += hf
    hb = _ml_chunk(1, qb_ref[...], kb_ref[...], vb_ref[...], gb_ref[...], ctb_ref, nb_ref, mb_ref,
                   2 * heads + h, 3 * heads + h)
    hacc_ref[pl.ds(pl.multiple_of(_bwd_block(s, nblk) * L, L), L), :] += hb

    @pl.when(s == nblk - 1)
    def _():
        for j in range(nblk):
            tot = hacc_ref[j * L:(j + 1) * L, :]
            cen = tot - jnp.mean(tot, axis=-1, keepdims=True)
            var = jnp.mean(cen * cen, axis=-1, keepdims=True)
            o_ref[j * L:(j + 1) * L, :] = (cen * lax.rsqrt(var + EPS)).astype(BF16)


def _ml_core(q, k, v, gates, dh):
    bsz, t, inner = q.shape
    heads = inner // dh
    nblk = t // ROW_BLOCK
    fwd = pl.BlockSpec((None, ROW_BLOCK, dh), lambda b, h, s: (b, s, h))
    bwd = pl.BlockSpec((None, ROW_BLOCK, dh), lambda b, h, s: (b, _bwd_block(s, nblk), h))
    gate_f = pl.BlockSpec((None, ROW_BLOCK, LANE), lambda b, h, s: (b, s, 0))
    gate_b = pl.BlockSpec((None, ROW_BLOCK, LANE), lambda b, h, s: (b, _bwd_block(s, nblk), 0))
    state = [pltpu.VMEM((dh, dh), F32), pltpu.VMEM((1, dh), F32), pltpu.VMEM((1, 1), F32)]
    kern = functools.partial(_ml_core_kernel, heads=heads, nblk=nblk)
    return pl.pallas_call(
        kern,
        grid=(bsz, heads, nblk),
        in_specs=[fwd, fwd, fwd, gate_f, bwd, bwd, bwd, gate_b],
        out_specs=pl.BlockSpec((None, t, dh), lambda b, h, s: (b, 0, h)),
        out_shape=jax.ShapeDtypeStruct((bsz, t, inner), BF16),
        scratch_shapes=state + state + [pltpu.VMEM((t, dh), F32)],
        compiler_params=_params(("arbitrary",) * 3),
        name="ml_core",
    )(q, k, v, gates, q, k, v, gates)


def _ml_out_kernel(hn_ref, uc_ref, z_ref, *refs, nres):
    res, (mod_ref, hg_ref, sk_ref, w_ref, o_ref) = refs[:nres], refs[nres:]
    z = z_ref[...].astype(F32)
    y = (hn_ref[...].astype(F32) * hg_ref[...] + sk_ref[...] * uc_ref[...].astype(F32)) * _silu(z)
    o_ref[...] = _residual_block(res) + mod_ref[2:3, :] * _dot(y.astype(BF16), w_ref[...])


def _ml_out(rows, hn, uc, z, r, mods, hnorm_g, skip, w_out):
    bsz, t, d = _residual_shape(r)
    inner = hn.shape[-1]
    res_specs, res_args = rows.residual(r)
    return pl.pallas_call(
        functools.partial(_ml_out_kernel, nres=len(res_args)),
        grid=rows.grid,
        in_specs=[rows.rows(inner), rows.rows(inner), rows.rows(inner)] + res_specs
        + [rows.mod(d), _const2((1, inner)), _const2((1, inner)), _const2((inner, d))],
        out_specs=rows.rows(d),
        out_shape=jax.ShapeDtypeStruct((bsz, t, d), F32),
        compiler_params=_params(("arbitrary", "arbitrary")),
        name="ml_out",
    )(hn, uc, z, *res_args, mods, hnorm_g, skip, w_out)


def _na_out_kernel(a_ref, r_ref, mod_ref, w_ref, b_ref, o_ref):
    o_ref[...] = r_ref[...] + mod_ref[2:3, :] * (_dot(a_ref[...], w_ref[...]) + b_ref[...])


def _na_out(rows, a, r, mods, w_out, b_out):
    bsz, t, d = r.shape
    return pl.pallas_call(
        _na_out_kernel,
        grid=rows.grid,
        in_specs=[rows.rows(d), rows.rows(d), rows.mod(d), _const2((d, d)), _const2((1, d))],
        out_specs=rows.rows(d),
        out_shape=jax.ShapeDtypeStruct((bsz, t, d), F32),
        compiler_params=_params(("arbitrary", "arbitrary")),
        name="na_out",
    )(a, r, mods, w_out, b_out)


def _na_attn_kernel(q_ref, k_ref, v_ref, *rest, first, ctx_len, rows, groups):
    b_refs, o_ref = rest[:NA_ROWS_PER_STEP], rest[NA_ROWS_PER_STEP]
    gw = GRID_W
    per = NA_GROUP // NA_HEAD_DIM
    nloc = NA_KH * gw
    lane = lax.broadcasted_iota(jnp.int32, (gw, NA_GROUP), 1)
    mine = [(lane >= h * NA_HEAD_DIM) & (lane < (h + 1) * NA_HEAD_DIM) for h in range(per)]

    def key_start(sub):
        t = (pl.program_id(1) + first) * NA_ROWS_PER_STEP + sub
        r = t - ctx_len // gw
        rs = jnp.clip(r - NA_KH // 2, 0, rows - NA_KH)
        return pl.multiple_of(ctx_len + rs * gw, gw)

    kstart = [key_start(sub) for sub in range(NA_ROWS_PER_STEP)]

    def scores(sub, g):
        qg = q_ref[g, sub * gw:(sub + 1) * gw, :].astype(F32)
        qm = jnp.concatenate([jnp.where(mine[h], qg, 0.0) for h in range(per)], axis=0).astype(BF16)
        return _dot_nt(qm, k_ref[g, pl.ds(kstart[sub], nloc), :]), _dot_nt(qm, k_ref[g, 0:ctx_len, :])

    work = [(sub, g) for sub in range(NA_ROWS_PER_STEP) for g in range(groups)]
    ahead = scores(*work[0])
    for i, (sub, g) in enumerate(work):
        s_loc, s_ctx = ahead
        if i + 1 < len(work):
            ahead = scores(*work[i + 1])
        s_loc = s_loc + b_refs[sub][per * g:per * (g + 1)].reshape(per * gw, nloc)
        m = jnp.maximum(jnp.max(s_loc, axis=1, keepdims=True), jnp.max(s_ctx, axis=1, keepdims=True))
        p_loc = jnp.exp(s_loc - m)
        p_ctx = jnp.exp(s_ctx - m)
        den = jnp.sum(p_loc, axis=1, keepdims=True) + jnp.sum(p_ctx, axis=1, keepdims=True)
        o = (_dot(p_loc.astype(BF16), v_ref[g, pl.ds(kstart[sub], nloc), :])
             + _dot(p_ctx.astype(BF16), v_ref[g, 0:ctx_len, :])) * (1.0 / den)
        og = jnp.where(mine[0], o[0:gw], 0.0)
        for h in range(1, per):
            og = jnp.where(mine[h], o[h * gw:(h + 1) * gw], og)
        o_ref[sub * gw:(sub + 1) * gw, g * NA_GROUP:(g + 1) * NA_GROUP] = og.astype(BF16)


def _na_attn(q, k, v, bias, need_ctx, ctx_len):
    bsz, groups, t, _ = q.shape
    d = groups * NA_GROUP
    rows = (t - ctx_len) // GRID_W
    blk = NA_ROWS_PER_STEP * GRID_W
    assert ctx_len % blk == 0 and t % blk == 0
    first = 0 if need_ctx else ctx_len // blk
    steps = t // blk - first
    heads, nvar = bias.shape[:2]

    def variant(tq):
        r = tq - ctx_len // GRID_W
        rs = jnp.clip(r - NA_KH // 2, 0, rows - NA_KH)
        return jnp.where(r < 0, nvar - 1, rs - r + NA_KH - 1)

    def bias_spec(sub):
        return pl.BlockSpec((heads, None) + bias.shape[2:],
                            lambda b, j: (0, variant((j + first) * NA_ROWS_PER_STEP + sub), 0, 0))

    whole = pl.BlockSpec((None, groups, t, NA_GROUP), lambda b, j: (b, 0, 0, 0))
    kern = functools.partial(_na_attn_kernel, first=first, ctx_len=ctx_len, rows=rows, groups=groups)
    return pl.pallas_call(
        kern,
        grid=(bsz, steps),
        in_specs=[pl.BlockSpec((None, groups, blk, NA_GROUP), lambda b, j: (b, 0, j + first, 0)), whole, whole]
        + [bias_spec(sub) for sub in range(NA_ROWS_PER_STEP)],
        out_specs=pl.BlockSpec((None, blk, d), lambda b, j: (b, j + first, 0)),
        out_shape=jax.ShapeDtypeStruct((bsz, t, d), BF16),
        compiler_params=_params(("arbitrary", "arbitrary")),
        name="na_attn",
    )(q, k, v, *([bias] * NA_ROWS_PER_STEP))


def _na_bias_table(rpb):
    heads, nrow, ncol = rpb.shape
    col = np.arange(GRID_W)
    col_start = np.clip(col - NA_KW // 2, 0, GRID_W - NA_KW)
    col_ok = (col[None, :] >= col_start[:, None]) & (col[None, :] < col_start[:, None] + NA_KW)
    dcol = np.clip(col[None, :] - col[:, None] + NA_KW - 1, 0, ncol - 1)
    full = jnp.where(jnp.asarray(col_ok[None, :, None, :]),
                     rpb.astype(F32)[:, :, jnp.asarray(dcol)].transpose(0, 2, 1, 3), -jnp.inf)
    nloc = NA_KH * GRID_W
    var = [full[:, :, o:o + NA_KH, :].reshape(heads, GRID_W, nloc) for o in range(NA_KH)]
    var.append(jnp.full((heads, GRID_W, nloc), -jnp.inf, F32))
    return jnp.stack(var, axis=1)


def _ffn_kernel(x_ref, xp_ref, xn_ref, mod_ref, g_ref, wu_ref, bu_ref, cw_ref, cb_ref, wd_ref, bd_ref,
                *rest, first, nblk_total, final):
    if final:
        fg_ref, o_ref, h_buf, acc_ref, *u_bufs = rest
    else:
        o_ref, h_buf, acc_ref, *u_bufs = rest
    j = pl.program_id(1) + first
    x = x_ref[...]
    ext = jnp.concatenate([xp_ref[...], x, xn_ref[...]], axis=0)
    h_buf[...] = _norm_mod(ext, g_ref[...], mod_ref[3:4, :], mod_ref[4:5, :]).astype(BF16)
    row = lax.broadcasted_iota(jnp.int32, (ext.shape[0], 1), 0)
    has_prev = j >= 2
    has_next = (j >= 1) & (j < nblk_total - 1)
    keep = ((row >= HALO) | has_prev) & ((row < HALO + ROW_BLOCK) | has_next)
    nchunk = wd_ref.shape[0]
    taps = cw_ref.shape[1]
    acc_ref[...] = jnp.zeros_like(acc_ref)

    def up(c, slot):
        for half in range(2):
            cc = half * nchunk + c
            u_bufs[2 * slot + half][...] = jnp.where(keep, _dot(h_buf[...], wu_ref[cc]) + bu_ref[cc], 0.0)

    def down(c, slot):
        halves = []
        for half in range(2):
            cc = half * nchunk + c
            y = cb_ref[cc]
            u = u_bufs[2 * slot + half][...]
            for t in range(taps):
                shifted = u if t == taps // 2 else pltpu.roll(u, (taps // 2 - t) % u.shape[0], axis=0)
                y = y + cw_ref[cc, t:t + 1, :] * shifted[HALO:HALO + ROW_BLOCK, :]
            halves.append(y)
        act = (halves[0] * _silu(halves[1])).astype(BF16)
        acc_ref[...] += _dot(act, wd_ref[c])

    def stage(c, slot):
        up(c + 1, 1 - slot)
        down(c, slot)

    unroll = FF_UNROLL
    assert unroll % 2 == 0
    up(0, 0)

    def body(i, carry):
        for k in range(unroll):
            stage(unroll * i + k, k % 2)
        return carry

    looped = (nchunk - 1) // unroll
    lax.fori_loop(0, looped, body, 0)
    for c in range(looped * unroll, nchunk - 1):
        stage(c, c % 2)
    down(nchunk - 1, (nchunk - 1) % 2)
    o = x + mod_ref[5:6, :] * (acc_ref[...] + bd_ref[...])
    if final:
        o = o * lax.rsqrt(jnp.mean(o * o, axis=-1, keepdims=True) + EPS) * fg_ref[...]
    o_ref[...] = o


def _ffn(rows, r, mods, g, w_up, b_up, conv_w, conv_b, w_down, b_down, final_g=None):
    bsz, t, d = r.shape
    f = rows.first
    nblk_total = rows.nblk_total
    per = ROW_BLOCK // HALO
    last = t // HALO - 1
    prev = pl.BlockSpec((None, HALO, d), lambda b, j: (b, jnp.maximum((j + f) * per - 1, f * per), 0))
    nxt = pl.BlockSpec((None, HALO, d), lambda b, j: (b, jnp.minimum((j + f + 1) * per, last), 0))
    hidden2 = w_up.shape[1]
    nc2 = hidden2 // FF_CHUNK
    taps = conv_w.shape[0]
    final = final_g is not None
    w_up = w_up.reshape(d, nc2, FF_CHUNK).transpose(1, 0, 2)
    b_up = b_up.reshape(nc2, 1, FF_CHUNK)
    conv_w = conv_w.reshape(taps, nc2, FF_CHUNK).transpose(1, 0, 2)
    conv_b = conv_b.reshape(nc2, 1, FF_CHUNK)
    w_down = w_down.reshape(nc2 // 2, FF_CHUNK, d)
    in_specs = [rows.rows(d), prev, nxt, rows.mod(d), _const2((1, d)), _const2(w_up.shape), _const2(b_up.shape),
                _const2(conv_w.shape), _const2(conv_b.shape), _const2(w_down.shape), _const2((1, d))]
    args = [r, r, r, mods, g, w_up, b_up, conv_w, conv_b, w_down, b_down]
    if final:
        in_specs.append(_const2((1, d)))
        args.append(final_g)
        out_spec = pl.BlockSpec((None, ROW_BLOCK, d), lambda b, j: (b, j, 0))
        out_shape = jax.ShapeDtypeStruct((bsz, t - f * ROW_BLOCK, d), F32)
    else:
        out_spec = rows.rows(d)
        out_shape = jax.ShapeDtypeStruct((bsz, t, d), F32)
    kern = functools.partial(_ffn_kernel, first=f, nblk_total=nblk_total, final=final)
    return pl.pallas_call(
        kern,
        grid=rows.grid,
        in_specs=in_specs,
        out_specs=out_spec,
        out_shape=out_shape,
        scratch_shapes=[pltpu.VMEM((ROW_BLOCK + 2 * HALO, d), BF16), pltpu.VMEM((ROW_BLOCK, d), F32)]
        + [pltpu.VMEM((ROW_BLOCK + 2 * HALO, FF_CHUNK), F32)] * 4,
        compiler_params=_params(("arbitrary", "arbitrary")),
        name="ffn",
    )(*args)


def _block_diag_tiles(w):
    nb, bs, _ = w.shape
    per = BD_TILE // bs
    rep = jnp.tile(w.reshape(nb * bs, bs), (1, per)).reshape(nb // per, BD_TILE, BD_TILE)
    pos = np.arange(BD_TILE) // bs
    return jnp.where(jnp.asarray(pos[:, None] == pos[None, :]), rep, 0.0).astype(BF16)


def _gate_cols(gate_w, gate_b, inner):
    dirs, _, ng = gate_w.shape
    g = gate_w.transpose(1, 0, 2).reshape(3 * inner, dirs * ng)
    g = jnp.pad(g, ((0, 0), (0, LANE - dirs * ng))).astype(BF16)
    gb = jnp.pad(gate_b.reshape(1, dirs * ng), ((0, 0), (0, LANE - dirs * ng)))
    return g[:inner], g[inner:2 * inner], g[2 * inner:], gb


def kernel(x, c, ctx, c_ctx, ada_w, ada_b, norm1_g, norm2_g, final_g, ml_w_in, ml_conv_w, ml_conv_b, ml_wq, ml_wk, ml_wv, ml_gate_w, ml_gate_b, ml_skip, ml_hnorm_g, ml_w_out, na_w_qkv, na_b_qkv, na_rpb, na_w_out, na_b_out, ff_w_up, ff_b_up, ff_conv_w, ff_conv_b, ff_w_down, ff_b_down):
    bsz, seq, d = x.shape
    ctx_len = ctx.shape[1]
    depth = ada_w.shape[0]
    assert ctx_len == ROW_BLOCK and seq % ROW_BLOCK == 0 and seq % GRID_W == 0
    assert bsz + 1 <= MOD_ROWS
    t = ctx_len + seq
    nblk = t // ROW_BLOCK

    c_all = jnp.zeros((MOD_ROWS, d), F32).at[:bsz].set(c).at[bsz].set(c_ctx)
    mods = _ada(c_all, ada_w, ada_b).reshape(depth, MOD_ROWS, 6, d)

    r = (ctx, x)
    row2 = lambda a: a.reshape(1, -1)
    out = None
    for i in range(depth):
        need_ctx = i < depth - 1
        j = i // 2
        all_rows = _Rows(bsz, nblk, i, 0)
        live_rows = all_rows if need_ctx else _Rows(bsz, nblk, i, 1)
        if i % 2 == 0:
            inner = ml_w_in.shape[2] // 2
            dh = inner // ML_HEADS
            u, z = _ml_in(all_rows, r, mods, row2(norm1_g[i]), ml_w_in[j].astype(BF16))
            gq, gk, gv, gb = _gate_cols(ml_gate_w[j], ml_gate_b[j], inner)
            uc, q, k, v, gates = _ml_proj(u, ml_conv_w[j], row2(ml_conv_b[j]), _block_diag_tiles(ml_wq[j]),
                                          _block_diag_tiles(ml_wk[j]), _block_diag_tiles(ml_wv[j]),
                                          gq, gk, gv, gb, dh, dh ** -0.5,
                                          ml_gate_w.shape[1] * ml_gate_w.shape[3])
            hn = _ml_core(q, k, v, gates, dh)
            r = _ml_out(live_rows, hn, uc, z, r, mods, row2(ml_hnorm_g[j]), row2(ml_skip[j]),
                        ml_w_out[j].astype(BF16))
        else:
            q, k, v = _na_qkv(all_rows, r, mods, row2(norm1_g[i]), na_w_qkv[j].astype(BF16), row2(na_b_qkv[j]),
                              NA_HEAD_DIM ** -0.5)
            a = _na_attn(q, k, v, _na_bias_table(na_rpb[j]), need_ctx, ctx_len)
            r = _na_out(live_rows, a, r, mods, na_w_out[j].astype(BF16), row2(na_b_out[j]))
        ffn_args = (live_rows, r, mods, row2(norm2_g[i]), ff_w_up[i].astype(BF16), row2(ff_b_up[i]), ff_conv_w[i],
                    row2(ff_conv_b[i]), ff_w_down[i].astype(BF16), row2(ff_b_down[i]))
        if need_ctx:
            r = _ffn(*ffn_args)
        else:
            out = _ffn(*ffn_args, final_g=row2(final_g))
    return out
```

```python
import functools

import jax
import jax.numpy as jnp
import numpy as np
from jax import lax
from jax.experimental import pallas as pl
from jax.experimental.pallas import tpu as pltpu

F32 = jnp.float32
BF16 = jnp.bfloat16
EPS = 1e-6

GRID_W = 64
ML_HEADS = 4
ML_QKV_BLOCK = 4
NA_HEAD_DIM = 64
NA_KH = 8
NA_KW = 16

ROW_BLOCK = 256
HALO = 8
BD_TILE = 256
LANE = 128
NA_GROUP = 256
NA_ROWS_PER_STEP = 4
ML_HEADS_PER_STEP = 2
FF_CHUNK = 256
FF_UNROLL = 4
MOD_ROWS = 16
VMEM_LIMIT = 56 * 1024 * 1024


def _dot(a, b):
    return jnp.dot(a, b, preferred_element_type=F32)


def _dot_nt(a, b):
    return lax.dot_general(a, b, (((1,), (1,)), ((), ())), preferred_element_type=F32)


def _dot_tn(a, b):
    return lax.dot_general(a, b, (((0,), (0,)), ((), ())), preferred_element_type=F32)


def _silu(x):
    return x * jax.nn.sigmoid(x)


def _log_sigmoid(x):
    return jnp.minimum(x, 0.0) - jnp.log1p(jnp.exp(-jnp.abs(x)))


def _norm_mod(x, g, shift, scale):
    y = x * lax.rsqrt(jnp.mean(x * x, axis=-1, keepdims=True) + EPS) * g
    return y * (1.0 + scale) + shift


def _params(sem):
    return pltpu.CompilerParams(dimension_semantics=sem, vmem_limit_bytes=VMEM_LIMIT)


def _resident(shape, index_map):
    return pl.BlockSpec(shape, index_map, pipeline_mode=pl.Buffered(1))


def _ada_kernel(c_ref, w_ref, b_ref, o_ref):
    s = _silu(c_ref[...]).astype(BF16)
    o_ref[...] = _dot(s, w_ref[...].astype(BF16)) + b_ref[...]


def _ada(c_all, ada_w, ada_b):
    depth, d, d6 = ada_w.shape
    n = d6 // d
    return pl.pallas_call(
        _ada_kernel,
        grid=(depth, n),
        in_specs=[
            pl.BlockSpec((MOD_ROWS, d), lambda l, j: (0, 0)),
            pl.BlockSpec((None, d, d), lambda l, j: (l, 0, j)),
            pl.BlockSpec((None, 1, d), lambda l, j: (l, 0, j)),
        ],
        out_specs=pl.BlockSpec((None, MOD_ROWS, d), lambda l, j: (l, 0, j)),
        out_shape=jax.ShapeDtypeStruct((depth, MOD_ROWS, d6), F32),
        compiler_params=_params(("arbitrary", "arbitrary")),
        name="ada",
    )(c_all, ada_w, ada_b.reshape(depth, 1, d6))


class _Rows:
    def __init__(self, bsz, nblk_total, layer, first_blk):
        self.bsz = bsz
        self.nblk_total = nblk_total
        self.layer = layer
        self.first = first_blk
        self.grid = (bsz, nblk_total - first_blk)

    def rows(self, width, rows=ROW_BLOCK):
        f = self.first
        return pl.BlockSpec((None, rows, width), lambda b, j: (b, j + f, 0))

    def residual(self, r):
        if not isinstance(r, tuple):
            return [self.rows(r.shape[-1])], [r]
        assert self.first == 0
        d = r[1].shape[-1]
        return ([pl.BlockSpec((None, ROW_BLOCK, d), lambda b, j: (b, 0, 0)),
                 pl.BlockSpec((None, ROW_BLOCK, d), lambda b, j: (b, jnp.maximum(j - 1, 0), 0))], list(r))

    def mod(self, d):
        f, bsz, layer = self.first, self.bsz, self.layer
        return pl.BlockSpec((None, None, 6, d), lambda b, j: (layer, jnp.where(j + f == 0, bsz, b), 0, 0))


def _const2(shape):
    return _resident(shape, lambda b, j: (0,) * len(shape))


def _residual_block(refs):
    if len(refs) == 1:
        return refs[0][...]
    return jnp.where(pl.program_id(1) == 0, refs[0][...], refs[1][...])


def _residual_shape(r):
    if isinstance(r, tuple):
        return r[1].shape[0], r[0].shape[1] + r[1].shape[1], r[1].shape[2]
    return r.shape


def _proj_kernel(*refs, nres, shift, scale, chunk, splits, has_bias, out_scale):
    res, (mod_ref, g_ref, w_ref), rest = refs[:nres], refs[nres:nres + 3], refs[nres + 3:]
    if has_bias:
        b_ref, out_refs = rest[0], rest[1:]
    else:
        b_ref, out_refs = None, rest
    h = _norm_mod(_residual_block(res), g_ref[...], mod_ref[shift:shift + 1, :],
                  mod_ref[scale:scale + 1, :]).astype(BF16)
    n = w_ref.shape[1]
    per_out = n // len(out_refs)
    for c in range(n // chunk):
        y = _dot(h, w_ref[:, c * chunk:(c + 1) * chunk])
        if has_bias:
            y = y + b_ref[:, c * chunk:(c + 1) * chunk]
        oi, off = divmod(c * chunk, per_out)
        if out_scale[oi] != 1.0:
            y = y * out_scale[oi]
        y = y.astype(BF16)
        o_ref = out_refs[oi]
        if splits:
            for t in range(chunk // splits):
                o_ref[(off + t * splits) // splits] = y[:, t * splits:(t + 1) * splits]
        else:
            o_ref[:, off:off + chunk] = y


def _ml_in(rows, r, mods, g, w_in):
    bsz, t, d = _residual_shape(r)
    n = w_in.shape[1]
    half = n // 2
    res_specs, res_args = rows.residual(r)
    kern = functools.partial(_proj_kernel, nres=len(res_args), shift=0, scale=1, chunk=512, splits=0, has_bias=False,
                             out_scale=(1.0, 1.0))
    return pl.pallas_call(
        kern,
        grid=rows.grid,
        in_specs=res_specs + [rows.mod(d), _const2((1, d)), _const2((d, n))],
        out_specs=[rows.rows(half), rows.rows(half)],
        out_shape=[jax.ShapeDtypeStruct((bsz, t, half), BF16)] * 2,
        compiler_params=_params(("arbitrary", "arbitrary")),
        name="ml_in",
    )(*res_args, mods, g, w_in)


def _na_qkv(rows, r, mods, g, w, b, q_scale):
    bsz, t, d = r.shape
    groups = d // NA_GROUP
    f = rows.first
    out_spec = pl.BlockSpec((None, groups, ROW_BLOCK, NA_GROUP), lambda bb, j: (bb, 0, j + f, 0))
    kern = functools.partial(_proj_kernel, nres=1, shift=0, scale=1, chunk=NA_GROUP, splits=NA_GROUP, has_bias=True,
                             out_scale=(q_scale, 1.0, 1.0))
    return pl.pallas_call(
        kern,
        grid=rows.grid,
        in_specs=[rows.rows(d), rows.mod(d), _const2((1, d)), _const2((d, 3 * d)), _const2((1, 3 * d))],
        out_specs=[out_spec] * 3,
        out_shape=[jax.ShapeDtypeStruct((bsz, groups, t, NA_GROUP), BF16)] * 3,
        compiler_params=_params(("arbitrary", "arbitrary")),
        name="na_qkv",
    )(r, mods, g, w, b)


def _ml_proj_kernel(u_ref, cw_ref, cb_ref, wq_ref, wk_ref, wv_ref, gq_ref, gk_ref, gv_ref, gb_ref,
                    uc_ref, q_ref, k_ref, v_ref, g_ref, *, nblk, kscale, ngates):
    @pl.when(pl.program_id(1) == 0)
    def _():
        g_ref[...] = jnp.broadcast_to(gb_ref[...], g_ref.shape)

    cw = cw_ref[...]
    cb = cb_ref[...]
    taps = cw.shape[0]
    width = u_ref.shape[1]
    pad = 2 * HALO
    for j in range(nblk):
        r0 = j * ROW_BLOCK
        cur_b = u_ref[r0:r0 + ROW_BLOCK, :]
        cur = cur_b.astype(F32)
        if j in (0, 1):
            prev = jnp.zeros((HALO, width), F32)
        else:
            prev = u_ref[r0 - pad:r0, :].astype(F32)[HALO:, :]
        if j in (0, nblk - 1):
            nxt = jnp.zeros((HALO, width), F32)
        else:
            nxt = u_ref[r0 + ROW_BLOCK:r0 + ROW_BLOCK + pad, :].astype(F32)[:HALO, :]
        ext = jnp.concatenate([prev, cur, nxt], axis=0)
        acc = cb
        for t in range(taps):
            shifted = ext if t == taps // 2 else pltpu.roll(ext, (taps // 2 - t) % ext.shape[0], axis=0)
            acc = acc + cw[t:t + 1, :] * shifted[HALO:HALO + ROW_BLOCK, :]
        uc_b = _silu(acc).astype(BF16)
        uc_ref[r0:r0 + ROW_BLOCK, :] = uc_b
        qs, ks, vs = [], [], []
        for t in range(width // BD_TILE):
            sl = slice(t * BD_TILE, (t + 1) * BD_TILE)
            qs.append(_dot(uc_b[:, sl], wq_ref[t]))
            ks.append(_dot(uc_b[:, sl], wk_ref[t]) * kscale)
            vs.append(_dot(cur_b[:, sl], wv_ref[t]))
        q_b = jnp.concatenate(qs, axis=1).astype(BF16)
        k_b = jnp.concatenate(ks, axis=1).astype(BF16)
        v_b = jnp.concatenate(vs, axis=1).astype(BF16)
        q_ref[r0:r0 + ROW_BLOCK, :] = q_b
        k_ref[r0:r0 + ROW_BLOCK, :] = k_b
        v_ref[r0:r0 + ROW_BLOCK, :] = v_b
        g_ref[r0:r0 + ROW_BLOCK, :] += _dot(q_b, gq_ref[...]) + _dot(k_b, gk_ref[...]) + _dot(v_b, gv_ref[...])

    @pl.when(pl.program_id(1) == pl.num_programs(1) - 1)
    def _():
        lane = lax.broadcasted_iota(jnp.int32, (ROW_BLOCK, g_ref.shape[1]), 1)
        forget = (lane < ngates) & (lane % (ngates // 2) >= ngates // 4)
        for j in range(nblk):
            pre = g_ref[j * ROW_BLOCK:(j + 1) * ROW_BLOCK, :]
            g_ref[j * ROW_BLOCK:(j + 1) * ROW_BLOCK, :] = jnp.where(forget, _log_sigmoid(pre), pre)


def _ml_proj(u, conv_w, conv_b, wq_bd, wk_bd, wv_bd, gq, gk, gv, gb, dh, kscale, ngates):
    bsz, t, inner = u.shape
    heads = inner // dh
    tiles = dh // BD_TILE
    taps = conv_w.shape[0]
    col = lambda shape: pl.BlockSpec(shape, lambda b, h: (0, h))
    seq = pl.BlockSpec((None, t, dh), lambda b, h: (b, 0, h))
    bd = pl.BlockSpec((tiles, BD_TILE, BD_TILE), lambda b, h: (h, 0, 0))
    gw = pl.BlockSpec((dh, LANE), lambda b, h: (h, 0))
    kern = functools.partial(_ml_proj_kernel, nblk=t // ROW_BLOCK, kscale=kscale, ngates=ngates)
    return pl.pallas_call(
        kern,
        grid=(bsz, heads),
        in_specs=[seq, col((taps, dh)), col((1, dh)), bd, bd, bd, gw, gw, gw,
                  pl.BlockSpec((1, LANE), lambda b, h: (0, 0))],
        out_specs=[seq, seq, seq, seq, pl.BlockSpec((None, t, LANE), lambda b, h: (b, 0, 0))],
        out_shape=[jax.ShapeDtypeStruct((bsz, t, inner), BF16)] * 4 + [jax.ShapeDtypeStruct((bsz, t, LANE), F32)],
        compiler_params=_params(("arbitrary", "arbitrary")),
        name="ml_proj",
    )(u, conv_w, conv_b, wq_bd, wk_bd, wv_bd, gq, gk, gv, gb)


def _ml_chunk(d, q, k, v, g, ct_ref, n_ref, m_ref, ci, cf):
    L = q.shape[0]
    gt = g.T
    lane = lax.broadcasted_iota(jnp.int32, g.shape, 1)
    sub = lax.broadcasted_iota(jnp.int32, gt.shape, 0)
    li_col = jnp.sum(jnp.where(lane == ci, g, 0.0), axis=1, keepdims=True)
    lf_col = jnp.sum(jnp.where(lane == cf, g, 0.0), axis=1, keepdims=True)
    li_row = jnp.sum(jnp.where(sub == ci, gt, 0.0), axis=0, keepdims=True)
    lf_row = jnp.sum(jnp.where(sub == cf, gt, 0.0), axis=0, keepdims=True)

    tt = lax.broadcasted_iota(jnp.int32, (L, L), 0)
    ss = lax.broadcasted_iota(jnp.int32, (L, L), 1)
    seen = (ss >= tt) if d else (ss <= tt)
    seen_t = (tt >= ss) if d else (tt <= ss)
    b_col = jnp.sum(jnp.where(seen, lf_row, 0.0), axis=1, keepdims=True)
    b_row = jnp.sum(jnp.where(seen_t, lf_col, 0.0), axis=0, keepdims=True)
    b_last = jnp.sum(lf_row, axis=1, keepdims=True)

    m_prev = m_ref[...]
    r_row = li_row - b_row
    m_t = b_col + jnp.maximum(m_prev, jnp.max(jnp.where(seen, r_row, -jnp.inf), axis=1, keepdims=True))
    sc = _dot_nt(q, k) * jnp.exp(jnp.where(seen, (b_col - m_t) + r_row, -jnp.inf))
    w_inter = jnp.exp(b_col + m_prev - m_t)
    n_rows = jnp.broadcast_to(n_ref[...], (2 * HALO, n_ref.shape[1])).astype(BF16)
    qn = _dot_nt(q, n_rows)[:, :1]
    num = _dot(sc.astype(BF16), v) + w_inter * _dot(q, ct_ref[...].astype(BF16))
    den = jnp.sum(sc, axis=1, keepdims=True) + w_inter * qn
    hh = num * (1.0 / jnp.maximum(jnp.abs(den), jnp.exp(-m_t)))

    g_row = b_last + r_row
    g_col = b_last - b_col + li_col
    m_new = jnp.maximum(b_last + m_prev, jnp.max(g_row, axis=1, keepdims=True))
    decay = jnp.exp(b_last + m_prev - m_new)
    vw = (v.astype(F32) * jnp.exp(g_col - m_new)).astype(BF16)
    wk_rows = jnp.broadcast_to(jnp.exp(g_row - m_new), (2 * HALO, L)).astype(BF16)
    ct_ref[...] = decay * ct_ref[...] + _dot_tn(k, vw)
    n_ref[...] = decay * n_ref[...] + _dot(wk_rows, k)[:1, :]
    m_ref[...] = m_new
    return hh


def _bwd_block(s, nblk):
    return jnp.where(s == 0, 0, nblk - s)


def _ml_core_kernel(qf_ref, kf_ref, vf_ref, gf_ref, qb_ref, kb_ref, vb_ref, gb_ref, o_ref, hacc_ref, *state,
                    heads, nblk, dh):
    s = pl.program_id(2)
    L = qf_ref.shape[0]
    per_step = len(state) // 6

    @pl.when(s == 0)
    def _():
        for ref in (hacc_ref,) + state:
            ref[...] = jnp.zeros_like(ref)

    rows_f = pl.ds(pl.multiple_of(s * L, L), L)
    rows_b = pl.ds(pl.multiple_of(_bwd_block(s, nblk) * L, L), L)
    gf, gb = gf_ref[...], gb_ref[...]
    for i in range(per_step):
        h = pl.program_id(1) * per_step + i
        cols = slice(i * dh, (i + 1) * dh)
        ctf, nf, mf, ctb, nb, mb = state[6 * i:6 * i + 6]
        hacc_ref[rows_f, cols] += _ml_chunk(0, qf_ref[:, cols], kf_ref[:, cols], vf_ref[:, cols], gf,
                                            ctf, nf, mf, h, h + heads)
        hacc_ref[rows_b, cols] += _ml_chunk(1, qb_ref[:, cols], kb_ref[:, cols], vb_ref[:, cols], gb,
                                            ctb, nb, mb, 2 * heads + h, 3 * heads + h)

    @pl.when(s == nblk - 1)
    def _():
        for j in range(nblk):
            for i in range(per_step):
                tot = hacc_ref[j * L:(j + 1) * L, i * dh:(i + 1) * dh]
                cen = tot - jnp.mean(tot, axis=-1, keepdims=True)
                var = jnp.mean(cen * cen, axis=-1, keepdims=True)
                o_ref[j * L:(j + 1) * L, i * dh:(i + 1) * dh] = (cen * lax.rsqrt(var + EPS)).astype(BF16)


def _ml_core(q, k, v, gates, dh):
    bsz, t, inner = q.shape
    heads = inner // dh
    nblk = t // ROW_BLOCK
    per_step = ML_HEADS_PER_STEP
    assert heads % per_step == 0
    width = per_step * dh
    fwd = pl.BlockSpec((None, ROW_BLOCK, width), lambda b, h, s: (b, s, h))
    bwd = pl.BlockSpec((None, ROW_BLOCK, width), lambda b, h, s: (b, _bwd_block(s, nblk), h))
    gate_f = pl.BlockSpec((None, ROW_BLOCK, LANE), lambda b, h, s: (b, s, 0))
    gate_b = pl.BlockSpec((None, ROW_BLOCK, LANE), lambda b, h, s: (b, _bwd_block(s, nblk), 0))
    state = [pltpu.VMEM((dh, dh), F32), pltpu.VMEM((1, dh), F32), pltpu.VMEM((1, 1), F32)]
    kern = functools.partial(_ml_core_kernel, heads=heads, nblk=nblk, dh=dh)
    return pl.pallas_call(
        kern,
        grid=(bsz, heads // per_step, nblk),
        in_specs=[fwd, fwd, fwd, gate_f, bwd, bwd, bwd, gate_b],
        out_specs=pl.BlockSpec((None, t, width), lambda b, h, s: (b, 0, h)),
        out_shape=jax.ShapeDtypeStruct((bsz, t, inner), BF16),
        scratch_shapes=[pltpu.VMEM((t, width), F32)] + state * (2 * per_step),
        compiler_params=_params(("arbitrary",) * 3),
        name="ml_core",
    )(q, k, v, gates, q, k, v, gates)


def _ml_out_kernel(hn_ref, uc_ref, z_ref, *refs, nres):
    res, (mod_ref, hg_ref, sk_ref, w_ref, o_ref) = refs[:nres], refs[nres:]
    z = z_ref[...].astype(F32)
    y = (hn_ref[...].astype(F32) * hg_ref[...] + sk_ref[...] * uc_ref[...].astype(F32)) * _silu(z)
    o_ref[...] = _residual_block(res) + mod_ref[2:3, :] * _dot(y.astype(BF16), w_ref[...])


def _ml_out(rows, hn, uc, z, r, mods, hnorm_g, skip, w_out):
    bsz, t, d = _residual_shape(r)
    inner = hn.shape[-1]
    res_specs, res_args = rows.residual(r)
    return pl.pallas_call(
        functools.partial(_ml_out_kernel, nres=len(res_args)),
        grid=rows.grid,
        in_specs=[rows.rows(inner), rows.rows(inner), rows.rows(inner)] + res_specs
        + [rows.mod(d), _const2((1, inner)), _const2((1, inner)), _const2((inner, d))],
        out_specs=rows.rows(d),
        out_shape=jax.ShapeDtypeStruct((bsz, t, d), F32),
        compiler_params=_params(("arbitrary", "arbitrary")),
        name="ml_out",
    )(hn, uc, z, *res_args, mods, hnorm_g, skip, w_out)


def _na_attn_kernel(q_ref, k_ref, v_ref, *rest, first, ctx_len, rows, groups):
    b_refs = rest[:NA_ROWS_PER_STEP]
    r_ref, mod_ref, w_ref, bo_ref, o_ref, a_buf = rest[NA_ROWS_PER_STEP:]
    gw = GRID_W
    per = NA_GROUP // NA_HEAD_DIM
    nloc = NA_KH * gw
    lane = lax.broadcasted_iota(jnp.int32, (gw, NA_GROUP), 1)
    mine = [(lane >= h * NA_HEAD_DIM) & (lane < (h + 1) * NA_HEAD_DIM) for h in range(per)]

    def key_start(sub):
        t = (pl.program_id(1) + first) * NA_ROWS_PER_STEP + sub
        r = t - ctx_len // gw
        rs = jnp.clip(r - NA_KH // 2, 0, rows - NA_KH)
        return pl.multiple_of(ctx_len + rs * gw, gw)

    kstart = [key_start(sub) for sub in range(NA_ROWS_PER_STEP)]

    def scores(sub, g):
        qg = q_ref[g, sub * gw:(sub + 1) * gw, :].astype(F32)
        qm = jnp.concatenate([jnp.where(mine[h], qg, 0.0) for h in range(per)], axis=0).astype(BF16)
        return _dot_nt(qm, k_ref[g, pl.ds(kstart[sub], nloc), :]), _dot_nt(qm, k_ref[g, 0:ctx_len, :])

    work = [(sub, g) for sub in range(NA_ROWS_PER_STEP) for g in range(groups)]
    ahead = scores(*work[0])
    for i, (sub, g) in enumerate(work):
        s_loc, s_ctx = ahead
        if i + 1 < len(work):
            ahead = scores(*work[i + 1])
        s_loc = s_loc + b_refs[sub][per * g:per * (g + 1)].reshape(per * gw, nloc)
        m = jnp.maximum(jnp.max(s_loc, axis=1, keepdims=True), jnp.max(s_ctx, axis=1, keepdims=True))
        p_loc = jnp.exp(s_loc - m)
        p_ctx = jnp.exp(s_ctx - m)
        den = jnp.sum(p_loc, axis=1, keepdims=True) + jnp.sum(p_ctx, axis=1, keepdims=True)
        o = (_dot(p_loc.astype(BF16), v_ref[g, pl.ds(kstart[sub], nloc), :])
             + _dot(p_ctx.astype(BF16), v_ref[g, 0:ctx_len, :])) * (1.0 / den)
        og = jnp.where(mine[0], o[0:gw], 0.0)
        for h in range(1, per):
            og = jnp.where(mine[h], o[h * gw:(h + 1) * gw], og)
        a_buf[sub * gw:(sub + 1) * gw, g * NA_GROUP:(g + 1) * NA_GROUP] = og.astype(BF16)

    o_ref[...] = r_ref[...] + mod_ref[2:3, :] * (_dot(a_buf[...], w_ref[...]) + bo_ref[...])


def _na_attn(live_rows, q, k, v, bias, r, mods, w_out, b_out, ctx_len):
    bsz, groups, t, _ = q.shape
    d = groups * NA_GROUP
    rows = (t - ctx_len) // GRID_W
    blk = NA_ROWS_PER_STEP * GRID_W
    assert blk == ROW_BLOCK and ctx_len % blk == 0 and t % blk == 0
    first = live_rows.first
    steps = t // blk - first
    heads, nvar = bias.shape[:2]

    def variant(tq):
        r = tq - ctx_len // GRID_W
        rs = jnp.clip(r - NA_KH // 2, 0, rows - NA_KH)
        return jnp.where(r < 0, nvar - 1, rs - r + NA_KH - 1)

    def bias_spec(sub):
        return pl.BlockSpec((heads, None) + bias.shape[2:],
                            lambda b, j: (0, variant((j + first) * NA_ROWS_PER_STEP + sub), 0, 0))

    whole = pl.BlockSpec((None, groups, t, NA_GROUP), lambda b, j: (b, 0, 0, 0))
    kern = functools.partial(_na_attn_kernel, first=first, ctx_len=ctx_len, rows=rows, groups=groups)
    return pl.pallas_call(
        kern,
        grid=(bsz, steps),
        in_specs=[pl.BlockSpec((None, groups, blk, NA_GROUP), lambda b, j: (b, 0, j + first, 0)), whole, whole]
        + [bias_spec(sub) for sub in range(NA_ROWS_PER_STEP)]
        + [live_rows.rows(d), live_rows.mod(d), _const2((d, d)), _const2((1, d))],
        out_specs=live_rows.rows(d),
        out_shape=jax.ShapeDtypeStruct((bsz, t, d), F32),
        scratch_shapes=[pltpu.VMEM((blk, d), BF16)],
        compiler_params=_params(("arbitrary", "arbitrary")),
        name="na_attn",
    )(q, k, v, *([bias] * NA_ROWS_PER_STEP), r, mods, w_out, b_out)


def _na_bias_table(rpb):
    heads, nrow, ncol = rpb.shape
    col = np.arange(GRID_W)
    col_start = np.clip(col - NA_KW // 2, 0, GRID_W - NA_KW)
    col_ok = (col[None, :] >= col_start[:, None]) & (col[None, :] < col_start[:, None] + NA_KW)
    dcol = np.clip(col[None, :] - col[:, None] + NA_KW - 1, 0, ncol - 1)
    full = jnp.where(jnp.asarray(col_ok[None, :, None, :]),
                     rpb.astype(F32)[:, :, jnp.asarray(dcol)].transpose(0, 2, 1, 3), -jnp.inf)
    nloc = NA_KH * GRID_W
    var = [full[:, :, o:o + NA_KH, :].reshape(heads, GRID_W, nloc) for o in range(NA_KH)]
    var.append(jnp.full((heads, GRID_W, nloc), -jnp.inf, F32))
    return jnp.stack(var, axis=1)


def _ffn_kernel(x_ref, xp_ref, xn_ref, mod_ref, g_ref, wu_ref, bu_ref, cw_ref, cb_ref, wd_ref, bd_ref,
                *rest, first, nblk_total, final):
    if final:
        fg_ref, o_ref, h_buf, acc_ref, *u_bufs = rest
    else:
        o_ref, h_buf, acc_ref, *u_bufs = rest
    j = pl.program_id(1) + first
    x = x_ref[...]
    ext = jnp.concatenate([xp_ref[...], x, xn_ref[...]], axis=0)
    h_buf[...] = _norm_mod(ext, g_ref[...], mod_ref[3:4, :], mod_ref[4:5, :]).astype(BF16)
    has_prev = j >= 2
    has_next = (j >= 1) & (j < nblk_total - 1)
    nchunk = wd_ref.shape[0]
    taps = cw_ref.shape[1]
    acc_ref[...] = jnp.zeros_like(acc_ref)

    def up(c, slot):
        for half in range(2):
            cc = half * nchunk + c
            raw = _dot(h_buf[...], wu_ref[cc])
            buf = u_bufs[2 * slot + half]
            buf[HALO:HALO + ROW_BLOCK, :] = raw[HALO:HALO + ROW_BLOCK, :]
            buf[0:HALO, :] = jnp.where(has_prev, raw[0:HALO, :], -bu_ref[cc])
            buf[HALO + ROW_BLOCK:, :] = jnp.where(has_next, raw[HALO + ROW_BLOCK:, :], -bu_ref[cc])

    def down(c, slot):
        halves = []
        for half in range(2):
            cc = half * nchunk + c
            y = cb_ref[cc] + bu_ref[cc] * jnp.sum(cw_ref[cc], axis=0, keepdims=True)
            u = u_bufs[2 * slot + half][...]
            for t in range(taps):
                shifted = u if t == taps // 2 else pltpu.roll(u, (taps // 2 - t) % u.shape[0], axis=0)
                y = y + cw_ref[cc, t:t + 1, :] * shifted[HALO:HALO + ROW_BLOCK, :]
            halves.append(y)
        act = (halves[0] * _silu(halves[1])).astype(BF16)
        acc_ref[...] += _dot(act, wd_ref[c])

    def stage(c, slot):
        up(c + 1, 1 - slot)
        down(c, slot)

    unroll = FF_UNROLL
    assert unroll % 2 == 0
    up(0, 0)

    def body(i, carry):
        for k in range(unroll):
            stage(unroll * i + k, k % 2)
        return carry

    looped = (nchunk - 1) // unroll
    lax.fori_loop(0, looped, body, 0)
    for c in range(looped * unroll, nchunk - 1):
        stage(c, c % 2)
    down(nchunk - 1, (nchunk - 1) % 2)
    o = x + mod_ref[5:6, :] * (acc_ref[...] + bd_ref[...])
    if final:
        o = o * lax.rsqrt(jnp.mean(o * o, axis=-1, keepdims=True) + EPS) * fg_ref[...]
    o_ref[...] = o


def _ffn(rows, r, mods, g, w_up, b_up, conv_w, conv_b, w_down, b_down, final_g=None):
    bsz, t, d = r.shape
    f = rows.first
    nblk_total = rows.nblk_total
    per = ROW_BLOCK // HALO
    last = t // HALO - 1
    prev = pl.BlockSpec((None, HALO, d), lambda b, j: (b, jnp.maximum((j + f) * per - 1, f * per), 0))
    nxt = pl.BlockSpec((None, HALO, d), lambda b, j: (b, jnp.minimum((j + f + 1) * per, last), 0))
    hidden2 = w_up.shape[1]
    nc2 = hidden2 // FF_CHUNK
    taps = conv_w.shape[0]
    final = final_g is not None
    w_up = w_up.reshape(d, nc2, FF_CHUNK).transpose(1, 0, 2)
    b_up = b_up.reshape(nc2, 1, FF_CHUNK)
    conv_w = conv_w.reshape(taps, nc2, FF_CHUNK).transpose(1, 0, 2)
    conv_b = conv_b.reshape(nc2, 1, FF_CHUNK)
    w_down = w_down.reshape(nc2 // 2, FF_CHUNK, d)
    in_specs = [rows.rows(d), prev, nxt, rows.mod(d), _const2((1, d)), _const2(w_up.shape), _const2(b_up.shape),
                _const2(conv_w.shape), _const2(conv_b.shape), _const2(w_down.shape), _const2((1, d))]
    args = [r, r, r, mods, g, w_up, b_up, conv_w, conv_b, w_down, b_down]
    if final:
        in_specs.append(_const2((1, d)))
        args.append(final_g)
        out_spec = pl.BlockSpec((None, ROW_BLOCK, d), lambda b, j: (b, j, 0))
        out_shape = jax.ShapeDtypeStruct((bsz, t - f * ROW_BLOCK, d), F32)
    else:
        out_spec = rows.rows(d)
        out_shape = jax.ShapeDtypeStruct((bsz, t, d), F32)
    kern = functools.partial(_ffn_kernel, first=f, nblk_total=nblk_total, final=final)
    return pl.pallas_call(
        kern,
        grid=rows.grid,
        in_specs=in_specs,
        out_specs=out_spec,
        out_shape=out_shape,
        scratch_shapes=[pltpu.VMEM((ROW_BLOCK + 2 * HALO, d), BF16), pltpu.VMEM((ROW_BLOCK, d), F32)]
        + [pltpu.VMEM((ROW_BLOCK + 2 * HALO, FF_CHUNK), F32)] * 4,
        compiler_params=_params(("arbitrary", "arbitrary")),
        name="ffn",
    )(*args)


def _block_diag_tiles(w):
    nb, bs, _ = w.shape
    per = BD_TILE // bs
    rep = jnp.tile(w.reshape(nb * bs, bs), (1, per)).reshape(nb // per, BD_TILE, BD_TILE)
    pos = np.arange(BD_TILE) // bs
    return jnp.where(jnp.asarray(pos[:, None] == pos[None, :]), rep, 0.0).astype(BF16)


def _gate_cols(gate_w, gate_b, inner):
    dirs, _, ng = gate_w.shape
    g = gate_w.transpose(1, 0, 2).reshape(3 * inner, dirs * ng)
    g = jnp.pad(g, ((0, 0), (0, LANE - dirs * ng))).astype(BF16)
    gb = jnp.pad(gate_b.reshape(1, dirs * ng), ((0, 0), (0, LANE - dirs * ng)))
    return g[:inner], g[inner:2 * inner], g[2 * inner:], gb


def kernel(x, c, ctx, c_ctx, ada_w, ada_b, norm1_g, norm2_g, final_g, ml_w_in, ml_conv_w, ml_conv_b, ml_wq, ml_wk, ml_wv, ml_gate_w, ml_gate_b, ml_skip, ml_hnorm_g, ml_w_out, na_w_qkv, na_b_qkv, na_rpb, na_w_out, na_b_out, ff_w_up, ff_b_up, ff_conv_w, ff_conv_b, ff_w_down, ff_b_down):
    bsz, seq, d = x.shape
    ctx_len = ctx.shape[1]
    depth = ada_w.shape[0]
    assert ctx_len == ROW_BLOCK and seq % ROW_BLOCK == 0 and seq % GRID_W == 0
    assert bsz + 1 <= MOD_ROWS
    t = ctx_len + seq
    nblk = t // ROW_BLOCK

    c_all = jnp.zeros((MOD_ROWS, d), F32).at[:bsz].set(c).at[bsz].set(c_ctx)
    mods = _ada(c_all, ada_w, ada_b).reshape(depth, MOD_ROWS, 6, d)

    r = (ctx, x)
    row2 = lambda a: a.reshape(1, -1)
    out = None
    for i in range(depth):
        need_ctx = i < depth - 1
        j = i // 2
        all_rows = _Rows(bsz, nblk, i, 0)
        live_rows = all_rows if need_ctx else _Rows(bsz, nblk, i, 1)
        if i % 2 == 0:
            inner = ml_w_in.shape[2] // 2
            dh = inner // ML_HEADS
            u, z = _ml_in(all_rows, r, mods, row2(norm1_g[i]), ml_w_in[j].astype(BF16))
            gq, gk, gv, gb = _gate_cols(ml_gate_w[j], ml_gate_b[j], inner)
            uc, q, k, v, gates = _ml_proj(u, ml_conv_w[j], row2(ml_conv_b[j]), _block_diag_tiles(ml_wq[j]),
                                          _block_diag_tiles(ml_wk[j]), _block_diag_tiles(ml_wv[j]),
                                          gq, gk, gv, gb, dh, dh ** -0.5,
                                          ml_gate_w.shape[1] * ml_gate_w.shape[3])
            hn = _ml_core(q, k, v, gates, dh)
            r = _ml_out(live_rows, hn, uc, z, r, mods, row2(ml_hnorm_g[j]), row2(ml_skip[j]),
                        ml_w_out[j].astype(BF16))
        else:
            q, k, v = _na_qkv(all_rows, r, mods, row2(norm1_g[i]), na_w_qkv[j].astype(BF16), row2(na_b_qkv[j]),
                              NA_HEAD_DIM ** -0.5)
            r = _na_attn(live_rows, q, k, v, _na_bias_table(na_rpb[j]), r, mods, na_w_out[j].astype(BF16),
                         row2(na_b_out[j]), ctx_len)
        ffn_args = (live_rows, r, mods, row2(norm2_g[i]), ff_w_up[i].astype(BF16), row2(ff_b_up[i]), ff_conv_w[i],
                    row2(ff_conv_b[i]), ff_w_down[i].astype(BF16), row2(ff_b_down[i]))
        if need_ctx:
            r = _ffn(*ffn_args)
        else:
            out = _ffn(*ffn_args, final_g=row2(final_g))
    return out
```

```python
import functools

import jax
import jax.numpy as jnp
import numpy as np
from jax import lax
from jax.experimental import pallas as pl
from jax.experimental.pallas import tpu as pltpu

F32 = jnp.float32
BF16 = jnp.bfloat16
EPS = 1e-6

GRID_W = 64
ML_HEADS = 4
ML_QKV_BLOCK = 4
NA_HEAD_DIM = 64
NA_KH = 8
NA_KW = 16

ROW_BLOCK = 256
HALO = 8
BD_TILE = 256
LANE = 128
NA_GROUP = 256
NA_ROWS_PER_STEP = 4
ML_HEADS_PER_STEP = 2
FF_CHUNK = 256
FF_UNROLL = 4
FF_ROWS = 512
MOD_ROWS = 16
VMEM_LIMIT = 56 * 1024 * 1024


def _dot(a, b):
    return jnp.dot(a, b, preferred_element_type=F32)


def _dot_nt(a, b):
    return lax.dot_general(a, b, (((1,), (1,)), ((), ())), preferred_element_type=F32)


def _dot_tn(a, b):
    return lax.dot_general(a, b, (((0,), (0,)), ((), ())), preferred_element_type=F32)


def _silu(x):
    return x * jax.nn.sigmoid(x)


def _log_sigmoid(x):
    return jnp.minimum(x, 0.0) - jnp.log1p(jnp.exp(-jnp.abs(x)))


def _norm_mod(x, g, shift, scale):
    y = x * lax.rsqrt(jnp.mean(x * x, axis=-1, keepdims=True) + EPS) * g
    return y * (1.0 + scale) + shift


def _params(sem):
    return pltpu.CompilerParams(dimension_semantics=sem, vmem_limit_bytes=VMEM_LIMIT)


def _resident(shape, index_map):
    return pl.BlockSpec(shape, index_map, pipeline_mode=pl.Buffered(1))


def _ada_kernel(c_ref, w_ref, b_ref, o_ref):
    s = _silu(c_ref[...]).astype(BF16)
    o_ref[...] = _dot(s, w_ref[...].astype(BF16)) + b_ref[...]


def _ada(c_all, ada_w, ada_b):
    depth, d, d6 = ada_w.shape
    n = d6 // d
    return pl.pallas_call(
        _ada_kernel,
        grid=(depth, n),
        in_specs=[
            pl.BlockSpec((MOD_ROWS, d), lambda l, j: (0, 0)),
            pl.BlockSpec((None, d, d), lambda l, j: (l, 0, j)),
            pl.BlockSpec((None, 1, d), lambda l, j: (l, 0, j)),
        ],
        out_specs=pl.BlockSpec((None, MOD_ROWS, d), lambda l, j: (l, 0, j)),
        out_shape=jax.ShapeDtypeStruct((depth, MOD_ROWS, d6), F32),
        compiler_params=_params(("arbitrary", "arbitrary")),
        name="ada",
    )(c_all, ada_w, ada_b.reshape(depth, 1, d6))


class _Rows:
    def __init__(self, bsz, nblk_total, layer, first_blk):
        self.bsz = bsz
        self.nblk_total = nblk_total
        self.layer = layer
        self.first = first_blk
        self.grid = (bsz, nblk_total - first_blk)

    def rows(self, width, rows=ROW_BLOCK):
        f = self.first
        return pl.BlockSpec((None, rows, width), lambda b, j: (b, j + f, 0))

    def residual(self, r):
        if not isinstance(r, tuple):
            return [self.rows(r.shape[-1])], [r]
        assert self.first == 0
        d = r[1].shape[-1]
        return ([pl.BlockSpec((None, ROW_BLOCK, d), lambda b, j: (b, 0, 0)),
                 pl.BlockSpec((None, ROW_BLOCK, d), lambda b, j: (b, jnp.maximum(j - 1, 0), 0))], list(r))

    def mod(self, d):
        f, bsz, layer = self.first, self.bsz, self.layer
        return pl.BlockSpec((None, None, 6, d), lambda b, j: (layer, jnp.where(j + f == 0, bsz, b), 0, 0))


def _const2(shape):
    return _resident(shape, lambda b, j: (0,) * len(shape))


def _residual_block(refs):
    if len(refs) == 1:
        return refs[0][...]
    return jnp.where(pl.program_id(1) == 0, refs[0][...], refs[1][...])


def _residual_shape(r):
    if isinstance(r, tuple):
        return r[1].shape[0], r[0].shape[1] + r[1].shape[1], r[1].shape[2]
    return r.shape


def _proj_kernel(*refs, nres, shift, scale, chunk, splits, has_bias, out_scale):
    res, (mod_ref, g_ref, w_ref), rest = refs[:nres], refs[nres:nres + 3], refs[nres + 3:]
    if has_bias:
        b_ref, out_refs = rest[0], rest[1:]
    else:
        b_ref, out_refs = None, rest
    h = _norm_mod(_residual_block(res), g_ref[...], mod_ref[shift:shift + 1, :],
                  mod_ref[scale:scale + 1, :]).astype(BF16)
    n = w_ref.shape[1]
    per_out = n // len(out_refs)
    for c in range(n // chunk):
        y = _dot(h, w_ref[:, c * chunk:(c + 1) * chunk])
        if has_bias:
            y = y + b_ref[:, c * chunk:(c + 1) * chunk]
        oi, off = divmod(c * chunk, per_out)
        if out_scale[oi] != 1.0:
            y = y * out_scale[oi]
        y = y.astype(BF16)
        o_ref = out_refs[oi]
        if splits:
            for t in range(chunk // splits):
                o_ref[(off + t * splits) // splits] = y[:, t * splits:(t + 1) * splits]
        else:
            o_ref[:, off:off + chunk] = y


def _ml_in(rows, r, mods, g, w_in):
    bsz, t, d = _residual_shape(r)
    n = w_in.shape[1]
    half = n // 2
    res_specs, res_args = rows.residual(r)
    kern = functools.partial(_proj_kernel, nres=len(res_args), shift=0, scale=1, chunk=512, splits=0, has_bias=False,
                             out_scale=(1.0, 1.0))
    return pl.pallas_call(
        kern,
        grid=rows.grid,
        in_specs=res_specs + [rows.mod(d), _const2((1, d)), _const2((d, n))],
        out_specs=[rows.rows(half), rows.rows(half)],
        out_shape=[jax.ShapeDtypeStruct((bsz, t, half), BF16)] * 2,
        compiler_params=_params(("arbitrary", "arbitrary")),
        name="ml_in",
    )(*res_args, mods, g, w_in)


def _na_qkv(rows, r, mods, g, w, b, q_scale):
    bsz, t, d = r.shape
    groups = d // NA_GROUP
    f = rows.first
    out_spec = pl.BlockSpec((None, groups, ROW_BLOCK, NA_GROUP), lambda bb, j: (bb, 0, j + f, 0))
    kern = functools.partial(_proj_kernel, nres=1, shift=0, scale=1, chunk=NA_GROUP, splits=NA_GROUP, has_bias=True,
                             out_scale=(q_scale, 1.0, 1.0))
    return pl.pallas_call(
        kern,
        grid=rows.grid,
        in_specs=[rows.rows(d), rows.mod(d), _const2((1, d)), _const2((d, 3 * d)), _const2((1, 3 * d))],
        out_specs=[out_spec] * 3,
        out_shape=[jax.ShapeDtypeStruct((bsz, groups, t, NA_GROUP), BF16)] * 3,
        compiler_params=_params(("arbitrary", "arbitrary")),
        name="na_qkv",
    )(r, mods, g, w, b)


def _ml_proj_kernel(u_ref, cw_ref, cb_ref, wq_ref, wk_ref, wv_ref, gq_ref, gk_ref, gv_ref, gb_ref,
                    uc_ref, q_ref, k_ref, v_ref, g_ref, *, nblk, kscale, ngates):
    @pl.when(pl.program_id(1) == 0)
    def _():
        g_ref[...] = jnp.broadcast_to(gb_ref[...], g_ref.shape)

    cw = cw_ref[...]
    cb = cb_ref[...]
    taps = cw.shape[0]
    width = u_ref.shape[1]
    pad = 2 * HALO
    for j in range(nblk):
        r0 = j * ROW_BLOCK
        cur_b = u_ref[r0:r0 + ROW_BLOCK, :]
        cur = cur_b.astype(F32)
        if j in (0, 1):
            prev = jnp.zeros((HALO, width), F32)
        else:
            prev = u_ref[r0 - pad:r0, :].astype(F32)[HALO:, :]
        if j in (0, nblk - 1):
            nxt = jnp.zeros((HALO, width), F32)
        else:
            nxt = u_ref[r0 + ROW_BLOCK:r0 + ROW_BLOCK + pad, :].astype(F32)[:HALO, :]
        ext = jnp.concatenate([prev, cur, nxt], axis=0)
        acc = cb
        for t in range(taps):
            shifted = ext if t == taps // 2 else pltpu.roll(ext, (taps // 2 - t) % ext.shape[0], axis=0)
            acc = acc + cw[t:t + 1, :] * shifted[HALO:HALO + ROW_BLOCK, :]
        uc_b = _silu(acc).astype(BF16)
        uc_ref[r0:r0 + ROW_BLOCK, :] = uc_b
        qs, ks, vs = [], [], []
        for t in range(width // BD_TILE):
            sl = slice(t * BD_TILE, (t + 1) * BD_TILE)
            qs.append(_dot(uc_b[:, sl], wq_ref[t]))
            ks.append(_dot(uc_b[:, sl], wk_ref[t]) * kscale)
            vs.append(_dot(cur_b[:, sl], wv_ref[t]))
        q_b = jnp.concatenate(qs, axis=1).astype(BF16)
        k_b = jnp.concatenate(ks, axis=1).astype(BF16)
        v_b = jnp.concatenate(vs, axis=1).astype(BF16)
        q_ref[r0:r0 + ROW_BLOCK, :] = q_b
        k_ref[r0:r0 + ROW_BLOCK, :] = k_b
        v_ref[r0:r0 + ROW_BLOCK, :] = v_b
        g_ref[r0:r0 + ROW_BLOCK, :] += _dot(q_b, gq_ref[...]) + _dot(k_b, gk_ref[...]) + _dot(v_b, gv_ref[...])

    @pl.when(pl.program_id(1) == pl.num_programs(1) - 1)
    def _():
        lane = lax.broadcasted_iota(jnp.int32, (ROW_BLOCK, g_ref.shape[1]), 1)
        forget = (lane < ngates) & (lane % (ngates // 2) >= ngates // 4)
        for j in range(nblk):
            pre = g_ref[j * ROW_BLOCK:(j + 1) * ROW_BLOCK, :]
            g_ref[j * ROW_BLOCK:(j + 1) * ROW_BLOCK, :] = jnp.where(forget, _log_sigmoid(pre), pre)


def _ml_proj(u, conv_w, conv_b, wq_bd, wk_bd, wv_bd, gq, gk, gv, gb, dh, kscale, ngates):
    bsz, t, inner = u.shape
    heads = inner // dh
    tiles = dh // BD_TILE
    taps = conv_w.shape[0]
    col = lambda shape: pl.BlockSpec(shape, lambda b, h: (0, h))
    seq = pl.BlockSpec((None, t, dh), lambda b, h: (b, 0, h))
    bd = pl.BlockSpec((tiles, BD_TILE, BD_TILE), lambda b, h: (h, 0, 0))
    gw = pl.BlockSpec((dh, LANE), lambda b, h: (h, 0))
    kern = functools.partial(_ml_proj_kernel, nblk=t // ROW_BLOCK, kscale=kscale, ngates=ngates)
    return pl.pallas_call(
        kern,
        grid=(bsz, heads),
        in_specs=[seq, col((taps, dh)), col((1, dh)), bd, bd, bd, gw, gw, gw,
                  pl.BlockSpec((1, LANE), lambda b, h: (0, 0))],
        out_specs=[seq, seq, seq, seq, pl.BlockSpec((None, t, LANE), lambda b, h: (b, 0, 0))],
        out_shape=[jax.ShapeDtypeStruct((bsz, t, inner), BF16)] * 4 + [jax.ShapeDtypeStruct((bsz, t, LANE), F32)],
        compiler_params=_params(("arbitrary", "arbitrary")),
        name="ml_proj",
    )(u, conv_w, conv_b, wq_bd, wk_bd, wv_bd, gq, gk, gv, gb)


def _ml_chunk(d, q, k, v, g, ct_ref, n_ref, m_ref, ci, cf):
    L = q.shape[0]
    gt = g.T
    lane = lax.broadcasted_iota(jnp.int32, g.shape, 1)
    sub = lax.broadcasted_iota(jnp.int32, gt.shape, 0)
    li_col = jnp.sum(jnp.where(lane == ci, g, 0.0), axis=1, keepdims=True)
    lf_col = jnp.sum(jnp.where(lane == cf, g, 0.0), axis=1, keepdims=True)
    li_row = jnp.sum(jnp.where(sub == ci, gt, 0.0), axis=0, keepdims=True)
    lf_row = jnp.sum(jnp.where(sub == cf, gt, 0.0), axis=0, keepdims=True)

    tt = lax.broadcasted_iota(jnp.int32, (L, L), 0)
    ss = lax.broadcasted_iota(jnp.int32, (L, L), 1)
    seen = (ss >= tt) if d else (ss <= tt)
    seen_t = (tt >= ss) if d else (tt <= ss)
    b_col = jnp.sum(jnp.where(seen, lf_row, 0.0), axis=1, keepdims=True)
    b_row = jnp.sum(jnp.where(seen_t, lf_col, 0.0), axis=0, keepdims=True)
    b_last = jnp.sum(lf_row, axis=1, keepdims=True)

    m_prev = m_ref[...]
    r_row = li_row - b_row
    m_t = b_col + jnp.maximum(m_prev, jnp.max(jnp.where(seen, r_row, -jnp.inf), axis=1, keepdims=True))
    sc = _dot_nt(q, k) * jnp.exp(jnp.where(seen, (b_col - m_t) + r_row, -jnp.inf))
    w_inter = jnp.exp(b_col + m_prev - m_t)
    n_rows = jnp.broadcast_to(n_ref[...], (2 * HALO, n_ref.shape[1])).astype(BF16)
    qn = _dot_nt(q, n_rows)[:, :1]
    num = _dot(sc.astype(BF16), v) + w_inter * _dot(q, ct_ref[...].astype(BF16))
    den = jnp.sum(sc, axis=1, keepdims=True) + w_inter * qn
    hh = num * (1.0 / jnp.maximum(jnp.abs(den), jnp.exp(-m_t)))

    g_row = b_last + r_row
    g_col = b_last - b_col + li_col
    m_new = jnp.maximum(b_last + m_prev, jnp.max(g_row, axis=1, keepdims=True))
    decay = jnp.exp(b_last + m_prev - m_new)
    vw = (v.astype(F32) * jnp.exp(g_col - m_new)).astype(BF16)
    wk_rows = jnp.broadcast_to(jnp.exp(g_row - m_new), (2 * HALO, L)).astype(BF16)
    ct_ref[...] = decay * ct_ref[...] + _dot_tn(k, vw)
    n_ref[...] = decay * n_ref[...] + _dot(wk_rows, k)[:1, :]
    m_ref[...] = m_new
    return hh


def _bwd_block(s, nblk):
    return jnp.where(s == 0, 0, nblk - s)


def _ml_core_kernel(qf_ref, kf_ref, vf_ref, gf_ref, qb_ref, kb_ref, vb_ref, gb_ref, o_ref, hacc_ref, *state,
                    heads, nblk, dh):
    s = pl.program_id(2)
    L = qf_ref.shape[0]
    per_step = len(state) // 6

    @pl.when(s == 0)
    def _():
        for ref in (hacc_ref,) + state:
            ref[...] = jnp.zeros_like(ref)

    rows_f = pl.ds(pl.multiple_of(s * L, L), L)
    rows_b = pl.ds(pl.multiple_of(_bwd_block(s, nblk) * L, L), L)
    gf, gb = gf_ref[...], gb_ref[...]
    for i in range(per_step):
        h = pl.program_id(1) * per_step + i
        cols = slice(i * dh, (i + 1) * dh)
        ctf, nf, mf, ctb, nb, mb = state[6 * i:6 * i + 6]
        hacc_ref[rows_f, cols] += _ml_chunk(0, qf_ref[:, cols], kf_ref[:, cols], vf_ref[:, cols], gf,
                                            ctf, nf, mf, h, h + heads)
        hacc_ref[rows_b, cols] += _ml_chunk(1, qb_ref[:, cols], kb_ref[:, cols], vb_ref[:, cols], gb,
                                            ctb, nb, mb, 2 * heads + h, 3 * heads + h)

    @pl.when(s == nblk - 1)
    def _():
        for j in range(nblk):
            for i in range(per_step):
                tot = hacc_ref[j * L:(j + 1) * L, i * dh:(i + 1) * dh]
                cen = tot - jnp.mean(tot, axis=-1, keepdims=True)
                var = jnp.mean(cen * cen, axis=-1, keepdims=True)
                o_ref[j * L:(j + 1) * L, i * dh:(i + 1) * dh] = (cen * lax.rsqrt(var + EPS)).astype(BF16)


def _ml_core(q, k, v, gates, dh):
    bsz, t, inner = q.shape
    heads = inner // dh
    nblk = t // ROW_BLOCK
    per_step = ML_HEADS_PER_STEP
    assert heads % per_step == 0
    width = per_step * dh
    fwd = pl.BlockSpec((None, ROW_BLOCK, width), lambda b, h, s: (b, s, h))
    bwd = pl.BlockSpec((None, ROW_BLOCK, width), lambda b, h, s: (b, _bwd_block(s, nblk), h))
    gate_f = pl.BlockSpec((None, ROW_BLOCK, LANE), lambda b, h, s: (b, s, 0))
    gate_b = pl.BlockSpec((None, ROW_BLOCK, LANE), lambda b, h, s: (b, _bwd_block(s, nblk), 0))
    state = [pltpu.VMEM((dh, dh), F32), pltpu.VMEM((1, dh), F32), pltpu.VMEM((1, 1), F32)]
    kern = functools.partial(_ml_core_kernel, heads=heads, nblk=nblk, dh=dh)
    return pl.pallas_call(
        kern,
        grid=(bsz, heads // per_step, nblk),
        in_specs=[fwd, fwd, fwd, gate_f, bwd, bwd, bwd, gate_b],
        out_specs=pl.BlockSpec((None, t, width), lambda b, h, s: (b, 0, h)),
        out_shape=jax.ShapeDtypeStruct((bsz, t, inner), BF16),
        scratch_shapes=[pltpu.VMEM((t, width), F32)] + state * (2 * per_step),
        compiler_params=_params(("arbitrary",) * 3),
        name="ml_core",
    )(q, k, v, gates, q, k, v, gates)


def _ml_out_kernel(hn_ref, uc_ref, z_ref, *refs, nres):
    res, (mod_ref, hg_ref, sk_ref, w_ref, o_ref) = refs[:nres], refs[nres:]
    z = z_ref[...].astype(F32)
    y = (hn_ref[...].astype(F32) * hg_ref[...] + sk_ref[...] * uc_ref[...].astype(F32)) * _silu(z)
    o_ref[...] = _residual_block(res) + mod_ref[2:3, :] * _dot(y.astype(BF16), w_ref[...])


def _ml_out(rows, hn, uc, z, r, mods, hnorm_g, skip, w_out):
    bsz, t, d = _residual_shape(r)
    inner = hn.shape[-1]
    res_specs, res_args = rows.residual(r)
    return pl.pallas_call(
        functools.partial(_ml_out_kernel, nres=len(res_args)),
        grid=rows.grid,
        in_specs=[rows.rows(inner), rows.rows(inner), rows.rows(inner)] + res_specs
        + [rows.mod(d), _const2((1, inner)), _const2((1, inner)), _const2((inner, d))],
        out_specs=rows.rows(d),
        out_shape=jax.ShapeDtypeStruct((bsz, t, d), F32),
        compiler_params=_params(("arbitrary", "arbitrary")),
        name="ml_out",
    )(hn, uc, z, *res_args, mods, hnorm_g, skip, w_out)


def _na_attn_kernel(q_ref, k_ref, v_ref, *rest, first, ctx_len, rows, groups):
    b_refs = rest[:NA_ROWS_PER_STEP]
    r_ref, mod_ref, w_ref, bo_ref, o_ref, a_buf = rest[NA_ROWS_PER_STEP:]
    gw = GRID_W
    per = NA_GROUP // NA_HEAD_DIM
    nloc = NA_KH * gw
    lane = lax.broadcasted_iota(jnp.int32, (gw, NA_GROUP), 1)
    mine = [(lane >= h * NA_HEAD_DIM) & (lane < (h + 1) * NA_HEAD_DIM) for h in range(per)]

    def key_start(sub):
        t = (pl.program_id(1) + first) * NA_ROWS_PER_STEP + sub
        r = t - ctx_len // gw
        rs = jnp.clip(r - NA_KH // 2, 0, rows - NA_KH)
        return pl.multiple_of(ctx_len + rs * gw, gw)

    kstart = [key_start(sub) for sub in range(NA_ROWS_PER_STEP)]

    def scores(sub, g):
        qg = q_ref[g, sub * gw:(sub + 1) * gw, :].astype(F32)
        qm = jnp.concatenate([jnp.where(mine[h], qg, 0.0) for h in range(per)], axis=0).astype(BF16)
        return _dot_nt(qm, k_ref[g, pl.ds(kstart[sub], nloc), :]), _dot_nt(qm, k_ref[g, 0:ctx_len, :])

    work = [(sub, g) for sub in range(NA_ROWS_PER_STEP) for g in range(groups)]
    ahead = scores(*work[0])
    for i, (sub, g) in enumerate(work):
        s_loc, s_ctx = ahead
        if i + 1 < len(work):
            ahead = scores(*work[i + 1])
        s_loc = s_loc + b_refs[sub][per * g:per * (g + 1)].reshape(per * gw, nloc)
        m = jnp.maximum(jnp.max(s_loc, axis=1, keepdims=True), jnp.max(s_ctx, axis=1, keepdims=True))
        p_loc = jnp.exp(s_loc - m)
        p_ctx = jnp.exp(s_ctx - m)
        den = jnp.sum(p_loc, axis=1, keepdims=True) + jnp.sum(p_ctx, axis=1, keepdims=True)
        o = (_dot(p_loc.astype(BF16), v_ref[g, pl.ds(kstart[sub], nloc), :])
             + _dot(p_ctx.astype(BF16), v_ref[g, 0:ctx_len, :])) * (1.0 / den)
        og = jnp.where(mine[0], o[0:gw], 0.0)
        for h in range(1, per):
            og = jnp.where(mine[h], o[h * gw:(h + 1) * gw], og)
        a_buf[sub * gw:(sub + 1) * gw, g * NA_GROUP:(g + 1) * NA_GROUP] = og.astype(BF16)

    o_ref[...] = r_ref[...] + mod_ref[2:3, :] * (_dot(a_buf[...], w_ref[...]) + bo_ref[...])


def _na_attn(live_rows, q, k, v, bias, r, mods, w_out, b_out, ctx_len):
    bsz, groups, t, _ = q.shape
    d = groups * NA_GROUP
    rows = (t - ctx_len) // GRID_W
    blk = NA_ROWS_PER_STEP * GRID_W
    assert blk == ROW_BLOCK and ctx_len % blk == 0 and t % blk == 0
    first = live_rows.first
    steps = t // blk - first
    heads, nvar = bias.shape[:2]

    def variant(tq):
        r = tq - ctx_len // GRID_W
        rs = jnp.clip(r - NA_KH // 2, 0, rows - NA_KH)
        return jnp.where(r < 0, nvar - 1, rs - r + NA_KH - 1)

    def bias_spec(sub):
        return pl.BlockSpec((heads, None) + bias.shape[2:],
                            lambda b, j: (0, variant((j + first) * NA_ROWS_PER_STEP + sub), 0, 0))

    whole = pl.BlockSpec((None, groups, t, NA_GROUP), lambda b, j: (b, 0, 0, 0))
    kern = functools.partial(_na_attn_kernel, first=first, ctx_len=ctx_len, rows=rows, groups=groups)
    return pl.pallas_call(
        kern,
        grid=(bsz, steps),
        in_specs=[pl.BlockSpec((None, groups, blk, NA_GROUP), lambda b, j: (b, 0, j + first, 0)), whole, whole]
        + [bias_spec(sub) for sub in range(NA_ROWS_PER_STEP)]
        + [live_rows.rows(d), live_rows.mod(d), _const2((d, d)), _const2((1, d))],
        out_specs=live_rows.rows(d),
        out_shape=jax.ShapeDtypeStruct((bsz, t, d), F32),
        scratch_shapes=[pltpu.VMEM((blk, d), BF16)],
        compiler_params=_params(("arbitrary", "arbitrary")),
        name="na_attn",
    )(q, k, v, *([bias] * NA_ROWS_PER_STEP), r, mods, w_out, b_out)


def _na_bias_table(rpb):
    heads, nrow, ncol = rpb.shape
    col = np.arange(GRID_W)
    col_start = np.clip(col - NA_KW // 2, 0, GRID_W - NA_KW)
    col_ok = (col[None, :] >= col_start[:, None]) & (col[None, :] < col_start[:, None] + NA_KW)
    dcol = np.clip(col[None, :] - col[:, None] + NA_KW - 1, 0, ncol - 1)
    full = jnp.where(jnp.asarray(col_ok[None, :, None, :]),
                     rpb.astype(F32)[:, :, jnp.asarray(dcol)].transpose(0, 2, 1, 3), -jnp.inf)
    nloc = NA_KH * GRID_W
    var = [full[:, :, o:o + NA_KH, :].reshape(heads, GRID_W, nloc) for o in range(NA_KH)]
    var.append(jnp.full((heads, GRID_W, nloc), -jnp.inf, F32))
    return jnp.stack(var, axis=1)


def _ffn_kernel(x_ref, xp_ref, xn_ref, mod_ref, g_ref, wu_ref, bu_ref, cw_ref, cb_ref, wd_ref, bd_ref,
                *rest, latent, final, merged):
    rest = list(rest)
    fg_ref = rest.pop(0) if final else None
    if merged:
        rest.pop(0)
    o_ref, h_buf, acc_ref, *u_bufs = rest
    rows = x_ref.shape[0]
    x = x_ref[...]
    ext = jnp.concatenate([xp_ref[...], x, xn_ref[...]], axis=0)
    h_buf[...] = _norm_mod(ext, g_ref[...], mod_ref[3:4, :], mod_ref[4:5, :]).astype(BF16)
    if latent:
        has_prev = pl.program_id(1) > 0
        has_next = pl.program_id(1) < pl.num_programs(1) - 1
    else:
        has_prev = has_next = False
    nchunk = wd_ref.shape[0]
    taps = cw_ref.shape[1]
    acc_ref[...] = jnp.zeros_like(acc_ref)

    def up(c, slot):
        for half in range(2):
            cc = half * nchunk + c
            raw = _dot(h_buf[...], wu_ref[cc])
            buf = u_bufs[2 * slot + half]
            buf[HALO:HALO + rows, :] = raw[HALO:HALO + rows, :]
            buf[0:HALO, :] = jnp.where(has_prev, raw[0:HALO, :], -bu_ref[cc])
            buf[HALO + rows:, :] = jnp.where(has_next, raw[HALO + rows:, :], -bu_ref[cc])

    def down(c, slot):
        halves = []
        for half in range(2):
            cc = half * nchunk + c
            y = cb_ref[cc] + bu_ref[cc] * jnp.sum(cw_ref[cc], axis=0, keepdims=True)
            u = u_bufs[2 * slot + half][...]
            for t in range(taps):
                shifted = u if t == taps // 2 else pltpu.roll(u, (taps // 2 - t) % u.shape[0], axis=0)
                y = y + cw_ref[cc, t:t + 1, :] * shifted[HALO:HALO + rows, :]
            halves.append(y)
        act = (halves[0] * _silu(halves[1])).astype(BF16)
        acc_ref[...] += _dot(act, wd_ref[c])

    def stage(c, slot):
        up(c + 1, 1 - slot)
        down(c, slot)

    unroll = FF_UNROLL
    assert unroll % 2 == 0
    up(0, 0)

    def body(i, carry):
        for k in range(unroll):
            stage(unroll * i + k, k % 2)
        return carry

    looped = (nchunk - 1) // unroll
    lax.fori_loop(0, looped, body, 0)
    for c in range(looped * unroll, nchunk - 1):
        stage(c, c % 2)
    down(nchunk - 1, (nchunk - 1) % 2)
    o = x + mod_ref[5:6, :] * (acc_ref[...] + bd_ref[...])
    if final:
        o = o * lax.rsqrt(jnp.mean(o * o, axis=-1, keepdims=True) + EPS) * fg_ref[...]
    o_ref[...] = o


def _ffn_call(r, mods, layer, ctx_len, weights, latent, final_g=None, merge_into=None):
    bsz, t, d = r.shape
    g, w_up, b_up, conv_w, conv_b, w_down, b_down = weights
    final = final_g is not None
    merged = merge_into is not None
    if latent:
        rows = FF_ROWS
        assert (t - ctx_len) % rows == 0
        grid = (bsz, (t - ctx_len) // rows)
        start = lambda j: ctx_len + rows * j
        tile = lambda i: pl.multiple_of(i, HALO)
        cur = pl.BlockSpec((None, pl.Element(rows), pl.Element(d)), lambda b, j: (b, tile(start(j)), 0))
        prev = pl.BlockSpec((None, pl.Element(HALO), pl.Element(d)),
                            lambda b, j: (b, tile(jnp.maximum(start(j) - HALO, ctx_len)), 0))
        nxt = pl.BlockSpec((None, pl.Element(HALO), pl.Element(d)),
                           lambda b, j: (b, tile(jnp.minimum(start(j + 1), t - HALO)), 0))
        mod = pl.BlockSpec((None, None, 6, d), lambda b, j: (layer, b, 0, 0))
    else:
        rows = ctx_len
        grid = (bsz, 1)
        cur = pl.BlockSpec((None, rows, d), lambda b, j: (b, 0, 0))
        prev = nxt = pl.BlockSpec((None, HALO, d), lambda b, j: (b, 0, 0))
        mod = pl.BlockSpec((None, None, 6, d), lambda b, j: (layer, bsz, 0, 0))
    in_specs = [cur, prev, nxt, mod, _const2((1, d)), _const2(w_up.shape), _const2(b_up.shape),
                _const2(conv_w.shape), _const2(conv_b.shape), _const2(w_down.shape), _const2((1, d))]
    args = [r, r, r, mods, g, w_up, b_up, conv_w, conv_b, w_down, b_down]
    if final:
        in_specs.append(_const2((1, d)))
        args.append(final_g)
    aliases = {}
    if merged:
        aliases = {len(args): 0}
        in_specs.append(pl.BlockSpec(memory_space=pl.ANY))
        args.append(merge_into)
    if final:
        assert latent
        out_spec = pl.BlockSpec((None, rows, d), lambda b, j: (b, j, 0))
        out_shape = jax.ShapeDtypeStruct((bsz, t - ctx_len, d), F32)
    else:
        out_spec = cur
        out_shape = jax.ShapeDtypeStruct((bsz, t, d), F32)
    kern = functools.partial(_ffn_kernel, latent=latent, final=final, merged=merged)
    return pl.pallas_call(
        kern,
        grid=grid,
        in_specs=in_specs,
        out_specs=out_spec,
        out_shape=out_shape,
        input_output_aliases=aliases,
        scratch_shapes=[pltpu.VMEM((rows + 2 * HALO, d), BF16), pltpu.VMEM((rows, d), F32)]
        + [pltpu.VMEM((rows + 2 * HALO, FF_CHUNK), F32)] * 4,
        compiler_params=_params(("arbitrary", "arbitrary")),
        name="ffn_latent" if latent else "ffn_context",
    )(*args)


def _ffn_weights(g, w_up, b_up, conv_w, conv_b, w_down, b_down):
    d, hidden2 = w_up.shape
    nc2 = hidden2 // FF_CHUNK
    taps = conv_w.shape[0]
    return (g, w_up.reshape(d, nc2, FF_CHUNK).transpose(1, 0, 2), b_up.reshape(nc2, 1, FF_CHUNK),
            conv_w.reshape(taps, nc2, FF_CHUNK).transpose(1, 0, 2), conv_b.reshape(nc2, 1, FF_CHUNK),
            w_down.reshape(nc2 // 2, FF_CHUNK, d), b_down)


def _block_diag_tiles(w):
    nb, bs, _ = w.shape
    per = BD_TILE // bs
    rep = jnp.tile(w.reshape(nb * bs, bs), (1, per)).reshape(nb // per, BD_TILE, BD_TILE)
    pos = np.arange(BD_TILE) // bs
    return jnp.where(jnp.asarray(pos[:, None] == pos[None, :]), rep, 0.0).astype(BF16)


def _gate_cols(gate_w, gate_b, inner):
    dirs, _, ng = gate_w.shape
    g = gate_w.transpose(1, 0, 2).reshape(3 * inner, dirs * ng)
    g = jnp.pad(g, ((0, 0), (0, LANE - dirs * ng))).astype(BF16)
    gb = jnp.pad(gate_b.reshape(1, dirs * ng), ((0, 0), (0, LANE - dirs * ng)))
    return g[:inner], g[inner:2 * inner], g[2 * inner:], gb


def kernel(x, c, ctx, c_ctx, ada_w, ada_b, norm1_g, norm2_g, final_g, ml_w_in, ml_conv_w, ml_conv_b, ml_wq, ml_wk, ml_wv, ml_gate_w, ml_gate_b, ml_skip, ml_hnorm_g, ml_w_out, na_w_qkv, na_b_qkv, na_rpb, na_w_out, na_b_out, ff_w_up, ff_b_up, ff_conv_w, ff_conv_b, ff_w_down, ff_b_down):
    bsz, seq, d = x.shape
    ctx_len = ctx.shape[1]
    depth = ada_w.shape[0]
    assert ctx_len == ROW_BLOCK and seq % ROW_BLOCK == 0 and seq % GRID_W == 0
    assert bsz + 1 <= MOD_ROWS
    t = ctx_len + seq
    nblk = t // ROW_BLOCK

    c_all = jnp.zeros((MOD_ROWS, d), F32).at[:bsz].set(c).at[bsz].set(c_ctx)
    mods = _ada(c_all, ada_w, ada_b).reshape(depth, MOD_ROWS, 6, d)

    r = (ctx, x)
    row2 = lambda a: a.reshape(1, -1)
    out = None
    for i in range(depth):
        need_ctx = i < depth - 1
        j = i // 2
        all_rows = _Rows(bsz, nblk, i, 0)
        live_rows = all_rows if need_ctx else _Rows(bsz, nblk, i, 1)
        if i % 2 == 0:
            inner = ml_w_in.shape[2] // 2
            dh = inner // ML_HEADS
            u, z = _ml_in(all_rows, r, mods, row2(norm1_g[i]), ml_w_in[j].astype(BF16))
            gq, gk, gv, gb = _gate_cols(ml_gate_w[j], ml_gate_b[j], inner)
            uc, q, k, v, gates = _ml_proj(u, ml_conv_w[j], row2(ml_conv_b[j]), _block_diag_tiles(ml_wq[j]),
                                          _block_diag_tiles(ml_wk[j]), _block_diag_tiles(ml_wv[j]),
                                          gq, gk, gv, gb, dh, dh ** -0.5,
                                          ml_gate_w.shape[1] * ml_gate_w.shape[3])
            hn = _ml_core(q, k, v, gates, dh)
            r = _ml_out(live_rows, hn, uc, z, r, mods, row2(ml_hnorm_g[j]), row2(ml_skip[j]),
                        ml_w_out[j].astype(BF16))
        else:
            q, k, v = _na_qkv(all_rows, r, mods, row2(norm1_g[i]), na_w_qkv[j].astype(BF16), row2(na_b_qkv[j]),
                              NA_HEAD_DIM ** -0.5)
            r = _na_attn(live_rows, q, k, v, _na_bias_table(na_rpb[j]), r, mods, na_w_out[j].astype(BF16),
                         row2(na_b_out[j]), ctx_len)
        weights = _ffn_weights(row2(norm2_g[i]), ff_w_up[i].astype(BF16), row2(ff_b_up[i]), ff_conv_w[i],
                               row2(ff_conv_b[i]), ff_w_down[i].astype(BF16), row2(ff_b_down[i]))
        if need_ctx:
            r_new = _ffn_call(r, mods, i, ctx_len, weights, latent=True)
            r = _ffn_call(r, mods, i, ctx_len, weights, latent=False, merge_into=r_new)
        else:
            out = _ffn_call(r, mods, i, ctx_len, weights, latent=True, final_g=row2(final_g))
    return out
```

```python
import functools

import jax
import jax.numpy as jnp
import numpy as np
from jax import lax
from jax.experimental import pallas as pl
from jax.experimental.pallas import tpu as pltpu

F32 = jnp.float32
BF16 = jnp.bfloat16
EPS = 1e-6

GRID_W = 64
ML_HEADS = 4
ML_QKV_BLOCK = 4
NA_HEAD_DIM = 64
NA_KH = 8
NA_KW = 16

ROW_BLOCK = 256
HALO = 8
BD_TILE = 256
LANE = 128
NA_GROUP = 256
NA_ROWS_PER_STEP = 4
ML_HEADS_PER_STEP = 2
FF_CHUNK = 256
FF_UNROLL = 4
FF_ROWS = 512
MOD_ROWS = 16
VMEM_LIMIT = 56 * 1024 * 1024


def _dot(a, b):
    return jnp.dot(a, b, preferred_element_type=F32)


def _dot_nt(a, b):
    return lax.dot_general(a, b, (((1,), (1,)), ((), ())), preferred_element_type=F32)


def _dot_tn(a, b):
    return lax.dot_general(a, b, (((0,), (0,)), ((), ())), preferred_element_type=F32)


def _silu(x):
    return x * jax.nn.sigmoid(x)


def _log_sigmoid(x):
    return jnp.minimum(x, 0.0) - jnp.log1p(jnp.exp(-jnp.abs(x)))


def _norm_mod(x, g, shift, scale):
    y = x * lax.rsqrt(jnp.mean(x * x, axis=-1, keepdims=True) + EPS) * g
    return y * (1.0 + scale) + shift


def _params(sem):
    return pltpu.CompilerParams(dimension_semantics=sem, vmem_limit_bytes=VMEM_LIMIT)


def _resident(shape, index_map):
    return pl.BlockSpec(shape, index_map, pipeline_mode=pl.Buffered(1))


def _ada_kernel(c_ref, w_ref, b_ref, o_ref):
    s = _silu(c_ref[...]).astype(BF16)
    o_ref[...] = _dot(s, w_ref[...].astype(BF16)) + b_ref[...]


def _ada(c_all, ada_w, ada_b):
    depth, d, d6 = ada_w.shape
    n = d6 // d
    return pl.pallas_call(
        _ada_kernel,
        grid=(depth, n),
        in_specs=[
            pl.BlockSpec((MOD_ROWS, d), lambda l, j: (0, 0)),
            pl.BlockSpec((None, d, d), lambda l, j: (l, 0, j)),
            pl.BlockSpec((None, 1, d), lambda l, j: (l, 0, j)),
        ],
        out_specs=pl.BlockSpec((None, MOD_ROWS, d), lambda l, j: (l, 0, j)),
        out_shape=jax.ShapeDtypeStruct((depth, MOD_ROWS, d6), F32),
        compiler_params=_params(("arbitrary", "arbitrary")),
        name="ada",
    )(c_all, ada_w, ada_b.reshape(depth, 1, d6))


class _Rows:
    def __init__(self, bsz, nblk_total, layer, first_blk):
        self.bsz = bsz
        self.nblk_total = nblk_total
        self.layer = layer
        self.first = first_blk
        self.grid = (bsz, nblk_total - first_blk)

    def rows(self, width, rows=ROW_BLOCK):
        f = self.first
        return pl.BlockSpec((None, rows, width), lambda b, j: (b, j + f, 0))

    def residual(self, r):
        if not isinstance(r, tuple):
            return [self.rows(r.shape[-1])], [r]
        assert self.first == 0
        d = r[1].shape[-1]
        return ([pl.BlockSpec((None, ROW_BLOCK, d), lambda b, j: (b, 0, 0)),
                 pl.BlockSpec((None, ROW_BLOCK, d), lambda b, j: (b, jnp.maximum(j - 1, 0), 0))], list(r))

    def mod(self, d):
        f, bsz, layer = self.first, self.bsz, self.layer
        return pl.BlockSpec((None, None, 6, d), lambda b, j: (layer, jnp.where(j + f == 0, bsz, b), 0, 0))


def _const2(shape):
    return _resident(shape, lambda b, j: (0,) * len(shape))


def _layer_const(stacked, layer):
    shape = stacked.shape[1:]
    return _resident((None,) + shape, lambda b, j: (layer,) + (0,) * len(shape))


def _residual_block(refs):
    if len(refs) == 1:
        return refs[0][...]
    return jnp.where(pl.program_id(1) == 0, refs[0][...], refs[1][...])


def _residual_shape(r):
    if isinstance(r, tuple):
        return r[1].shape[0], r[0].shape[1] + r[1].shape[1], r[1].shape[2]
    return r.shape


def _proj_kernel(*refs, nres, shift, scale, chunk, splits, has_bias, out_scale):
    res, (mod_ref, g_ref, w_ref), rest = refs[:nres], refs[nres:nres + 3], refs[nres + 3:]
    if has_bias:
        b_ref, out_refs = rest[0], rest[1:]
    else:
        b_ref, out_refs = None, rest
    h = _norm_mod(_residual_block(res), g_ref[...], mod_ref[shift:shift + 1, :],
                  mod_ref[scale:scale + 1, :]).astype(BF16)
    n = w_ref.shape[1]
    per_out = n // len(out_refs)
    for c in range(n // chunk):
        y = _dot(h, w_ref[:, c * chunk:(c + 1) * chunk])
        if has_bias:
            y = y + b_ref[:, c * chunk:(c + 1) * chunk]
        oi, off = divmod(c * chunk, per_out)
        if out_scale[oi] != 1.0:
            y = y * out_scale[oi]
        y = y.astype(BF16)
        o_ref = out_refs[oi]
        if splits:
            for t in range(chunk // splits):
                o_ref[(off + t * splits) // splits] = y[:, t * splits:(t + 1) * splits]
        else:
            o_ref[:, off:off + chunk] = y


def _ml_in(rows, r, mods, g, w_in):
    bsz, t, d = _residual_shape(r)
    n = w_in.shape[1]
    half = n // 2
    res_specs, res_args = rows.residual(r)
    kern = functools.partial(_proj_kernel, nres=len(res_args), shift=0, scale=1, chunk=512, splits=0, has_bias=False,
                             out_scale=(1.0, 1.0))
    return pl.pallas_call(
        kern,
        grid=rows.grid,
        in_specs=res_specs + [rows.mod(d), _const2((1, d)), _const2((d, n))],
        out_specs=[rows.rows(half), rows.rows(half)],
        out_shape=[jax.ShapeDtypeStruct((bsz, t, half), BF16)] * 2,
        compiler_params=_params(("arbitrary", "arbitrary")),
        name="ml_in",
    )(*res_args, mods, g, w_in)


def _na_qkv(rows, r, mods, g, w, b, q_scale):
    bsz, t, d = r.shape
    groups = d // NA_GROUP
    f = rows.first
    out_spec = pl.BlockSpec((None, groups, ROW_BLOCK, NA_GROUP), lambda bb, j: (bb, 0, j + f, 0))
    kern = functools.partial(_proj_kernel, nres=1, shift=0, scale=1, chunk=NA_GROUP, splits=NA_GROUP, has_bias=True,
                             out_scale=(q_scale, 1.0, 1.0))
    return pl.pallas_call(
        kern,
        grid=rows.grid,
        in_specs=[rows.rows(d), rows.mod(d), _const2((1, d)), _const2((d, 3 * d)), _const2((1, 3 * d))],
        out_specs=[out_spec] * 3,
        out_shape=[jax.ShapeDtypeStruct((bsz, groups, t, NA_GROUP), BF16)] * 3,
        compiler_params=_params(("arbitrary", "arbitrary")),
        name="na_qkv",
    )(r, mods, g, w, b)


def _ml_proj_kernel(u_ref, cw_ref, cb_ref, wq_ref, wk_ref, wv_ref, gq_ref, gk_ref, gv_ref, gb_ref,
                    uc_ref, q_ref, k_ref, v_ref, g_ref, *, nblk, kscale, ngates):
    @pl.when(pl.program_id(1) == 0)
    def _():
        g_ref[...] = jnp.broadcast_to(gb_ref[...], g_ref.shape)

    cw = cw_ref[...]
    cb = cb_ref[...]
    taps = cw.shape[0]
    width = u_ref.shape[1]
    pad = 2 * HALO
    for j in range(nblk):
        r0 = j * ROW_BLOCK
        cur_b = u_ref[r0:r0 + ROW_BLOCK, :]
        cur = cur_b.astype(F32)
        if j in (0, 1):
            prev = jnp.zeros((HALO, width), F32)
        else:
            prev = u_ref[r0 - pad:r0, :].astype(F32)[HALO:, :]
        if j in (0, nblk - 1):
            nxt = jnp.zeros((HALO, width), F32)
        else:
            nxt = u_ref[r0 + ROW_BLOCK:r0 + ROW_BLOCK + pad, :].astype(F32)[:HALO, :]
        ext = jnp.concatenate([prev, cur, nxt], axis=0)
        acc = cb
        for t in range(taps):
            shifted = ext if t == taps // 2 else pltpu.roll(ext, (taps // 2 - t) % ext.shape[0], axis=0)
            acc = acc + cw[t:t + 1, :] * shifted[HALO:HALO + ROW_BLOCK, :]
        uc_b = _silu(acc).astype(BF16)
        uc_ref[r0:r0 + ROW_BLOCK, :] = uc_b
        qs, ks, vs = [], [], []
        for t in range(width // BD_TILE):
            sl = slice(t * BD_TILE, (t + 1) * BD_TILE)
            qs.append(_dot(uc_b[:, sl], wq_ref[t]))
            ks.append(_dot(uc_b[:, sl], wk_ref[t]) * kscale)
            vs.append(_dot(cur_b[:, sl], wv_ref[t]))
        q_b = jnp.concatenate(qs, axis=1).astype(BF16)
        k_b = jnp.concatenate(ks, axis=1).astype(BF16)
        v_b = jnp.concatenate(vs, axis=1).astype(BF16)
        q_ref[r0:r0 + ROW_BLOCK, :] = q_b
        k_ref[r0:r0 + ROW_BLOCK, :] = k_b
        v_ref[r0:r0 + ROW_BLOCK, :] = v_b
        g_ref[r0:r0 + ROW_BLOCK, :] += _dot(q_b, gq_ref[...]) + _dot(k_b, gk_ref[...]) + _dot(v_b, gv_ref[...])

    @pl.when(pl.program_id(1) == pl.num_programs(1) - 1)
    def _():
        lane = lax.broadcasted_iota(jnp.int32, (ROW_BLOCK, g_ref.shape[1]), 1)
        forget = (lane < ngates) & (lane % (ngates // 2) >= ngates // 4)
        for j in range(nblk):
            pre = g_ref[j * ROW_BLOCK:(j + 1) * ROW_BLOCK, :]
            g_ref[j * ROW_BLOCK:(j + 1) * ROW_BLOCK, :] = jnp.where(forget, _log_sigmoid(pre), pre)


def _ml_proj(u, conv_w, conv_b, wq_bd, wk_bd, wv_bd, gq, gk, gv, gb, dh, kscale, ngates):
    bsz, t, inner = u.shape
    heads = inner // dh
    tiles = dh // BD_TILE
    taps = conv_w.shape[0]
    col = lambda shape: pl.BlockSpec(shape, lambda b, h: (0, h))
    seq = pl.BlockSpec((None, t, dh), lambda b, h: (b, 0, h))
    bd = pl.BlockSpec((tiles, BD_TILE, BD_TILE), lambda b, h: (h, 0, 0))
    gw = pl.BlockSpec((dh, LANE), lambda b, h: (h, 0))
    kern = functools.partial(_ml_proj_kernel, nblk=t // ROW_BLOCK, kscale=kscale, ngates=ngates)
    return pl.pallas_call(
        kern,
        grid=(bsz, heads),
        in_specs=[seq, col((taps, dh)), col((1, dh)), bd, bd, bd, gw, gw, gw,
                  pl.BlockSpec((1, LANE), lambda b, h: (0, 0))],
        out_specs=[seq, seq, seq, seq, pl.BlockSpec((None, t, LANE), lambda b, h: (b, 0, 0))],
        out_shape=[jax.ShapeDtypeStruct((bsz, t, inner), BF16)] * 4 + [jax.ShapeDtypeStruct((bsz, t, LANE), F32)],
        compiler_params=_params(("arbitrary", "arbitrary")),
        name="ml_proj",
    )(u, conv_w, conv_b, wq_bd, wk_bd, wv_bd, gq, gk, gv, gb)


def _ml_chunk(d, q, k, v, g, ct_ref, n_ref, m_ref, ci, cf):
    L = q.shape[0]
    gt = g.T
    lane = lax.broadcasted_iota(jnp.int32, g.shape, 1)
    sub = lax.broadcasted_iota(jnp.int32, gt.shape, 0)
    li_col = jnp.sum(jnp.where(lane == ci, g, 0.0), axis=1, keepdims=True)
    lf_col = jnp.sum(jnp.where(lane == cf, g, 0.0), axis=1, keepdims=True)
    li_row = jnp.sum(jnp.where(sub == ci, gt, 0.0), axis=0, keepdims=True)
    lf_row = jnp.sum(jnp.where(sub == cf, gt, 0.0), axis=0, keepdims=True)

    tt = lax.broadcasted_iota(jnp.int32, (L, L), 0)
    ss = lax.broadcasted_iota(jnp.int32, (L, L), 1)
    seen = (ss >= tt) if d else (ss <= tt)
    seen_t = (tt >= ss) if d else (tt <= ss)
    b_col = jnp.sum(jnp.where(seen, lf_row, 0.0), axis=1, keepdims=True)
    b_row = jnp.sum(jnp.where(seen_t, lf_col, 0.0), axis=0, keepdims=True)
    b_last = jnp.sum(lf_row, axis=1, keepdims=True)

    m_prev = m_ref[...]
    r_row = li_row - b_row
    m_t = b_col + jnp.maximum(m_prev, jnp.max(jnp.where(seen, r_row, -jnp.inf), axis=1, keepdims=True))
    sc = _dot_nt(q, k) * jnp.exp(jnp.where(seen, (b_col - m_t) + r_row, -jnp.inf))
    w_inter = jnp.exp(b_col + m_prev - m_t)
    n_rows = jnp.broadcast_to(n_ref[...], (2 * HALO, n_ref.shape[1])).astype(BF16)
    qn = _dot_nt(q, n_rows)[:, :1]
    num = _dot(sc.astype(BF16), v) + w_inter * _dot(q, ct_ref[...].astype(BF16))
    den = jnp.sum(sc, axis=1, keepdims=True) + w_inter * qn
    hh = num * (1.0 / jnp.maximum(jnp.abs(den), jnp.exp(-m_t)))

    g_row = b_last + r_row
    g_col = b_last - b_col + li_col
    m_new = jnp.maximum(b_last + m_prev, jnp.max(g_row, axis=1, keepdims=True))
    decay = jnp.exp(b_last + m_prev - m_new)
    vw = (v.astype(F32) * jnp.exp(g_col - m_new)).astype(BF16)
    wk_rows = jnp.broadcast_to(jnp.exp(g_row - m_new), (2 * HALO, L)).astype(BF16)
    ct_ref[...] = decay * ct_ref[...] + _dot_tn(k, vw)
    n_ref[...] = decay * n_ref[...] + _dot(wk_rows, k)[:1, :]
    m_ref[...] = m_new
    return hh


def _bwd_block(s, nblk):
    return jnp.where(s == 0, 0, nblk - s)


def _ml_core_kernel(qf_ref, kf_ref, vf_ref, gf_ref, qb_ref, kb_ref, vb_ref, gb_ref, o_ref, hacc_ref, *state,
                    heads, nblk, dh):
    s = pl.program_id(2)
    L = qf_ref.shape[0]
    per_step = len(state) // 6

    @pl.when(s == 0)
    def _():
        for ref in (hacc_ref,) + state:
            ref[...] = jnp.zeros_like(ref)

    rows_f = pl.ds(pl.multiple_of(s * L, L), L)
    rows_b = pl.ds(pl.multiple_of(_bwd_block(s, nblk) * L, L), L)
    gf, gb = gf_ref[...], gb_ref[...]
    for i in range(per_step):
        h = pl.program_id(1) * per_step + i
        cols = slice(i * dh, (i + 1) * dh)
        ctf, nf, mf, ctb, nb, mb = state[6 * i:6 * i + 6]
        hacc_ref[rows_f, cols] += _ml_chunk(0, qf_ref[:, cols], kf_ref[:, cols], vf_ref[:, cols], gf,
                                            ctf, nf, mf, h, h + heads)
        hacc_ref[rows_b, cols] += _ml_chunk(1, qb_ref[:, cols], kb_ref[:, cols], vb_ref[:, cols], gb,
                                            ctb, nb, mb, 2 * heads + h, 3 * heads + h)

    @pl.when(s == nblk - 1)
    def _():
        for j in range(nblk):
            for i in range(per_step):
                tot = hacc_ref[j * L:(j + 1) * L, i * dh:(i + 1) * dh]
                cen = tot - jnp.mean(tot, axis=-1, keepdims=True)
                var = jnp.mean(cen * cen, axis=-1, keepdims=True)
                o_ref[j * L:(j + 1) * L, i * dh:(i + 1) * dh] = (cen * lax.rsqrt(var + EPS)).astype(BF16)


def _ml_core(q, k, v, gates, dh):
    bsz, t, inner = q.shape
    heads = inner // dh
    nblk = t // ROW_BLOCK
    per_step = ML_HEADS_PER_STEP
    assert heads % per_step == 0
    width = per_step * dh
    fwd = pl.BlockSpec((None, ROW_BLOCK, width), lambda b, h, s: (b, s, h))
    bwd = pl.BlockSpec((None, ROW_BLOCK, width), lambda b, h, s: (b, _bwd_block(s, nblk), h))
    gate_f = pl.BlockSpec((None, ROW_BLOCK, LANE), lambda b, h, s: (b, s, 0))
    gate_b = pl.BlockSpec((None, ROW_BLOCK, LANE), lambda b, h, s: (b, _bwd_block(s, nblk), 0))
    state = [pltpu.VMEM((dh, dh), F32), pltpu.VMEM((1, dh), F32), pltpu.VMEM((1, 1), F32)]
    kern = functools.partial(_ml_core_kernel, heads=heads, nblk=nblk, dh=dh)
    return pl.pallas_call(
        kern,
        grid=(bsz, heads // per_step, nblk),
        in_specs=[fwd, fwd, fwd, gate_f, bwd, bwd, bwd, gate_b],
        out_specs=pl.BlockSpec((None, t, width), lambda b, h, s: (b, 0, h)),
        out_shape=jax.ShapeDtypeStruct((bsz, t, inner), BF16),
        scratch_shapes=[pltpu.VMEM((t, width), F32)] + state * (2 * per_step),
        compiler_params=_params(("arbitrary",) * 3),
        name="ml_core",
    )(q, k, v, gates, q, k, v, gates)


def _ml_out_kernel(hn_ref, uc_ref, z_ref, *refs, nres):
    res, (mod_ref, hg_ref, sk_ref, w_ref, o_ref) = refs[:nres], refs[nres:]
    z = z_ref[...].astype(F32)
    y = (hn_ref[...].astype(F32) * hg_ref[...] + sk_ref[...] * uc_ref[...].astype(F32)) * _silu(z)
    o_ref[...] = _residual_block(res) + mod_ref[2:3, :] * _dot(y.astype(BF16), w_ref[...])


def _ml_out(rows, hn, uc, z, r, mods, hnorm_g, skip, w_out):
    bsz, t, d = _residual_shape(r)
    inner = hn.shape[-1]
    res_specs, res_args = rows.residual(r)
    return pl.pallas_call(
        functools.partial(_ml_out_kernel, nres=len(res_args)),
        grid=rows.grid,
        in_specs=[rows.rows(inner), rows.rows(inner), rows.rows(inner)] + res_specs
        + [rows.mod(d), _const2((1, inner)), _const2((1, inner)), _const2((inner, d))],
        out_specs=rows.rows(d),
        out_shape=jax.ShapeDtypeStruct((bsz, t, d), F32),
        compiler_params=_params(("arbitrary", "arbitrary")),
        name="ml_out",
    )(hn, uc, z, *res_args, mods, hnorm_g, skip, w_out)


def _na_attn_kernel(q_ref, k_ref, v_ref, *rest, first, ctx_len, rows, groups):
    b_refs = rest[:NA_ROWS_PER_STEP]
    r_ref, mod_ref, w_ref, bo_ref, o_ref, a_buf = rest[NA_ROWS_PER_STEP:]
    gw = GRID_W
    per = NA_GROUP // NA_HEAD_DIM
    nloc = NA_KH * gw
    lane = lax.broadcasted_iota(jnp.int32, (gw, NA_GROUP), 1)
    mine = [(lane >= h * NA_HEAD_DIM) & (lane < (h + 1) * NA_HEAD_DIM) for h in range(per)]

    def key_start(sub):
        t = (pl.program_id(1) + first) * NA_ROWS_PER_STEP + sub
        r = t - ctx_len // gw
        rs = jnp.clip(r - NA_KH // 2, 0, rows - NA_KH)
        return pl.multiple_of(ctx_len + rs * gw, gw)

    kstart = [key_start(sub) for sub in range(NA_ROWS_PER_STEP)]

    def scores(sub, g):
        qg = q_ref[g, sub * gw:(sub + 1) * gw, :].astype(F32)
        qm = jnp.concatenate([jnp.where(mine[h], qg, 0.0) for h in range(per)], axis=0).astype(BF16)
        return _dot_nt(qm, k_ref[g, pl.ds(kstart[sub], nloc), :]), _dot_nt(qm, k_ref[g, 0:ctx_len, :])

    work = [(sub, g) for sub in range(NA_ROWS_PER_STEP) for g in range(groups)]
    ahead = scores(*work[0])
    for i, (sub, g) in enumerate(work):
        s_loc, s_ctx = ahead
        if i + 1 < len(work):
            ahead = scores(*work[i + 1])
        s_loc = s_loc + b_refs[sub][per * g:per * (g + 1)].reshape(per * gw, nloc)
        m = jnp.maximum(jnp.max(s_loc, axis=1, keepdims=True), jnp.max(s_ctx, axis=1, keepdims=True))
        p_loc = jnp.exp(s_loc - m)
        p_ctx = jnp.exp(s_ctx - m)
        den = jnp.sum(p_loc, axis=1, keepdims=True) + jnp.sum(p_ctx, axis=1, keepdims=True)
        o = (_dot(p_loc.astype(BF16), v_ref[g, pl.ds(kstart[sub], nloc), :])
             + _dot(p_ctx.astype(BF16), v_ref[g, 0:ctx_len, :])) * (1.0 / den)
        og = jnp.where(mine[0], o[0:gw], 0.0)
        for h in range(1, per):
            og = jnp.where(mine[h], o[h * gw:(h + 1) * gw], og)
        a_buf[sub * gw:(sub + 1) * gw, g * NA_GROUP:(g + 1) * NA_GROUP] = og.astype(BF16)

    o_ref[...] = r_ref[...] + mod_ref[2:3, :] * (_dot(a_buf[...], w_ref[...]) + bo_ref[...])


def _na_attn(live_rows, q, k, v, bias, r, mods, w_out, b_out, ctx_len):
    bsz, groups, t, _ = q.shape
    d = groups * NA_GROUP
    rows = (t - ctx_len) // GRID_W
    blk = NA_ROWS_PER_STEP * GRID_W
    assert blk == ROW_BLOCK and ctx_len % blk == 0 and t % blk == 0
    first = live_rows.first
    steps = t // blk - first
    heads, nvar = bias.shape[:2]

    def variant(tq):
        r = tq - ctx_len // GRID_W
        rs = jnp.clip(r - NA_KH // 2, 0, rows - NA_KH)
        return jnp.where(r < 0, nvar - 1, rs - r + NA_KH - 1)

    def bias_spec(sub):
        return pl.BlockSpec((heads, None) + bias.shape[2:],
                            lambda b, j: (0, variant((j + first) * NA_ROWS_PER_STEP + sub), 0, 0))

    whole = pl.BlockSpec((None, groups, t, NA_GROUP), lambda b, j: (b, 0, 0, 0))
    kern = functools.partial(_na_attn_kernel, first=first, ctx_len=ctx_len, rows=rows, groups=groups)
    return pl.pallas_call(
        kern,
        grid=(bsz, steps),
        in_specs=[pl.BlockSpec((None, groups, blk, NA_GROUP), lambda b, j: (b, 0, j + first, 0)), whole, whole]
        + [bias_spec(sub) for sub in range(NA_ROWS_PER_STEP)]
        + [live_rows.rows(d), live_rows.mod(d), _const2((d, d)), _const2((1, d))],
        out_specs=live_rows.rows(d),
        out_shape=jax.ShapeDtypeStruct((bsz, t, d), F32),
        scratch_shapes=[pltpu.VMEM((blk, d), BF16)],
        compiler_params=_params(("arbitrary", "arbitrary")),
        name="na_attn",
    )(q, k, v, *([bias] * NA_ROWS_PER_STEP), r, mods, w_out, b_out)


def _na_bias_table(rpb):
    heads, nrow, ncol = rpb.shape
    col = np.arange(GRID_W)
    col_start = np.clip(col - NA_KW // 2, 0, GRID_W - NA_KW)
    col_ok = (col[None, :] >= col_start[:, None]) & (col[None, :] < col_start[:, None] + NA_KW)
    w = GRID_W
    left = w - NA_KW
    padded = jnp.pad(rpb.astype(F32), ((0, 0), (0, 0), (left, 2 * w - 1 - left - ncol)))
    toep = jnp.tile(padded, (1, 1, w))[:, :, w - 1:w - 1 + w * (2 * w - 2)]
    toep = toep.reshape(heads, nrow, w, 2 * w - 2)[:, :, :, :w]
    full = jnp.where(jnp.asarray(col_ok[None, :, None, :]), toep.transpose(0, 2, 1, 3), -jnp.inf)
    nloc = NA_KH * GRID_W
    var = [full[:, :, o:o + NA_KH, :].reshape(heads, GRID_W, nloc) for o in range(NA_KH)]
    var.append(jnp.full((heads, GRID_W, nloc), -jnp.inf, F32))
    return jnp.stack(var, axis=1)


def _ffn_kernel(x_ref, xp_ref, xn_ref, mod_ref, g_ref, wu_ref, bu_ref, cw_ref, cb_ref, wd_ref, bd_ref,
                *rest, latent, final, merged):
    rest = list(rest)
    fg_ref = rest.pop(0) if final else None
    if merged:
        rest.pop(0)
    o_ref, h_buf, acc_ref, *u_bufs = rest
    rows = x_ref.shape[0]
    x = x_ref[...]
    ext = jnp.concatenate([xp_ref[...], x, xn_ref[...]], axis=0)
    h_buf[...] = _norm_mod(ext, g_ref[...], mod_ref[3:4, :], mod_ref[4:5, :]).astype(BF16)
    if latent:
        has_prev = pl.program_id(1) > 0
        has_next = pl.program_id(1) < pl.num_programs(1) - 1
    else:
        has_prev = has_next = False
    nchunk = wd_ref.shape[0]
    taps = cw_ref.shape[1]
    acc_ref[...] = jnp.zeros_like(acc_ref)

    def up(c, slot):
        for half in range(2):
            cc = half * nchunk + c
            raw = _dot(h_buf[...], wu_ref[cc])
            buf = u_bufs[2 * slot + half]
            buf[HALO:HALO + rows, :] = raw[HALO:HALO + rows, :]
            buf[0:HALO, :] = jnp.where(has_prev, raw[0:HALO, :], -bu_ref[cc])
            buf[HALO + rows:, :] = jnp.where(has_next, raw[HALO + rows:, :], -bu_ref[cc])

    def down(c, slot):
        halves = []
        for half in range(2):
            cc = half * nchunk + c
            y = cb_ref[cc] + bu_ref[cc] * jnp.sum(cw_ref[cc], axis=0, keepdims=True)
            u = u_bufs[2 * slot + half][...]
            for t in range(taps):
                shifted = u if t == taps // 2 else pltpu.roll(u, (taps // 2 - t) % u.shape[0], axis=0)
                y = y + cw_ref[cc, t:t + 1, :] * shifted[HALO:HALO + rows, :]
            halves.append(y)
        act = (halves[0] * _silu(halves[1])).astype(BF16)
        acc_ref[...] += _dot(act, wd_ref[c])

    def stage(c, slot):
        up(c + 1, 1 - slot)
        down(c, slot)

    unroll = FF_UNROLL
    assert unroll % 2 == 0
    up(0, 0)

    def body(i, carry):
        for k in range(unroll):
            stage(unroll * i + k, k % 2)
        return carry

    looped = (nchunk - 1) // unroll
    lax.fori_loop(0, looped, body, 0)
    for c in range(looped * unroll, nchunk - 1):
        stage(c, c % 2)
    down(nchunk - 1, (nchunk - 1) % 2)
    o = x + mod_ref[5:6, :] * (acc_ref[...] + bd_ref[...])
    if final:
        o = o * lax.rsqrt(jnp.mean(o * o, axis=-1, keepdims=True) + EPS) * fg_ref[...]
    o_ref[...] = o


def _ffn_call(r, mods, layer, ctx_len, weights, latent, final_g=None, merge_into=None):
    bsz, t, d = r.shape
    final = final_g is not None
    merged = merge_into is not None
    if latent:
        rows = FF_ROWS
        assert (t - ctx_len) % rows == 0
        grid = (bsz, (t - ctx_len) // rows)
        start = lambda j: ctx_len + rows * j
        tile = lambda i: pl.multiple_of(i, HALO)
        cur = pl.BlockSpec((None, pl.Element(rows), pl.Element(d)), lambda b, j: (b, tile(start(j)), 0))
        prev = pl.BlockSpec((None, pl.Element(HALO), pl.Element(d)),
                            lambda b, j: (b, tile(jnp.maximum(start(j) - HALO, ctx_len)), 0))
        nxt = pl.BlockSpec((None, pl.Element(HALO), pl.Element(d)),
                           lambda b, j: (b, tile(jnp.minimum(start(j + 1), t - HALO)), 0))
        mod = pl.BlockSpec((None, None, 6, d), lambda b, j: (layer, b, 0, 0))
    else:
        rows = ctx_len
        grid = (bsz, 1)
        cur = pl.BlockSpec((None, rows, d), lambda b, j: (b, 0, 0))
        prev = nxt = pl.BlockSpec((None, HALO, d), lambda b, j: (b, 0, 0))
        mod = pl.BlockSpec((None, None, 6, d), lambda b, j: (layer, bsz, 0, 0))
    in_specs = [cur, prev, nxt, mod] + [_layer_const(w, layer) for w in weights]
    args = [r, r, r, mods, *weights]
    if final:
        in_specs.append(_const2((1, d)))
        args.append(final_g)
    aliases = {}
    if merged:
        aliases = {len(args): 0}
        in_specs.append(pl.BlockSpec(memory_space=pl.ANY))
        args.append(merge_into)
    if final:
        assert latent
        out_spec = pl.BlockSpec((None, rows, d), lambda b, j: (b, j, 0))
        out_shape = jax.ShapeDtypeStruct((bsz, t - ctx_len, d), F32)
    else:
        out_spec = cur
        out_shape = jax.ShapeDtypeStruct((bsz, t, d), F32)
    kern = functools.partial(_ffn_kernel, latent=latent, final=final, merged=merged)
    return pl.pallas_call(
        kern,
        grid=grid,
        in_specs=in_specs,
        out_specs=out_spec,
        out_shape=out_shape,
        input_output_aliases=aliases,
        scratch_shapes=[pltpu.VMEM((rows + 2 * HALO, d), BF16), pltpu.VMEM((rows, d), F32)]
        + [pltpu.VMEM((rows + 2 * HALO, FF_CHUNK), F32)] * 4,
        compiler_params=_params(("arbitrary", "arbitrary")),
        name="ffn_latent" if latent else "ffn_context",
    )(*args)


def _ffn_weights(g, w_up, b_up, conv_w, conv_b, w_down, b_down):
    layers, d, hidden2 = w_up.shape
    nc2 = hidden2 // FF_CHUNK
    taps = conv_w.shape[1]
    return (g.reshape(layers, 1, d),
            w_up.astype(BF16).reshape(layers, d, nc2, FF_CHUNK).transpose(0, 2, 1, 3),
            b_up.reshape(layers, nc2, 1, FF_CHUNK),
            conv_w.reshape(layers, taps, nc2, FF_CHUNK).transpose(0, 2, 1, 3),
            conv_b.reshape(layers, nc2, 1, FF_CHUNK),
            w_down.astype(BF16).reshape(layers, nc2 // 2, FF_CHUNK, d),
            b_down.reshape(layers, 1, d))


def _block_diag_tiles(w):
    nb, bs, _ = w.shape
    per = BD_TILE // bs
    rep = jnp.tile(w.reshape(nb * bs, bs), (1, per)).reshape(nb // per, BD_TILE, BD_TILE)
    pos = np.arange(BD_TILE) // bs
    return jnp.where(jnp.asarray(pos[:, None] == pos[None, :]), rep, 0.0).astype(BF16)


def _gate_cols(gate_w, gate_b, inner):
    dirs, _, ng = gate_w.shape
    g = gate_w.transpose(1, 0, 2).reshape(3 * inner, dirs * ng)
    g = jnp.pad(g, ((0, 0), (0, LANE - dirs * ng))).astype(BF16)
    gb = jnp.pad(gate_b.reshape(1, dirs * ng), ((0, 0), (0, LANE - dirs * ng)))
    return g[:inner], g[inner:2 * inner], g[2 * inner:], gb


def kernel(x, c, ctx, c_ctx, ada_w, ada_b, norm1_g, norm2_g, final_g, ml_w_in, ml_conv_w, ml_conv_b, ml_wq, ml_wk, ml_wv, ml_gate_w, ml_gate_b, ml_skip, ml_hnorm_g, ml_w_out, na_w_qkv, na_b_qkv, na_rpb, na_w_out, na_b_out, ff_w_up, ff_b_up, ff_conv_w, ff_conv_b, ff_w_down, ff_b_down):
    bsz, seq, d = x.shape
    ctx_len = ctx.shape[1]
    depth = ada_w.shape[0]
    assert ctx_len == ROW_BLOCK and seq % ROW_BLOCK == 0 and seq % GRID_W == 0
    assert bsz + 1 <= MOD_ROWS
    t = ctx_len + seq
    nblk = t // ROW_BLOCK

    c_all = jnp.zeros((MOD_ROWS, d), F32).at[:bsz].set(c).at[bsz].set(c_ctx)
    mods = _ada(c_all, ada_w, ada_b).reshape(depth, MOD_ROWS, 6, d)

    r = (ctx, x)
    row2 = lambda a: a.reshape(1, -1)
    ffn_weights = _ffn_weights(norm2_g, ff_w_up, ff_b_up, ff_conv_w, ff_conv_b, ff_w_down, ff_b_down)
    out = None
    for i in range(depth):
        need_ctx = i < depth - 1
        j = i // 2
        all_rows = _Rows(bsz, nblk, i, 0)
        live_rows = all_rows if need_ctx else _Rows(bsz, nblk, i, 1)
        if i % 2 == 0:
            inner = ml_w_in.shape[2] // 2
            dh = inner // ML_HEADS
            u, z = _ml_in(all_rows, r, mods, row2(norm1_g[i]), ml_w_in[j].astype(BF16))
            gq, gk, gv, gb = _gate_cols(ml_gate_w[j], ml_gate_b[j], inner)
            uc, q, k, v, gates = _ml_proj(u, ml_conv_w[j], row2(ml_conv_b[j]), _block_diag_tiles(ml_wq[j]),
                                          _block_diag_tiles(ml_wk[j]), _block_diag_tiles(ml_wv[j]),
                                          gq, gk, gv, gb, dh, dh ** -0.5,
                                          ml_gate_w.shape[1] * ml_gate_w.shape[3])
            hn = _ml_core(q, k, v, gates, dh)
            r = _ml_out(live_rows, hn, uc, z, r, mods, row2(ml_hnorm_g[j]), row2(ml_skip[j]),
                        ml_w_out[j].astype(BF16))
        else:
            q, k, v = _na_qkv(all_rows, r, mods, row2(norm1_g[i]), na_w_qkv[j].astype(BF16), row2(na_b_qkv[j]),
                              NA_HEAD_DIM ** -0.5)
            r = _na_attn(live_rows, q, k, v, _na_bias_table(na_rpb[j]), r, mods, na_w_out[j].astype(BF16),
                         row2(na_b_out[j]), ctx_len)
        if need_ctx:
            r_new = _ffn_call(r, mods, i, ctx_len, ffn_weights, latent=True)
            r = _ffn_call(r, mods, i, ctx_len, ffn_weights, latent=False, merge_into=r_new)
        else:
            out = _ffn_call(r, mods, i, ctx_len, ffn_weights, latent=True, final_g=row2(final_g))
    return out
```

```python
import functools

import jax
import jax.numpy as jnp
import numpy as np
from jax import lax
from jax.experimental import pallas as pl
from jax.experimental.pallas import tpu as pltpu

F32 = jnp.float32
BF16 = jnp.bfloat16
EPS = 1e-6
LOG2E = 1.4426950408889634

GRID_W = 64
ML_HEADS = 4
ML_QKV_BLOCK = 4
NA_HEAD_DIM = 64
NA_KH = 8
NA_KW = 16

ROW_BLOCK = 256
HALO = 8
BD_TILE = 256
LANE = 128
NA_GROUP = 256
NA_ROWS_PER_STEP = 4
ML_HEADS_PER_STEP = 2
FF_CHUNK = 256
FF_UNROLL = 4
FF_ROWS = 512
MOD_ROWS = 16
VMEM_LIMIT = 56 * 1024 * 1024


def _dot(a, b):
    return jnp.dot(a, b, preferred_element_type=F32)


def _dot_nt(a, b):
    return lax.dot_general(a, b, (((1,), (1,)), ((), ())), preferred_element_type=F32)


def _dot_tn(a, b):
    return lax.dot_general(a, b, (((0,), (0,)), ((), ())), preferred_element_type=F32)


def _silu(x):
    h = 0.5 * x
    return h + h * jnp.tanh(h)


def _log_sigmoid(x):
    return jnp.minimum(x, 0.0) - jnp.log1p(jnp.exp(-jnp.abs(x)))


def _norm_mod(x, g, shift, scale):
    y = x * lax.rsqrt(jnp.mean(x * x, axis=-1, keepdims=True) + EPS) * g
    return y * (1.0 + scale) + shift


def _params(sem):
    return pltpu.CompilerParams(dimension_semantics=sem, vmem_limit_bytes=VMEM_LIMIT)


def _resident(shape, index_map):
    return pl.BlockSpec(shape, index_map, pipeline_mode=pl.Buffered(1))


def _ada_kernel(c_ref, w_ref, b_ref, o_ref):
    s = _silu(c_ref[...]).astype(BF16)
    o_ref[...] = _dot(s, w_ref[...].astype(BF16)) + b_ref[...]


def _ada(c_all, ada_w, ada_b):
    depth, d, d6 = ada_w.shape
    n = d6 // d
    return pl.pallas_call(
        _ada_kernel,
        grid=(depth, n),
        in_specs=[
            pl.BlockSpec((MOD_ROWS, d), lambda l, j: (0, 0)),
            pl.BlockSpec((None, d, d), lambda l, j: (l, 0, j)),
            pl.BlockSpec((None, 1, d), lambda l, j: (l, 0, j)),
        ],
        out_specs=pl.BlockSpec((None, MOD_ROWS, d), lambda l, j: (l, 0, j)),
        out_shape=jax.ShapeDtypeStruct((depth, MOD_ROWS, d6), F32),
        compiler_params=_params(("arbitrary", "arbitrary")),
        name="ada",
    )(c_all, ada_w, ada_b.reshape(depth, 1, d6))


class _Rows:
    def __init__(self, bsz, nblk_total, layer, first_blk):
        self.bsz = bsz
        self.nblk_total = nblk_total
        self.layer = layer
        self.first = first_blk
        self.grid = (bsz, nblk_total - first_blk)

    def rows(self, width, rows=ROW_BLOCK):
        f = self.first
        return pl.BlockSpec((None, rows, width), lambda b, j: (b, j + f, 0))

    def residual(self, r):
        if not isinstance(r, tuple):
            return [self.rows(r.shape[-1])], [r]
        assert self.first == 0
        d = r[1].shape[-1]
        return ([pl.BlockSpec((None, ROW_BLOCK, d), lambda b, j: (b, 0, 0)),
                 pl.BlockSpec((None, ROW_BLOCK, d), lambda b, j: (b, jnp.maximum(j - 1, 0), 0))], list(r))

    def mod(self, d):
        f, bsz, layer = self.first, self.bsz, self.layer
        return pl.BlockSpec((None, None, 6, d), lambda b, j: (layer, jnp.where(j + f == 0, bsz, b), 0, 0))


def _const2(shape):
    return _resident(shape, lambda b, j: (0,) * len(shape))


def _layer_const(stacked, layer):
    shape = stacked.shape[1:]
    return _resident((None,) + shape, lambda b, j: (layer,) + (0,) * len(shape))


def _residual_block(refs):
    if len(refs) == 1:
        return refs[0][...]
    return jnp.where(pl.program_id(1) == 0, refs[0][...], refs[1][...])


def _residual_shape(r):
    if isinstance(r, tuple):
        return r[1].shape[0], r[0].shape[1] + r[1].shape[1], r[1].shape[2]
    return r.shape


def _proj_kernel(*refs, nres, shift, scale, chunk, splits, has_bias, out_scale):
    res, (mod_ref, g_ref, w_ref), rest = refs[:nres], refs[nres:nres + 3], refs[nres + 3:]
    if has_bias:
        b_ref, out_refs = rest[0], rest[1:]
    else:
        b_ref, out_refs = None, rest
    h = _norm_mod(_residual_block(res), g_ref[...], mod_ref[shift:shift + 1, :],
                  mod_ref[scale:scale + 1, :]).astype(BF16)
    n = w_ref.shape[1]
    per_out = n // len(out_refs)
    for c in range(n // chunk):
        y = _dot(h, w_ref[:, c * chunk:(c + 1) * chunk])
        if has_bias:
            y = y + b_ref[:, c * chunk:(c + 1) * chunk]
        oi, off = divmod(c * chunk, per_out)
        if out_scale[oi] != 1.0:
            y = y * out_scale[oi]
        y = y.astype(BF16)
        o_ref = out_refs[oi]
        if splits:
            for t in range(chunk // splits):
                o_ref[(off + t * splits) // splits] = y[:, t * splits:(t + 1) * splits]
        else:
            o_ref[:, off:off + chunk] = y


def _ml_in(rows, r, mods, g, w_in):
    bsz, t, d = _residual_shape(r)
    n = w_in.shape[1]
    half = n // 2
    res_specs, res_args = rows.residual(r)
    kern = functools.partial(_proj_kernel, nres=len(res_args), shift=0, scale=1, chunk=512, splits=0, has_bias=False,
                             out_scale=(1.0, 1.0))
    return pl.pallas_call(
        kern,
        grid=rows.grid,
        in_specs=res_specs + [rows.mod(d), _const2((1, d)), _const2((d, n))],
        out_specs=[rows.rows(half), rows.rows(half)],
        out_shape=[jax.ShapeDtypeStruct((bsz, t, half), BF16)] * 2,
        compiler_params=_params(("arbitrary", "arbitrary")),
        name="ml_in",
    )(*res_args, mods, g, w_in)


def _na_qkv(rows, r, mods, g, w, b, q_scale):
    bsz, t, d = r.shape
    groups = d // NA_GROUP
    f = rows.first
    out_spec = pl.BlockSpec((None, groups, ROW_BLOCK, NA_GROUP), lambda bb, j: (bb, 0, j + f, 0))
    kern = functools.partial(_proj_kernel, nres=1, shift=0, scale=1, chunk=NA_GROUP, splits=NA_GROUP, has_bias=True,
                             out_scale=(q_scale, 1.0, 1.0))
    return pl.pallas_call(
        kern,
        grid=rows.grid,
        in_specs=[rows.rows(d), rows.mod(d), _const2((1, d)), _const2((d, 3 * d)), _const2((1, 3 * d))],
        out_specs=[out_spec] * 3,
        out_shape=[jax.ShapeDtypeStruct((bsz, groups, t, NA_GROUP), BF16)] * 3,
        compiler_params=_params(("arbitrary", "arbitrary")),
        name="na_qkv",
    )(r, mods, g, w, b)


def _ml_proj_kernel(u_ref, cw_ref, cb_ref, wq_ref, wk_ref, wv_ref, gq_ref, gk_ref, gv_ref, gb_ref,
                    uc_ref, q_ref, k_ref, v_ref, g_ref, *, nblk, kscale, ngates):
    @pl.when(pl.program_id(1) == 0)
    def _():
        g_ref[...] = jnp.broadcast_to(gb_ref[...], g_ref.shape)

    cw = cw_ref[...]
    cb = cb_ref[...]
    taps = cw.shape[0]
    width = u_ref.shape[1]
    pad = 2 * HALO
    for j in range(nblk):
        r0 = j * ROW_BLOCK
        cur_b = u_ref[r0:r0 + ROW_BLOCK, :]
        cur = cur_b.astype(F32)
        if j in (0, 1):
            prev = jnp.zeros((HALO, width), F32)
        else:
            prev = u_ref[r0 - pad:r0, :].astype(F32)[HALO:, :]
        if j in (0, nblk - 1):
            nxt = jnp.zeros((HALO, width), F32)
        else:
            nxt = u_ref[r0 + ROW_BLOCK:r0 + ROW_BLOCK + pad, :].astype(F32)[:HALO, :]
        ext = jnp.concatenate([prev, cur, nxt], axis=0)
        acc = cb
        for t in range(taps):
            shifted = ext if t == taps // 2 else pltpu.roll(ext, (taps // 2 - t) % ext.shape[0], axis=0)
            acc = acc + cw[t:t + 1, :] * shifted[HALO:HALO + ROW_BLOCK, :]
        uc_b = _silu(acc).astype(BF16)
        uc_ref[r0:r0 + ROW_BLOCK, :] = uc_b
        qs, ks, vs = [], [], []
        for t in range(width // BD_TILE):
            sl = slice(t * BD_TILE, (t + 1) * BD_TILE)
            qs.append(_dot(uc_b[:, sl], wq_ref[t]))
            ks.append(_dot(uc_b[:, sl], wk_ref[t]) * kscale)
            vs.append(_dot(cur_b[:, sl], wv_ref[t]))
        q_b = jnp.concatenate(qs, axis=1).astype(BF16)
        k_b = jnp.concatenate(ks, axis=1).astype(BF16)
        v_b = jnp.concatenate(vs, axis=1).astype(BF16)
        q_ref[r0:r0 + ROW_BLOCK, :] = q_b
        k_ref[r0:r0 + ROW_BLOCK, :] = k_b
        v_ref[r0:r0 + ROW_BLOCK, :] = v_b
        g_ref[r0:r0 + ROW_BLOCK, :] += _dot(q_b, gq_ref[...]) + _dot(k_b, gk_ref[...]) + _dot(v_b, gv_ref[...])

    @pl.when(pl.program_id(1) == pl.num_programs(1) - 1)
    def _():
        lane = lax.broadcasted_iota(jnp.int32, (ROW_BLOCK, g_ref.shape[1]), 1)
        forget = (lane < ngates) & (lane % (ngates // 2) >= ngates // 4)
        for j in range(nblk):
            pre = g_ref[j * ROW_BLOCK:(j + 1) * ROW_BLOCK, :]
            g_ref[j * ROW_BLOCK:(j + 1) * ROW_BLOCK, :] = jnp.where(forget, _log_sigmoid(pre), pre) * LOG2E


def _ml_proj(u, conv_w, conv_b, wq_bd, wk_bd, wv_bd, gq, gk, gv, gb, dh, kscale, ngates):
    bsz, t, inner = u.shape
    heads = inner // dh
    tiles = dh // BD_TILE
    taps = conv_w.shape[0]
    col = lambda shape: pl.BlockSpec(shape, lambda b, h: (0, h))
    seq = pl.BlockSpec((None, t, dh), lambda b, h: (b, 0, h))
    bd = pl.BlockSpec((tiles, BD_TILE, BD_TILE), lambda b, h: (h, 0, 0))
    gw = pl.BlockSpec((dh, LANE), lambda b, h: (h, 0))
    kern = functools.partial(_ml_proj_kernel, nblk=t // ROW_BLOCK, kscale=kscale, ngates=ngates)
    return pl.pallas_call(
        kern,
        grid=(bsz, heads),
        in_specs=[seq, col((taps, dh)), col((1, dh)), bd, bd, bd, gw, gw, gw,
                  pl.BlockSpec((1, LANE), lambda b, h: (0, 0))],
        out_specs=[seq, seq, seq, seq, pl.BlockSpec((None, t, LANE), lambda b, h: (b, 0, 0))],
        out_shape=[jax.ShapeDtypeStruct((bsz, t, inner), BF16)] * 4 + [jax.ShapeDtypeStruct((bsz, t, LANE), F32)],
        compiler_params=_params(("arbitrary", "arbitrary")),
        name="ml_proj",
    )(u, conv_w, conv_b, wq_bd, wk_bd, wv_bd, gq, gk, gv, gb)


def _ml_chunk(d, q, k, v, g, ct_ref, n_ref, m_ref, ci, cf):
    L = q.shape[0]
    gt = g.T
    lane = lax.broadcasted_iota(jnp.int32, g.shape, 1)
    sub = lax.broadcasted_iota(jnp.int32, gt.shape, 0)
    li_col = jnp.sum(jnp.where(lane == ci, g, 0.0), axis=1, keepdims=True)
    lf_col = jnp.sum(jnp.where(lane == cf, g, 0.0), axis=1, keepdims=True)
    li_row = jnp.sum(jnp.where(sub == ci, gt, 0.0), axis=0, keepdims=True)
    lf_row = jnp.sum(jnp.where(sub == cf, gt, 0.0), axis=0, keepdims=True)

    tt = lax.broadcasted_iota(jnp.int32, (L, L), 0)
    ss = lax.broadcasted_iota(jnp.int32, (L, L), 1)
    seen = (ss >= tt) if d else (ss <= tt)
    seen_t = (tt >= ss) if d else (tt <= ss)
    b_col = jnp.sum(jnp.where(seen, lf_row, 0.0), axis=1, keepdims=True)
    b_row = jnp.sum(jnp.where(seen_t, lf_col, 0.0), axis=0, keepdims=True)
    b_last = jnp.sum(lf_row, axis=1, keepdims=True)

    m_prev = m_ref[...]
    r_row = li_row - b_row
    m_t = b_col + jnp.maximum(m_prev, jnp.max(jnp.where(seen, r_row, -jnp.inf), axis=1, keepdims=True))
    sc = _dot_nt(q, k) * jnp.exp2(jnp.where(seen, (b_col - m_t) + r_row, -jnp.inf))
    w_inter = jnp.exp2(b_col + m_prev - m_t)
    n_rows = jnp.broadcast_to(n_ref[...], (2 * HALO, n_ref.shape[1])).astype(BF16)
    qn = _dot_nt(q, n_rows)[:, :1]
    num = _dot(sc.astype(BF16), v) + w_inter * _dot(q, ct_ref[...].astype(BF16))
    den = jnp.sum(sc, axis=1, keepdims=True) + w_inter * qn
    hh = num * (1.0 / jnp.maximum(jnp.abs(den), jnp.exp2(-m_t)))

    g_row = b_last + r_row
    g_col = b_last - b_col + li_col
    m_new = jnp.maximum(b_last + m_prev, jnp.max(g_row, axis=1, keepdims=True))
    decay = jnp.exp2(b_last + m_prev - m_new)
    vw = (v.astype(F32) * jnp.exp2(g_col - m_new)).astype(BF16)
    wk_rows = jnp.broadcast_to(jnp.exp2(g_row - m_new), (2 * HALO, L)).astype(BF16)
    ct_ref[...] = decay * ct_ref[...] + _dot_tn(k, vw)
    n_ref[...] = decay * n_ref[...] + _dot(wk_rows, k)[:1, :]
    m_ref[...] = m_new
    return hh


def _bwd_block(s, nblk):
    return jnp.where(s == 0, 0, nblk - s)


def _ml_core_kernel(qf_ref, kf_ref, vf_ref, gf_ref, qb_ref, kb_ref, vb_ref, gb_ref, o_ref, hacc_ref, *state,
                    heads, nblk, dh):
    s = pl.program_id(2)
    L = qf_ref.shape[0]
    per_step = len(state) // 6

    @pl.when(s == 0)
    def _():
        for ref in (hacc_ref,) + state:
            ref[...] = jnp.zeros_like(ref)

    rows_f = pl.ds(pl.multiple_of(s * L, L), L)
    rows_b = pl.ds(pl.multiple_of(_bwd_block(s, nblk) * L, L), L)
    gf, gb = gf_ref[...], gb_ref[...]
    for i in range(per_step):
        h = pl.program_id(1) * per_step + i
        cols = slice(i * dh, (i + 1) * dh)
        ctf, nf, mf, ctb, nb, mb = state[6 * i:6 * i + 6]
        hacc_ref[rows_f, cols] += _ml_chunk(0, qf_ref[:, cols], kf_ref[:, cols], vf_ref[:, cols], gf,
                                            ctf, nf, mf, h, h + heads)
        hacc_ref[rows_b, cols] += _ml_chunk(1, qb_ref[:, cols], kb_ref[:, cols], vb_ref[:, cols], gb,
                                            ctb, nb, mb, 2 * heads + h, 3 * heads + h)

    @pl.when(s == nblk - 1)
    def _():
        for j in range(nblk):
            for i in range(per_step):
                tot = hacc_ref[j * L:(j + 1) * L, i * dh:(i + 1) * dh]
                cen = tot - jnp.mean(tot, axis=-1, keepdims=True)
                var = jnp.mean(cen * cen, axis=-1, keepdims=True)
                o_ref[j * L:(j + 1) * L, i * dh:(i + 1) * dh] = (cen * lax.rsqrt(var + EPS)).astype(BF16)


def _ml_core(q, k, v, gates, dh):
    bsz, t, inner = q.shape
    heads = inner // dh
    nblk = t // ROW_BLOCK
    per_step = ML_HEADS_PER_STEP
    assert heads % per_step == 0
    width = per_step * dh
    fwd = pl.BlockSpec((None, ROW_BLOCK, width), lambda b, h, s: (b, s, h))
    bwd = pl.BlockSpec((None, ROW_BLOCK, width), lambda b, h, s: (b, _bwd_block(s, nblk), h))
    gate_f = pl.BlockSpec((None, ROW_BLOCK, LANE), lambda b, h, s: (b, s, 0))
    gate_b = pl.BlockSpec((None, ROW_BLOCK, LANE), lambda b, h, s: (b, _bwd_block(s, nblk), 0))
    state = [pltpu.VMEM((dh, dh), F32), pltpu.VMEM((1, dh), F32), pltpu.VMEM((1, 1), F32)]
    kern = functools.partial(_ml_core_kernel, heads=heads, nblk=nblk, dh=dh)
    return pl.pallas_call(
        kern,
        grid=(bsz, heads // per_step, nblk),
        in_specs=[fwd, fwd, fwd, gate_f, bwd, bwd, bwd, gate_b],
        out_specs=pl.BlockSpec((None, t, width), lambda b, h, s: (b, 0, h)),
        out_shape=jax.ShapeDtypeStruct((bsz, t, inner), BF16),
        scratch_shapes=[pltpu.VMEM((t, width), F32)] + state * (2 * per_step),
        compiler_params=_params(("arbitrary",) * 3),
        name="ml_core",
    )(q, k, v, gates, q, k, v, gates)


def _ml_out_kernel(hn_ref, uc_ref, z_ref, *refs, nres):
    res, (mod_ref, hg_ref, sk_ref, w_ref, o_ref) = refs[:nres], refs[nres:]
    z = z_ref[...].astype(F32)
    y = (hn_ref[...].astype(F32) * hg_ref[...] + sk_ref[...] * uc_ref[...].astype(F32)) * _silu(z)
    o_ref[...] = _residual_block(res) + mod_ref[2:3, :] * _dot(y.astype(BF16), w_ref[...])


def _ml_out(rows, hn, uc, z, r, mods, hnorm_g, skip, w_out):
    bsz, t, d = _residual_shape(r)
    inner = hn.shape[-1]
    res_specs, res_args = rows.residual(r)
    return pl.pallas_call(
        functools.partial(_ml_out_kernel, nres=len(res_args)),
        grid=rows.grid,
        in_specs=[rows.rows(inner), rows.rows(inner), rows.rows(inner)] + res_specs
        + [rows.mod(d), _const2((1, inner)), _const2((1, inner)), _const2((inner, d))],
        out_specs=rows.rows(d),
        out_shape=jax.ShapeDtypeStruct((bsz, t, d), F32),
        compiler_params=_params(("arbitrary", "arbitrary")),
        name="ml_out",
    )(hn, uc, z, *res_args, mods, hnorm_g, skip, w_out)


def _na_attn_kernel(q_ref, k_ref, v_ref, *rest, first, ctx_len, rows, groups):
    b_refs = rest[:NA_ROWS_PER_STEP]
    r_ref, mod_ref, w_ref, bo_ref, o_ref, a_buf = rest[NA_ROWS_PER_STEP:]
    gw = GRID_W
    per = NA_GROUP // NA_HEAD_DIM
    nloc = NA_KH * gw
    lane = lax.broadcasted_iota(jnp.int32, (gw, NA_GROUP), 1)
    mine = [(lane >= h * NA_HEAD_DIM) & (lane < (h + 1) * NA_HEAD_DIM) for h in range(per)]

    def key_start(sub):
        t = (pl.program_id(1) + first) * NA_ROWS_PER_STEP + sub
        r = t - ctx_len // gw
        rs = jnp.clip(r - NA_KH // 2, 0, rows - NA_KH)
        return pl.multiple_of(ctx_len + rs * gw, gw)

    kstart = [key_start(sub) for sub in range(NA_ROWS_PER_STEP)]

    def scores(sub, g):
        qg = q_ref[g, sub * gw:(sub + 1) * gw, :].astype(F32)
        qm = jnp.concatenate([jnp.where(mine[h], qg, 0.0) for h in range(per)], axis=0).astype(BF16)
        return _dot_nt(qm, k_ref[g, pl.ds(kstart[sub], nloc), :]), _dot_nt(qm, k_ref[g, 0:ctx_len, :])

    work = [(sub, g) for sub in range(NA_ROWS_PER_STEP) for g in range(groups)]
    ahead = scores(*work[0])
    for i, (sub, g) in enumerate(work):
        s_loc, s_ctx = ahead
        if i + 1 < len(work):
            ahead = scores(*work[i + 1])
        s_loc = s_loc + b_refs[sub][per * g:per * (g + 1)].reshape(per * gw, nloc)
        m = jnp.maximum(jnp.max(s_loc, axis=1, keepdims=True), jnp.max(s_ctx, axis=1, keepdims=True))
        p_loc = jnp.exp2(s_loc - m)
        p_ctx = jnp.exp2(s_ctx - m)
        den = jnp.sum(p_loc, axis=1, keepdims=True) + jnp.sum(p_ctx, axis=1, keepdims=True)
        o = (_dot(p_loc.astype(BF16), v_ref[g, pl.ds(kstart[sub], nloc), :])
             + _dot(p_ctx.astype(BF16), v_ref[g, 0:ctx_len, :])) * (1.0 / den)
        og = jnp.where(mine[0], o[0:gw], 0.0)
        for h in range(1, per):
            og = jnp.where(mine[h], o[h * gw:(h + 1) * gw], og)
        a_buf[sub * gw:(sub + 1) * gw, g * NA_GROUP:(g + 1) * NA_GROUP] = og.astype(BF16)

    o_ref[...] = r_ref[...] + mod_ref[2:3, :] * (_dot(a_buf[...], w_ref[...]) + bo_ref[...])


def _na_attn(live_rows, q, k, v, bias, r, mods, w_out, b_out, ctx_len):
    bsz, groups, t, _ = q.shape
    d = groups * NA_GROUP
    rows = (t - ctx_len) // GRID_W
    blk = NA_ROWS_PER_STEP * GRID_W
    assert blk == ROW_BLOCK and ctx_len % blk == 0 and t % blk == 0
    first = live_rows.first
    steps = t // blk - first
    heads, nvar = bias.shape[:2]

    def variant(tq):
        r = tq - ctx_len // GRID_W
        rs = jnp.clip(r - NA_KH // 2, 0, rows - NA_KH)
        return jnp.where(r < 0, nvar - 1, rs - r + NA_KH - 1)

    def bias_spec(sub):
        return pl.BlockSpec((heads, None) + bias.shape[2:],
                            lambda b, j: (0, variant((j + first) * NA_ROWS_PER_STEP + sub), 0, 0))

    whole = pl.BlockSpec((None, groups, t, NA_GROUP), lambda b, j: (b, 0, 0, 0))
    kern = functools.partial(_na_attn_kernel, first=first, ctx_len=ctx_len, rows=rows, groups=groups)
    return pl.pallas_call(
        kern,
        grid=(bsz, steps),
        in_specs=[pl.BlockSpec((None, groups, blk, NA_GROUP), lambda b, j: (b, 0, j + first, 0)), whole, whole]
        + [bias_spec(sub) for sub in range(NA_ROWS_PER_STEP)]
        + [live_rows.rows(d), live_rows.mod(d), _const2((d, d)), _const2((1, d))],
        out_specs=live_rows.rows(d),
        out_shape=jax.ShapeDtypeStruct((bsz, t, d), F32),
        scratch_shapes=[pltpu.VMEM((blk, d), BF16)],
        compiler_params=_params(("arbitrary", "arbitrary")),
        name="na_attn",
    )(q, k, v, *([bias] * NA_ROWS_PER_STEP), r, mods, w_out, b_out)


def _na_bias_table(rpb):
    heads, nrow, ncol = rpb.shape
    col = np.arange(GRID_W)
    col_start = np.clip(col - NA_KW // 2, 0, GRID_W - NA_KW)
    col_ok = (col[None, :] >= col_start[:, None]) & (col[None, :] < col_start[:, None] + NA_KW)
    w = GRID_W
    left = w - NA_KW
    padded = jnp.pad(rpb.astype(F32), ((0, 0), (0, 0), (left, 2 * w - 1 - left - ncol)))
    toep = jnp.tile(padded, (1, 1, w))[:, :, w - 1:w - 1 + w * (2 * w - 2)]
    toep = toep.reshape(heads, nrow, w, 2 * w - 2)[:, :, :, :w]
    full = jnp.where(jnp.asarray(col_ok[None, :, None, :]), toep.transpose(0, 2, 1, 3), -jnp.inf)
    nloc = NA_KH * GRID_W
    var = [full[:, :, o:o + NA_KH, :].reshape(heads, GRID_W, nloc) for o in range(NA_KH)]
    var.append(jnp.full((heads, GRID_W, nloc), -jnp.inf, F32))
    return jnp.stack(var, axis=1)


def _ffn_kernel(x_ref, xp_ref, xn_ref, mod_ref, g_ref, wu_ref, bu_ref, cw_ref, cb_ref, wd_ref, bd_ref,
                *rest, latent, final, merged):
    rest = list(rest)
    fg_ref = rest.pop(0) if final else None
    if merged:
        rest.pop(0)
    o_ref, h_buf, acc_ref, *u_bufs = rest
    rows = x_ref.shape[0]
    x = x_ref[...]
    ext = jnp.concatenate([xp_ref[...], x, xn_ref[...]], axis=0)
    h_buf[...] = _norm_mod(ext, g_ref[...], mod_ref[3:4, :], mod_ref[4:5, :]).astype(BF16)
    if latent:
        has_prev = pl.program_id(1) > 0
        has_next = pl.program_id(1) < pl.num_programs(1) - 1
    else:
        has_prev = has_next = False
    nchunk = wd_ref.shape[0]
    taps = cw_ref.shape[1]
    acc_ref[...] = jnp.zeros_like(acc_ref)

    def up(c, slot):
        for half in range(2):
            cc = half * nchunk + c
            raw = _dot(h_buf[...], wu_ref[cc])
            buf = u_bufs[2 * slot + half]
            buf[HALO:HALO + rows, :] = raw[HALO:HALO + rows, :]
            buf[0:HALO, :] = jnp.where(has_prev, raw[0:HALO, :], -bu_ref[cc])
            buf[HALO + rows:, :] = jnp.where(has_next, raw[HALO + rows:, :], -bu_ref[cc])

    def down(c, slot):
        halves = []
        for half in range(2):
            cc = half * nchunk + c
            y = cb_ref[cc] + bu_ref[cc] * jnp.sum(cw_ref[cc], axis=0, keepdims=True)
            u = u_bufs[2 * slot + half][...]
            for t in range(taps):
                shifted = u if t == taps // 2 else pltpu.roll(u, (taps // 2 - t) % u.shape[0], axis=0)
                y = y + cw_ref[cc, t:t + 1, :] * shifted[HALO:HALO + rows, :]
            halves.append(y)
        act = (halves[0] * _silu(halves[1])).astype(BF16)
        acc_ref[...] += _dot(act, wd_ref[c])

    def stage(c, slot):
        up(c + 1, 1 - slot)
        down(c, slot)

    unroll = FF_UNROLL
    assert unroll % 2 == 0
    up(0, 0)

    def body(i, carry):
        for k in range(unroll):
            stage(unroll * i + k, k % 2)
        return carry

    looped = (nchunk - 1) // unroll
    lax.fori_loop(0, looped, body, 0)
    for c in range(looped * unroll, nchunk - 1):
        stage(c, c % 2)
    down(nchunk - 1, (nchunk - 1) % 2)
    o = x + mod_ref[5:6, :] * (acc_ref[...] + bd_ref[...])
    if final:
        o = o * lax.rsqrt(jnp.mean(o * o, axis=-1, keepdims=True) + EPS) * fg_ref[...]
    o_ref[...] = o


def _ffn_call(r, mods, layer, ctx_len, weights, latent, final_g=None, merge_into=None):
    bsz, t, d = r.shape
    final = final_g is not None
    merged = merge_into is not None
    if latent:
        rows = FF_ROWS
        assert (t - ctx_len) % rows == 0
        grid = (bsz, (t - ctx_len) // rows)
        start = lambda j: ctx_len + rows * j
        tile = lambda i: pl.multiple_of(i, HALO)
        cur = pl.BlockSpec((None, pl.Element(rows), pl.Element(d)), lambda b, j: (b, tile(start(j)), 0))
        prev = pl.BlockSpec((None, pl.Element(HALO), pl.Element(d)),
                            lambda b, j: (b, tile(jnp.maximum(start(j) - HALO, ctx_len)), 0))
        nxt = pl.BlockSpec((None, pl.Element(HALO), pl.Element(d)),
                           lambda b, j: (b, tile(jnp.minimum(start(j + 1), t - HALO)), 0))
        mod = pl.BlockSpec((None, None, 6, d), lambda b, j: (layer, b, 0, 0))
    else:
        rows = ctx_len
        grid = (bsz, 1)
        cur = pl.BlockSpec((None, rows, d), lambda b, j: (b, 0, 0))
        prev = nxt = pl.BlockSpec((None, HALO, d), lambda b, j: (b, 0, 0))
        mod = pl.BlockSpec((None, None, 6, d), lambda b, j: (layer, bsz, 0, 0))
    in_specs = [cur, prev, nxt, mod] + [_layer_const(w, layer) for w in weights]
    args = [r, r, r, mods, *weights]
    if final:
        in_specs.append(_const2((1, d)))
        args.append(final_g)
    aliases = {}
    if merged:
        aliases = {len(args): 0}
        in_specs.append(pl.BlockSpec(memory_space=pl.ANY))
        args.append(merge_into)
    if final:
        assert latent
        out_spec = pl.BlockSpec((None, rows, d), lambda b, j: (b, j, 0))
        out_shape = jax.ShapeDtypeStruct((bsz, t - ctx_len, d), F32)
    else:
        out_spec = cur
        out_shape = jax.ShapeDtypeStruct((bsz, t, d), F32)
    kern = functools.partial(_ffn_kernel, latent=latent, final=final, merged=merged)
    return pl.pallas_call(
        kern,
        grid=grid,
        in_specs=in_specs,
        out_specs=out_spec,
        out_shape=out_shape,
        input_output_aliases=aliases,
        scratch_shapes=[pltpu.VMEM((rows + 2 * HALO, d), BF16), pltpu.VMEM((rows, d), F32)]
        + [pltpu.VMEM((rows + 2 * HALO, FF_CHUNK), F32)] * 4,
        compiler_params=_params(("arbitrary", "arbitrary")),
        name="ffn_latent" if latent else "ffn_context",
    )(*args)


def _ffn_weights(g, w_up, b_up, conv_w, conv_b, w_down, b_down):
    layers, d, hidden2 = w_up.shape
    nc2 = hidden2 // FF_CHUNK
    taps = conv_w.shape[1]
    return (g.reshape(layers, 1, d),
            w_up.astype(BF16).reshape(layers, d, nc2, FF_CHUNK).transpose(0, 2, 1, 3),
            b_up.reshape(layers, nc2, 1, FF_CHUNK),
            conv_w.reshape(layers, taps, nc2, FF_CHUNK).transpose(0, 2, 1, 3),
            conv_b.reshape(layers, nc2, 1, FF_CHUNK),
            w_down.astype(BF16).reshape(layers, nc2 // 2, FF_CHUNK, d),
            b_down.reshape(layers, 1, d))


def _block_diag_tiles(w):
    nb, bs, _ = w.shape
    per = BD_TILE // bs
    rep = jnp.tile(w.reshape(nb * bs, bs), (1, per)).reshape(nb // per, BD_TILE, BD_TILE)
    pos = np.arange(BD_TILE) // bs
    return jnp.where(jnp.asarray(pos[:, None] == pos[None, :]), rep, 0.0).astype(BF16)


def _gate_cols(gate_w, gate_b, inner):
    dirs, _, ng = gate_w.shape
    g = gate_w.transpose(1, 0, 2).reshape(3 * inner, dirs * ng)
    g = jnp.pad(g, ((0, 0), (0, LANE - dirs * ng))).astype(BF16)
    gb = jnp.pad(gate_b.reshape(1, dirs * ng), ((0, 0), (0, LANE - dirs * ng)))
    return g[:inner], g[inner:2 * inner], g[2 * inner:], gb


def kernel(x, c, ctx, c_ctx, ada_w, ada_b, norm1_g, norm2_g, final_g, ml_w_in, ml_conv_w, ml_conv_b, ml_wq, ml_wk, ml_wv, ml_gate_w, ml_gate_b, ml_skip, ml_hnorm_g, ml_w_out, na_w_qkv, na_b_qkv, na_rpb, na_w_out, na_b_out, ff_w_up, ff_b_up, ff_conv_w, ff_conv_b, ff_w_down, ff_b_down):
    bsz, seq, d = x.shape
    ctx_len = ctx.shape[1]
    depth = ada_w.shape[0]
    assert ctx_len == ROW_BLOCK and seq % ROW_BLOCK == 0 and seq % GRID_W == 0
    assert bsz + 1 <= MOD_ROWS
    t = ctx_len + seq
    nblk = t // ROW_BLOCK

    c_all = jnp.zeros((MOD_ROWS, d), F32).at[:bsz].set(c).at[bsz].set(c_ctx)
    mods = _ada(c_all, ada_w, ada_b).reshape(depth, MOD_ROWS, 6, d)

    r = (ctx, x)
    row2 = lambda a: a.reshape(1, -1)
    ffn_weights = _ffn_weights(norm2_g, ff_w_up, ff_b_up, ff_conv_w, ff_conv_b, ff_w_down, ff_b_down)
    out = None
    for i in range(depth):
        need_ctx = i < depth - 1
        j = i // 2
        all_rows = _Rows(bsz, nblk, i, 0)
        live_rows = all_rows if need_ctx else _Rows(bsz, nblk, i, 1)
        if i % 2 == 0:
            inner = ml_w_in.shape[2] // 2
            dh = inner // ML_HEADS
            u, z = _ml_in(all_rows, r, mods, row2(norm1_g[i]), ml_w_in[j].astype(BF16))
            gq, gk, gv, gb = _gate_cols(ml_gate_w[j], ml_gate_b[j], inner)
            uc, q, k, v, gates = _ml_proj(u, ml_conv_w[j], row2(ml_conv_b[j]), _block_diag_tiles(ml_wq[j]),
                                          _block_diag_tiles(ml_wk[j]), _block_diag_tiles(ml_wv[j]),
                                          gq, gk, gv, gb, dh, dh ** -0.5,
                                          ml_gate_w.shape[1] * ml_gate_w.shape[3])
            hn = _ml_core(q, k, v, gates, dh)
            r = _ml_out(live_rows, hn, uc, z, r, mods, row2(ml_hnorm_g[j]), row2(ml_skip[j]),
                        ml_w_out[j].astype(BF16))
        else:
            q, k, v = _na_qkv(all_rows, r, mods, row2(norm1_g[i]), na_w_qkv[j].astype(BF16), row2(na_b_qkv[j]),
                              NA_HEAD_DIM ** -0.5 * LOG2E)
            r = _na_attn(live_rows, q, k, v, _na_bias_table(na_rpb[j] * LOG2E), r, mods,
                         na_w_out[j].astype(BF16), row2(na_b_out[j]), ctx_len)
        if need_ctx:
            r_new = _ffn_call(r, mods, i, ctx_len, ffn_weights, latent=True)
            r = _ffn_call(r, mods, i, ctx_len, ffn_weights, latent=False, merge_into=r_new)
        else:
            out = _ffn_call(r, mods, i, ctx_len, ffn_weights, latent=True, final_g=row2(final_g))
    return out
```

```python
import functools

import jax
import jax.numpy as jnp
import numpy as np
from jax import lax
from jax.experimental import pallas as pl
from jax.experimental.pallas import tpu as pltpu

F32 = jnp.float32
BF16 = jnp.bfloat16
EPS = 1e-6
LOG2E = 1.4426950408889634

GRID_W = 64
ML_HEADS = 4
ML_QKV_BLOCK = 4
NA_HEAD_DIM = 64
NA_KH = 8
NA_KW = 16

ROW_BLOCK = 256
HALO = 8
BD_TILE = 256
LANE = 128
NA_GROUP = 256
NA_ROWS_PER_STEP = 4
ML_HEADS_PER_STEP = 2
FF_CHUNK = 256
FF_UNROLL = 4
FF_ROWS = 512
MOD_ROWS = 16
VMEM_LIMIT = 56 * 1024 * 1024


def _dot(a, b):
    return jnp.dot(a, b, preferred_element_type=F32)


def _dot_nt(a, b):
    return lax.dot_general(a, b, (((1,), (1,)), ((), ())), preferred_element_type=F32)


def _dot_tn(a, b):
    return lax.dot_general(a, b, (((0,), (0,)), ((), ())), preferred_element_type=F32)


def _silu(x):
    h = 0.5 * x
    return h + h * jnp.tanh(h)


def _log_sigmoid(x):
    return jnp.minimum(x, 0.0) - jnp.log1p(jnp.exp(-jnp.abs(x)))


def _norm_mod(x, g, shift, scale):
    y = x * lax.rsqrt(jnp.mean(x * x, axis=-1, keepdims=True) + EPS) * g
    return y * (1.0 + scale) + shift


def _params(sem):
    return pltpu.CompilerParams(dimension_semantics=sem, vmem_limit_bytes=VMEM_LIMIT)


def _resident(shape, index_map):
    return pl.BlockSpec(shape, index_map, pipeline_mode=pl.Buffered(1))


def _ada_kernel(c_ref, w_ref, b_ref, o_ref):
    s = _silu(c_ref[...]).astype(BF16)
    o_ref[...] = _dot(s, w_ref[...].astype(BF16)) + b_ref[...]


def _ada(c_all, ada_w, ada_b):
    depth, d, d6 = ada_w.shape
    n = d6 // d
    return pl.pallas_call(
        _ada_kernel,
        grid=(depth, n),
        in_specs=[
            pl.BlockSpec((MOD_ROWS, d), lambda l, j: (0, 0)),
            pl.BlockSpec((None, d, d), lambda l, j: (l, 0, j)),
            pl.BlockSpec((None, 1, d), lambda l, j: (l, 0, j)),
        ],
        out_specs=pl.BlockSpec((None, MOD_ROWS, d), lambda l, j: (l, 0, j)),
        out_shape=jax.ShapeDtypeStruct((depth, MOD_ROWS, d6), F32),
        compiler_params=_params(("arbitrary", "arbitrary")),
        name="ada",
    )(c_all, ada_w, ada_b.reshape(depth, 1, d6))


class _Rows:
    def __init__(self, bsz, nblk_total, layer, first_blk):
        self.bsz = bsz
        self.nblk_total = nblk_total
        self.layer = layer
        self.first = first_blk
        self.grid = (bsz, nblk_total - first_blk)

    def rows(self, width, rows=ROW_BLOCK):
        f = self.first
        return pl.BlockSpec((None, rows, width), lambda b, j: (b, j + f, 0))

    def residual(self, r):
        if not isinstance(r, tuple):
            return [self.rows(r.shape[-1])], [r]
        assert self.first == 0
        d = r[1].shape[-1]
        return ([pl.BlockSpec((None, ROW_BLOCK, d), lambda b, j: (b, 0, 0)),
                 pl.BlockSpec((None, ROW_BLOCK, d), lambda b, j: (b, jnp.maximum(j - 1, 0), 0))], list(r))

    def mod(self, d):
        f, bsz, layer = self.first, self.bsz, self.layer
        return pl.BlockSpec((None, None, 6, d), lambda b, j: (layer, jnp.where(j + f == 0, bsz, b), 0, 0))


def _const2(shape):
    return _resident(shape, lambda b, j: (0,) * len(shape))


def _layer_const(stacked, layer):
    shape = stacked.shape[1:]
    return _resident((None,) + shape, lambda b, j: (layer,) + (0,) * len(shape))


def _residual_block(refs):
    if len(refs) == 1:
        return refs[0][...]
    return jnp.where(pl.program_id(1) == 0, refs[0][...], refs[1][...])


def _residual_shape(r):
    if isinstance(r, tuple):
        return r[1].shape[0], r[0].shape[1] + r[1].shape[1], r[1].shape[2]
    return r.shape


def _proj_kernel(*refs, nres, shift, scale, chunk, splits, has_bias, out_scale, silu_out=()):
    res, (mod_ref, g_ref, w_ref), rest = refs[:nres], refs[nres:nres + 3], refs[nres + 3:]
    if has_bias:
        b_ref, out_refs = rest[0], rest[1:]
    else:
        b_ref, out_refs = None, rest
    h = _norm_mod(_residual_block(res), g_ref[...], mod_ref[shift:shift + 1, :],
                  mod_ref[scale:scale + 1, :]).astype(BF16)
    n = w_ref.shape[1]
    per_out = n // len(out_refs)
    for c in range(n // chunk):
        y = _dot(h, w_ref[:, c * chunk:(c + 1) * chunk])
        if has_bias:
            y = y + b_ref[:, c * chunk:(c + 1) * chunk]
        oi, off = divmod(c * chunk, per_out)
        if out_scale[oi] != 1.0:
            y = y * out_scale[oi]
        if oi in silu_out:
            y = _silu(y)
        y = y.astype(BF16)
        o_ref = out_refs[oi]
        if splits:
            for t in range(chunk // splits):
                o_ref[(off + t * splits) // splits] = y[:, t * splits:(t + 1) * splits]
        else:
            o_ref[:, off:off + chunk] = y


def _ml_in(rows, r, mods, g, w_in):
    bsz, t, d = _residual_shape(r)
    n = w_in.shape[1]
    half = n // 2
    res_specs, res_args = rows.residual(r)
    kern = functools.partial(_proj_kernel, nres=len(res_args), shift=0, scale=1, chunk=512, splits=0, has_bias=False,
                             out_scale=(1.0, 1.0), silu_out=(1,))
    return pl.pallas_call(
        kern,
        grid=rows.grid,
        in_specs=res_specs + [rows.mod(d), _const2((1, d)), _const2((d, n))],
        out_specs=[rows.rows(half), rows.rows(half)],
        out_shape=[jax.ShapeDtypeStruct((bsz, t, half), BF16)] * 2,
        compiler_params=_params(("arbitrary", "arbitrary")),
        name="ml_in",
    )(*res_args, mods, g, w_in)


def _na_qkv(rows, r, mods, g, w, b, q_scale):
    bsz, t, d = r.shape
    groups = d // NA_GROUP
    f = rows.first
    out_spec = pl.BlockSpec((None, groups, ROW_BLOCK, NA_GROUP), lambda bb, j: (bb, 0, j + f, 0))
    kern = functools.partial(_proj_kernel, nres=1, shift=0, scale=1, chunk=NA_GROUP, splits=NA_GROUP, has_bias=True,
                             out_scale=(q_scale, 1.0, 1.0))
    return pl.pallas_call(
        kern,
        grid=rows.grid,
        in_specs=[rows.rows(d), rows.mod(d), _const2((1, d)), _const2((d, 3 * d)), _const2((1, 3 * d))],
        out_specs=[out_spec] * 3,
        out_shape=[jax.ShapeDtypeStruct((bsz, groups, t, NA_GROUP), BF16)] * 3,
        compiler_params=_params(("arbitrary", "arbitrary")),
        name="na_qkv",
    )(r, mods, g, w, b)


def _ml_proj_kernel(u_ref, cw_ref, cb_ref, wq_ref, wk_ref, wv_ref, gq_ref, gk_ref, gv_ref, gb_ref,
                    uc_ref, q_ref, k_ref, v_ref, g_ref, *, nblk, kscale, ngates):
    @pl.when(pl.program_id(1) == 0)
    def _():
        g_ref[...] = jnp.broadcast_to(gb_ref[...], g_ref.shape)

    cw = cw_ref[...]
    cb = cb_ref[...]
    taps = cw.shape[0]
    width = u_ref.shape[1]
    pad = 2 * HALO
    for j in range(nblk):
        r0 = j * ROW_BLOCK
        cur_b = u_ref[r0:r0 + ROW_BLOCK, :]
        cur = cur_b.astype(F32)
        if j in (0, 1):
            prev = jnp.zeros((HALO, width), F32)
        else:
            prev = u_ref[r0 - pad:r0, :].astype(F32)[HALO:, :]
        if j in (0, nblk - 1):
            nxt = jnp.zeros((HALO, width), F32)
        else:
            nxt = u_ref[r0 + ROW_BLOCK:r0 + ROW_BLOCK + pad, :].astype(F32)[:HALO, :]
        ext = jnp.concatenate([prev, cur, nxt], axis=0)
        acc = cb
        for t in range(taps):
            shifted = ext if t == taps // 2 else pltpu.roll(ext, (taps // 2 - t) % ext.shape[0], axis=0)
            acc = acc + cw[t:t + 1, :] * shifted[HALO:HALO + ROW_BLOCK, :]
        uc_b = _silu(acc).astype(BF16)
        uc_ref[r0:r0 + ROW_BLOCK, :] = uc_b
        qs, ks, vs = [], [], []
        for t in range(width // BD_TILE):
            sl = slice(t * BD_TILE, (t + 1) * BD_TILE)
            qs.append(_dot(uc_b[:, sl], wq_ref[t]))
            ks.append(_dot(uc_b[:, sl], wk_ref[t]) * kscale)
            vs.append(_dot(cur_b[:, sl], wv_ref[t]))
        q_b = jnp.concatenate(qs, axis=1).astype(BF16)
        k_b = jnp.concatenate(ks, axis=1).astype(BF16)
        v_b = jnp.concatenate(vs, axis=1).astype(BF16)
        q_ref[r0:r0 + ROW_BLOCK, :] = q_b
        k_ref[r0:r0 + ROW_BLOCK, :] = k_b
        v_ref[r0:r0 + ROW_BLOCK, :] = v_b
        g_ref[r0:r0 + ROW_BLOCK, :] += _dot(q_b, gq_ref[...]) + _dot(k_b, gk_ref[...]) + _dot(v_b, gv_ref[...])

    @pl.when(pl.program_id(1) == pl.num_programs(1) - 1)
    def _():
        lane = lax.broadcasted_iota(jnp.int32, (ROW_BLOCK, g_ref.shape[1]), 1)
        forget = (lane < ngates) & (lane % (ngates // 2) >= ngates // 4)
        for j in range(nblk):
            pre = g_ref[j * ROW_BLOCK:(j + 1) * ROW_BLOCK, :]
            g_ref[j * ROW_BLOCK:(j + 1) * ROW_BLOCK, :] = jnp.where(forget, _log_sigmoid(pre), pre) * LOG2E


def _ml_proj(u, conv_w, conv_b, wq_bd, wk_bd, wv_bd, gq, gk, gv, gb, dh, kscale, ngates):
    bsz, t, inner = u.shape
    heads = inner // dh
    tiles = dh // BD_TILE
    taps = conv_w.shape[0]
    col = lambda shape: pl.BlockSpec(shape, lambda b, h: (0, h))
    seq = pl.BlockSpec((None, t, dh), lambda b, h: (b, 0, h))
    bd = pl.BlockSpec((tiles, BD_TILE, BD_TILE), lambda b, h: (h, 0, 0))
    gw = pl.BlockSpec((dh, LANE), lambda b, h: (h, 0))
    kern = functools.partial(_ml_proj_kernel, nblk=t // ROW_BLOCK, kscale=kscale, ngates=ngates)
    return pl.pallas_call(
        kern,
        grid=(bsz, heads),
        in_specs=[seq, col((taps, dh)), col((1, dh)), bd, bd, bd, gw, gw, gw,
                  pl.BlockSpec((1, LANE), lambda b, h: (0, 0))],
        out_specs=[seq, seq, seq, seq, pl.BlockSpec((None, t, LANE), lambda b, h: (b, 0, 0))],
        out_shape=[jax.ShapeDtypeStruct((bsz, t, inner), BF16)] * 4 + [jax.ShapeDtypeStruct((bsz, t, LANE), F32)],
        compiler_params=_params(("arbitrary", "arbitrary")),
        name="ml_proj",
    )(u, conv_w, conv_b, wq_bd, wk_bd, wv_bd, gq, gk, gv, gb)


def _ml_chunk(d, q, k, v, g, ct_ref, n_ref, m_ref, ci, cf):
    L = q.shape[0]
    gt = g.T
    lane = lax.broadcasted_iota(jnp.int32, g.shape, 1)
    sub = lax.broadcasted_iota(jnp.int32, gt.shape, 0)
    li_col = jnp.sum(jnp.where(lane == ci, g, 0.0), axis=1, keepdims=True)
    lf_col = jnp.sum(jnp.where(lane == cf, g, 0.0), axis=1, keepdims=True)
    li_row = jnp.sum(jnp.where(sub == ci, gt, 0.0), axis=0, keepdims=True)
    lf_row = jnp.sum(jnp.where(sub == cf, gt, 0.0), axis=0, keepdims=True)

    tt = lax.broadcasted_iota(jnp.int32, (L, L), 0)
    ss = lax.broadcasted_iota(jnp.int32, (L, L), 1)
    seen = (ss >= tt) if d else (ss <= tt)
    seen_t = (tt >= ss) if d else (tt <= ss)
    b_col = jnp.sum(jnp.where(seen, lf_row, 0.0), axis=1, keepdims=True)
    b_row = jnp.sum(jnp.where(seen_t, lf_col, 0.0), axis=0, keepdims=True)
    b_last = jnp.sum(lf_row, axis=1, keepdims=True)

    m_prev = m_ref[...]
    r_row = li_row - b_row
    m_t = b_col + jnp.maximum(m_prev, jnp.max(jnp.where(seen, r_row, -jnp.inf), axis=1, keepdims=True))
    sc = _dot_nt(q, k) * jnp.exp2(jnp.where(seen, (b_col - m_t) + r_row, -jnp.inf))
    w_inter = jnp.exp2(b_col + m_prev - m_t)
    n_rows = jnp.broadcast_to(n_ref[...], (2 * HALO, n_ref.shape[1])).astype(BF16)
    qn = _dot_nt(q, n_rows)[:, :1]
    num = _dot(sc.astype(BF16), v) + w_inter * _dot(q, ct_ref[...].astype(BF16))
    den = jnp.sum(sc, axis=1, keepdims=True) + w_inter * qn
    hh = num * (1.0 / jnp.maximum(jnp.abs(den), jnp.exp2(-m_t)))

    g_row = b_last + r_row
    g_col = b_last - b_col + li_col
    m_new = jnp.maximum(b_last + m_prev, jnp.max(g_row, axis=1, keepdims=True))
    decay = jnp.exp2(b_last + m_prev - m_new)
    vw = (v.astype(F32) * jnp.exp2(g_col - m_new)).astype(BF16)
    wk_rows = jnp.broadcast_to(jnp.exp2(g_row - m_new), (2 * HALO, L)).astype(BF16)
    ct_ref[...] = decay * ct_ref[...] + _dot_tn(k, vw)
    n_ref[...] = decay * n_ref[...] + _dot(wk_rows, k)[:1, :]
    m_ref[...] = m_new
    return hh


def _bwd_block(s, nblk):
    return jnp.where(s == 0, 0, nblk - s)


def _ml_core_kernel(qf_ref, kf_ref, vf_ref, gf_ref, qb_ref, kb_ref, vb_ref, gb_ref, o_ref, hacc_ref, *state,
                    heads, nblk, dh):
    s = pl.program_id(2)
    L = qf_ref.shape[0]
    per_step = len(state) // 6

    @pl.when(s == 0)
    def _():
        for ref in (hacc_ref,) + state:
            ref[...] = jnp.zeros_like(ref)

    rows_f = pl.ds(pl.multiple_of(s * L, L), L)
    rows_b = pl.ds(pl.multiple_of(_bwd_block(s, nblk) * L, L), L)
    gf, gb = gf_ref[...], gb_ref[...]
    for i in range(per_step):
        h = pl.program_id(1) * per_step + i
        cols = slice(i * dh, (i + 1) * dh)
        ctf, nf, mf, ctb, nb, mb = state[6 * i:6 * i + 6]
        hacc_ref[rows_f, cols] += _ml_chunk(0, qf_ref[:, cols], kf_ref[:, cols], vf_ref[:, cols], gf,
                                            ctf, nf, mf, h, h + heads)
        hacc_ref[rows_b, cols] += _ml_chunk(1, qb_ref[:, cols], kb_ref[:, cols], vb_ref[:, cols], gb,
                                            ctb, nb, mb, 2 * heads + h, 3 * heads + h)

    @pl.when(s == nblk - 1)
    def _():
        for j in range(nblk):
            for i in range(per_step):
                tot = hacc_ref[j * L:(j + 1) * L, i * dh:(i + 1) * dh]
                cen = tot - jnp.mean(tot, axis=-1, keepdims=True)
                var = jnp.mean(cen * cen, axis=-1, keepdims=True)
                o_ref[j * L:(j + 1) * L, i * dh:(i + 1) * dh] = (cen * lax.rsqrt(var + EPS)).astype(BF16)


def _ml_core(q, k, v, gates, dh):
    bsz, t, inner = q.shape
    heads = inner // dh
    nblk = t // ROW_BLOCK
    per_step = ML_HEADS_PER_STEP
    assert heads % per_step == 0
    width = per_step * dh
    fwd = pl.BlockSpec((None, ROW_BLOCK, width), lambda b, h, s: (b, s, h))
    bwd = pl.BlockSpec((None, ROW_BLOCK, width), lambda b, h, s: (b, _bwd_block(s, nblk), h))
    gate_f = pl.BlockSpec((None, ROW_BLOCK, LANE), lambda b, h, s: (b, s, 0))
    gate_b = pl.BlockSpec((None, ROW_BLOCK, LANE), lambda b, h, s: (b, _bwd_block(s, nblk), 0))
    state = [pltpu.VMEM((dh, dh), F32), pltpu.VMEM((1, dh), F32), pltpu.VMEM((1, 1), F32)]
    kern = functools.partial(_ml_core_kernel, heads=heads, nblk=nblk, dh=dh)
    return pl.pallas_call(
        kern,
        grid=(bsz, heads // per_step, nblk),
        in_specs=[fwd, fwd, fwd, gate_f, bwd, bwd, bwd, gate_b],
        out_specs=pl.BlockSpec((None, t, width), lambda b, h, s: (b, 0, h)),
        out_shape=jax.ShapeDtypeStruct((bsz, t, inner), BF16),
        scratch_shapes=[pltpu.VMEM((t, width), F32)] + state * (2 * per_step),
        compiler_params=_params(("arbitrary",) * 3),
        name="ml_core",
    )(q, k, v, gates, q, k, v, gates)


def _ml_out_kernel(hn_ref, uc_ref, sz_ref, *refs, nres):
    res, (mod_ref, hg_ref, sk_ref, w_ref, o_ref) = refs[:nres], refs[nres:]
    y = (hn_ref[...].astype(F32) * hg_ref[...] + sk_ref[...] * uc_ref[...].astype(F32)) * sz_ref[...].astype(F32)
    o_ref[...] = _residual_block(res) + mod_ref[2:3, :] * _dot(y.astype(BF16), w_ref[...])


def _ml_out(rows, hn, uc, sz, r, mods, hnorm_g, skip, w_out):
    bsz, t, d = _residual_shape(r)
    inner = hn.shape[-1]
    res_specs, res_args = rows.residual(r)
    return pl.pallas_call(
        functools.partial(_ml_out_kernel, nres=len(res_args)),
        grid=rows.grid,
        in_specs=[rows.rows(inner), rows.rows(inner), rows.rows(inner)] + res_specs
        + [rows.mod(d), _const2((1, inner)), _const2((1, inner)), _const2((inner, d))],
        out_specs=rows.rows(d),
        out_shape=jax.ShapeDtypeStruct((bsz, t, d), F32),
        compiler_params=_params(("arbitrary", "arbitrary")),
        name="ml_out",
    )(hn, uc, sz, *res_args, mods, hnorm_g, skip, w_out)


def _na_attn_kernel(q_ref, k_ref, v_ref, *rest, first, ctx_len, rows, groups):
    b_refs = rest[:NA_ROWS_PER_STEP]
    r_ref, mod_ref, w_ref, bo_ref, o_ref, a_buf = rest[NA_ROWS_PER_STEP:]
    gw = GRID_W
    per = NA_GROUP // NA_HEAD_DIM
    nloc = NA_KH * gw
    lane = lax.broadcasted_iota(jnp.int32, (gw, NA_GROUP), 1)
    mine = [(lane >= h * NA_HEAD_DIM) & (lane < (h + 1) * NA_HEAD_DIM) for h in range(per)]

    def key_start(sub):
        t = (pl.program_id(1) + first) * NA_ROWS_PER_STEP + sub
        r = t - ctx_len // gw
        rs = jnp.clip(r - NA_KH // 2, 0, rows - NA_KH)
        return pl.multiple_of(ctx_len + rs * gw, gw)

    kstart = [key_start(sub) for sub in range(NA_ROWS_PER_STEP)]

    def scores(sub, g):
        qg = q_ref[g, sub * gw:(sub + 1) * gw, :].astype(F32)
        qm = jnp.concatenate([jnp.where(mine[h], qg, 0.0) for h in range(per)], axis=0).astype(BF16)
        return _dot_nt(qm, k_ref[g, pl.ds(kstart[sub], nloc), :]), _dot_nt(qm, k_ref[g, 0:ctx_len, :])

    work = [(sub, g) for sub in range(NA_ROWS_PER_STEP) for g in range(groups)]
    ahead = scores(*work[0])
    for i, (sub, g) in enumerate(work):
        s_loc, s_ctx = ahead
        if i + 1 < len(work):
            ahead = scores(*work[i + 1])
        s_loc = s_loc + b_refs[sub][per * g:per * (g + 1)].reshape(per * gw, nloc)
        m = jnp.maximum(jnp.max(s_loc, axis=1, keepdims=True), jnp.max(s_ctx, axis=1, keepdims=True))
        p_loc = jnp.exp2(s_loc - m)
        p_ctx = jnp.exp2(s_ctx - m)
        den = jnp.sum(p_loc, axis=1, keepdims=True) + jnp.sum(p_ctx, axis=1, keepdims=True)
        o = (_dot(p_loc.astype(BF16), v_ref[g, pl.ds(kstart[sub], nloc), :])
             + _dot(p_ctx.astype(BF16), v_ref[g, 0:ctx_len, :])) * (1.0 / den)
        og = jnp.where(mine[0], o[0:gw], 0.0)
        for h in range(1, per):
            og = jnp.where(mine[h], o[h * gw:(h + 1) * gw], og)
        a_buf[sub * gw:(sub + 1) * gw, g * NA_GROUP:(g + 1) * NA_GROUP] = og.astype(BF16)

    o_ref[...] = r_ref[...] + mod_ref[2:3, :] * (_dot(a_buf[...], w_ref[...]) + bo_ref[...])


def _na_attn(live_rows, q, k, v, bias, r, mods, w_out, b_out, ctx_len):
    bsz, groups, t, _ = q.shape
    d = groups * NA_GROUP
    rows = (t - ctx_len) // GRID_W
    blk = NA_ROWS_PER_STEP * GRID_W
    assert blk == ROW_BLOCK and ctx_len % blk == 0 and t % blk == 0
    first = live_rows.first
    steps = t // blk - first
    heads, nvar = bias.shape[:2]

    def variant(tq):
        r = tq - ctx_len // GRID_W
        rs = jnp.clip(r - NA_KH // 2, 0, rows - NA_KH)
        return jnp.where(r < 0, nvar - 1, rs - r + NA_KH - 1)

    def bias_spec(sub):
        return pl.BlockSpec((heads, None) + bias.shape[2:],
                            lambda b, j: (0, variant((j + first) * NA_ROWS_PER_STEP + sub), 0, 0))

    whole = pl.BlockSpec((None, groups, t, NA_GROUP), lambda b, j: (b, 0, 0, 0))
    kern = functools.partial(_na_attn_kernel, first=first, ctx_len=ctx_len, rows=rows, groups=groups)
    return pl.pallas_call(
        kern,
        grid=(bsz, steps),
        in_specs=[pl.BlockSpec((None, groups, blk, NA_GROUP), lambda b, j: (b, 0, j + first, 0)), whole, whole]
        + [bias_spec(sub) for sub in range(NA_ROWS_PER_STEP)]
        + [live_rows.rows(d), live_rows.mod(d), _const2((d, d)), _const2((1, d))],
        out_specs=live_rows.rows(d),
        out_shape=jax.ShapeDtypeStruct((bsz, t, d), F32),
        scratch_shapes=[pltpu.VMEM((blk, d), BF16)],
        compiler_params=_params(("arbitrary", "arbitrary")),
        name="na_attn",
    )(q, k, v, *([bias] * NA_ROWS_PER_STEP), r, mods, w_out, b_out)


def _na_bias_table(rpb):
    heads, nrow, ncol = rpb.shape
    col = np.arange(GRID_W)
    col_start = np.clip(col - NA_KW // 2, 0, GRID_W - NA_KW)
    col_ok = (col[None, :] >= col_start[:, None]) & (col[None, :] < col_start[:, None] + NA_KW)
    w = GRID_W
    left = w - NA_KW
    padded = jnp.pad(rpb.astype(F32), ((0, 0), (0, 0), (left, 2 * w - 1 - left - ncol)))
    toep = jnp.tile(padded, (1, 1, w))[:, :, w - 1:w - 1 + w * (2 * w - 2)]
    toep = toep.reshape(heads, nrow, w, 2 * w - 2)[:, :, :, :w]
    full = jnp.where(jnp.asarray(col_ok[None, :, None, :]), toep.transpose(0, 2, 1, 3), -jnp.inf)
    nloc = NA_KH * GRID_W
    var = [full[:, :, o:o + NA_KH, :].reshape(heads, GRID_W, nloc) for o in range(NA_KH)]
    var.append(jnp.full((heads, GRID_W, nloc), -jnp.inf, F32))
    return jnp.stack(var, axis=1)


def _ffn_kernel(x_ref, xp_ref, xn_ref, mod_ref, g_ref, wu_ref, bu_ref, cw_ref, cb_ref, wd_ref, bd_ref,
                *rest, latent, final, merged):
    rest = list(rest)
    fg_ref = rest.pop(0) if final else None
    if merged:
        rest.pop(0)
    o_ref, h_buf, acc_ref, *u_bufs = rest
    rows = x_ref.shape[0]
    x = x_ref[...]
    ext = jnp.concatenate([xp_ref[...], x, xn_ref[...]], axis=0)
    h_buf[...] = _norm_mod(ext, g_ref[...], mod_ref[3:4, :], mod_ref[4:5, :]).astype(BF16)
    if latent:
        has_prev = pl.program_id(1) > 0
        has_next = pl.program_id(1) < pl.num_programs(1) - 1
    else:
        has_prev = has_next = False
    nchunk = wd_ref.shape[0]
    taps = cw_ref.shape[1]
    acc_ref[...] = jnp.zeros_like(acc_ref)

    def up(c, slot):
        for half in range(2):
            cc = half * nchunk + c
            raw = _dot(h_buf[...], wu_ref[cc])
            buf = u_bufs[2 * slot + half]
            buf[HALO:HALO + rows, :] = raw[HALO:HALO + rows, :]
            buf[0:HALO, :] = jnp.where(has_prev, raw[0:HALO, :], -bu_ref[cc])
            buf[HALO + rows:, :] = jnp.where(has_next, raw[HALO + rows:, :], -bu_ref[cc])

    def down(c, slot):
        halves = []
        for half in range(2):
            cc = half * nchunk + c
            y = cb_ref[cc] + bu_ref[cc] * jnp.sum(cw_ref[cc], axis=0, keepdims=True)
            u = u_bufs[2 * slot + half][...]
            for t in range(taps):
                shifted = u if t == taps // 2 else pltpu.roll(u, (taps // 2 - t) % u.shape[0], axis=0)
                y = y + cw_ref[cc, t:t + 1, :] * shifted[HALO:HALO + rows, :]
            halves.append(y)
        act = (halves[0] * _silu(halves[1])).astype(BF16)
        acc_ref[...] += _dot(act, wd_ref[c])

    def stage(c, slot):
        up(c + 1, 1 - slot)
        down(c, slot)

    unroll = FF_UNROLL
    assert unroll % 2 == 0
    up(0, 0)

    def body(i, carry):
        for k in range(unroll):
            stage(unroll * i + k, k % 2)
        return carry

    looped = (nchunk - 1) // unroll
    lax.fori_loop(0, looped, body, 0)
    for c in range(looped * unroll, nchunk - 1):
        stage(c, c % 2)
    down(nchunk - 1, (nchunk - 1) % 2)
    o = x + mod_ref[5:6, :] * (acc_ref[...] + bd_ref[...])
    if final:
        o = o * lax.rsqrt(jnp.mean(o * o, axis=-1, keepdims=True) + EPS) * fg_ref[...]
    o_ref[...] = o


def _ffn_call(r, mods, layer, ctx_len, weights, latent, final_g=None, merge_into=None):
    bsz, t, d = r.shape
    final = final_g is not None
    merged = merge_into is not None
    if latent:
        rows = FF_ROWS
        assert (t - ctx_len) % rows == 0
        grid = (bsz, (t - ctx_len) // rows)
        start = lambda j: ctx_len + rows * j
        tile = lambda i: pl.multiple_of(i, HALO)
        cur = pl.BlockSpec((None, pl.Element(rows), pl.Element(d)), lambda b, j: (b, tile(start(j)), 0))
        prev = pl.BlockSpec((None, pl.Element(HALO), pl.Element(d)),
                            lambda b, j: (b, tile(jnp.maximum(start(j) - HALO, ctx_len)), 0))
        nxt = pl.BlockSpec((None, pl.Element(HALO), pl.Element(d)),
                           lambda b, j: (b, tile(jnp.minimum(start(j + 1), t - HALO)), 0))
        mod = pl.BlockSpec((None, None, 6, d), lambda b, j: (layer, b, 0, 0))
    else:
        rows = ctx_len
        grid = (bsz, 1)
        cur = pl.BlockSpec((None, rows, d), lambda b, j: (b, 0, 0))
        prev = nxt = pl.BlockSpec((None, HALO, d), lambda b, j: (b, 0, 0))
        mod = pl.BlockSpec((None, None, 6, d), lambda b, j: (layer, bsz, 0, 0))
    in_specs = [cur, prev, nxt, mod] + [_layer_const(w, layer) for w in weights]
    args = [r, r, r, mods, *weights]
    if final:
        in_specs.append(_const2((1, d)))
        args.append(final_g)
    aliases = {}
    if merged:
        aliases = {len(args): 0}
        in_specs.append(pl.BlockSpec(memory_space=pl.ANY))
        args.append(merge_into)
    if final:
        assert latent
        out_spec = pl.BlockSpec((None, rows, d), lambda b, j: (b, j, 0))
        out_shape = jax.ShapeDtypeStruct((bsz, t - ctx_len, d), F32)
    else:
        out_spec = cur
        out_shape = jax.ShapeDtypeStruct((bsz, t, d), F32)
    kern = functools.partial(_ffn_kernel, latent=latent, final=final, merged=merged)
    return pl.pallas_call(
        kern,
        grid=grid,
        in_specs=in_specs,
        out_specs=out_spec,
        out_shape=out_shape,
        input_output_aliases=aliases,
        scratch_shapes=[pltpu.VMEM((rows + 2 * HALO, d), BF16), pltpu.VMEM((rows, d), F32)]
        + [pltpu.VMEM((rows + 2 * HALO, FF_CHUNK), F32)] * 4,
        compiler_params=_params(("arbitrary", "arbitrary")),
        name="ffn_latent" if latent else "ffn_context",
    )(*args)


def _ffn_weights(g, w_up, b_up, conv_w, conv_b, w_down, b_down):
    layers, d, hidden2 = w_up.shape
    nc2 = hidden2 // FF_CHUNK
    taps = conv_w.shape[1]
    return (g.reshape(layers, 1, d),
            w_up.astype(BF16).reshape(layers, d, nc2, FF_CHUNK).transpose(0, 2, 1, 3),
            b_up.reshape(layers, nc2, 1, FF_CHUNK),
            conv_w.reshape(layers, taps, nc2, FF_CHUNK).transpose(0, 2, 1, 3),
            conv_b.reshape(layers, nc2, 1, FF_CHUNK),
            w_down.astype(BF16).reshape(layers, nc2 // 2, FF_CHUNK, d),
            b_down.reshape(layers, 1, d))


def _block_diag_tiles(w):
    nb, bs, _ = w.shape
    per = BD_TILE // bs
    rep = jnp.tile(w.reshape(nb * bs, bs), (1, per)).reshape(nb // per, BD_TILE, BD_TILE)
    pos = np.arange(BD_TILE) // bs
    return jnp.where(jnp.asarray(pos[:, None] == pos[None, :]), rep, 0.0).astype(BF16)


def _gate_cols(gate_w, gate_b, inner):
    dirs, _, ng = gate_w.shape
    g = gate_w.transpose(1, 0, 2).reshape(3 * inner, dirs * ng)
    g = jnp.pad(g, ((0, 0), (0, LANE - dirs * ng))).astype(BF16)
    gb = jnp.pad(gate_b.reshape(1, dirs * ng), ((0, 0), (0, LANE - dirs * ng)))
    return g[:inner], g[inner:2 * inner], g[2 * inner:], gb


def kernel(x, c, ctx, c_ctx, ada_w, ada_b, norm1_g, norm2_g, final_g, ml_w_in, ml_conv_w, ml_conv_b, ml_wq, ml_wk, ml_wv, ml_gate_w, ml_gate_b, ml_skip, ml_hnorm_g, ml_w_out, na_w_qkv, na_b_qkv, na_rpb, na_w_out, na_b_out, ff_w_up, ff_b_up, ff_conv_w, ff_conv_b, ff_w_down, ff_b_down):
    bsz, seq, d = x.shape
    ctx_len = ctx.shape[1]
    depth = ada_w.shape[0]
    assert ctx_len == ROW_BLOCK and seq % ROW_BLOCK == 0 and seq % GRID_W == 0
    assert bsz + 1 <= MOD_ROWS
    t = ctx_len + seq
    nblk = t // ROW_BLOCK

    c_all = jnp.zeros((MOD_ROWS, d), F32).at[:bsz].set(c).at[bsz].set(c_ctx)
    mods = _ada(c_all, ada_w, ada_b).reshape(depth, MOD_ROWS, 6, d)

    r = (ctx, x)
    row2 = lambda a: a.reshape(1, -1)
    ffn_weights = _ffn_weights(norm2_g, ff_w_up, ff_b_up, ff_conv_w, ff_conv_b, ff_w_down, ff_b_down)
    out = None
    for i in range(depth):
        need_ctx = i < depth - 1
        j = i // 2
        all_rows = _Rows(bsz, nblk, i, 0)
        live_rows = all_rows if need_ctx else _Rows(bsz, nblk, i, 1)
        if i % 2 == 0:
            inner = ml_w_in.shape[2] // 2
            dh = inner // ML_HEADS
            u, sz = _ml_in(all_rows, r, mods, row2(norm1_g[i]), ml_w_in[j].astype(BF16))
            gq, gk, gv, gb = _gate_cols(ml_gate_w[j], ml_gate_b[j], inner)
            uc, q, k, v, gates = _ml_proj(u, ml_conv_w[j], row2(ml_conv_b[j]), _block_diag_tiles(ml_wq[j]),
                                          _block_diag_tiles(ml_wk[j]), _block_diag_tiles(ml_wv[j]),
                                          gq, gk, gv, gb, dh, dh ** -0.5,
                                          ml_gate_w.shape[1] * ml_gate_w.shape[3])
            hn = _ml_core(q, k, v, gates, dh)
            r = _ml_out(live_rows, hn, uc, sz, r, mods, row2(ml_hnorm_g[j]), row2(ml_skip[j]),
                        ml_w_out[j].astype(BF16))
        else:
            q, k, v = _na_qkv(all_rows, r, mods, row2(norm1_g[i]), na_w_qkv[j].astype(BF16), row2(na_b_qkv[j]),
                              NA_HEAD_DIM ** -0.5 * LOG2E)
            r = _na_attn(live_rows, q, k, v, _na_bias_table(na_rpb[j] * LOG2E), r, mods,
                         na_w_out[j].astype(BF16), row2(na_b_out[j]), ctx_len)
        if need_ctx:
            r_new = _ffn_call(r, mods, i, ctx_len, ffn_weights, latent=True)
            r = _ffn_call(r, mods, i, ctx_len, ffn_weights, latent=False, merge_into=r_new)
        else:
            out = _ffn_call(r, mods, i, ctx_len, ffn_weights, latent=True, final_g=row2(final_g))
    return out
```

```python
import functools

import jax
import jax.numpy as jnp
import numpy as np
from jax import lax
from jax.experimental import pallas as pl
from jax.experimental.pallas import tpu as pltpu

F32 = jnp.float32
BF16 = jnp.bfloat16
EPS = 1e-6
LOG2E = 1.4426950408889634

GRID_W = 64
ML_HEADS = 4
ML_QKV_BLOCK = 4
NA_HEAD_DIM = 64
NA_KH = 8
NA_KW = 16

ROW_BLOCK = 256
HALO = 8
BD_TILE = 256
LANE = 128
NA_GROUP = 256
NA_ROWS_PER_STEP = 4
ML_HEADS_PER_STEP = 2
FF_CHUNK = 256
FF_UNROLL = 4
FF_ROWS = 512
MOD_ROWS = 16
VMEM_LIMIT = 56 * 1024 * 1024


def _dot(a, b):
    return jnp.dot(a, b, preferred_element_type=F32)


def _dot_nt(a, b):
    return lax.dot_general(a, b, (((1,), (1,)), ((), ())), preferred_element_type=F32)


def _dot_tn(a, b):
    return lax.dot_general(a, b, (((0,), (0,)), ((), ())), preferred_element_type=F32)


def _silu(x):
    h = 0.5 * x
    return h + h * jnp.tanh(h)


def _log_sigmoid(x):
    return jnp.minimum(x, 0.0) - jnp.log1p(jnp.exp(-jnp.abs(x)))


def _norm_mod(x, g, shift, scale):
    y = x * lax.rsqrt(jnp.mean(x * x, axis=-1, keepdims=True) + EPS) * g
    return y * (1.0 + scale) + shift


def _params(sem):
    return pltpu.CompilerParams(dimension_semantics=sem, vmem_limit_bytes=VMEM_LIMIT)


def _resident(shape, index_map):
    return pl.BlockSpec(shape, index_map, pipeline_mode=pl.Buffered(1))


def _ada_kernel(c_ref, w_ref, b_ref, o_ref):
    s = _silu(c_ref[...]).astype(BF16)
    o_ref[...] = _dot(s, w_ref[...].astype(BF16)) + b_ref[...]


def _ada(c_all, ada_w, ada_b):
    depth, d, d6 = ada_w.shape
    n = d6 // d
    return pl.pallas_call(
        _ada_kernel,
        grid=(depth, n),
        in_specs=[
            pl.BlockSpec((MOD_ROWS, d), lambda l, j: (0, 0)),
            pl.BlockSpec((None, d, d), lambda l, j: (l, 0, j)),
            pl.BlockSpec((None, 1, d), lambda l, j: (l, 0, j)),
        ],
        out_specs=pl.BlockSpec((None, MOD_ROWS, d), lambda l, j: (l, 0, j)),
        out_shape=jax.ShapeDtypeStruct((depth, MOD_ROWS, d6), F32),
        compiler_params=_params(("arbitrary", "arbitrary")),
        name="ada",
    )(c_all, ada_w, ada_b.reshape(depth, 1, d6))


class _Rows:
    def __init__(self, bsz, nblk_total, layer, first_blk):
        self.bsz = bsz
        self.nblk_total = nblk_total
        self.layer = layer
        self.first = first_blk
        self.grid = (bsz, nblk_total - first_blk)

    def rows(self, width, rows=ROW_BLOCK):
        f = self.first
        return pl.BlockSpec((None, rows, width), lambda b, j: (b, j + f, 0))

    def residual(self, r):
        if not isinstance(r, tuple):
            return [self.rows(r.shape[-1])], [r]
        assert self.first == 0
        d = r[1].shape[-1]
        return ([pl.BlockSpec((None, ROW_BLOCK, d), lambda b, j: (b, 0, 0)),
                 pl.BlockSpec((None, ROW_BLOCK, d), lambda b, j: (b, jnp.maximum(j - 1, 0), 0))], list(r))

    def mod(self, d):
        f, bsz, layer = self.first, self.bsz, self.layer
        return pl.BlockSpec((None, None, 6, d), lambda b, j: (layer, jnp.where(j + f == 0, bsz, b), 0, 0))


def _const2(shape):
    return _resident(shape, lambda b, j: (0,) * len(shape))


def _layer_const(stacked, layer):
    shape = stacked.shape[1:]
    return _resident((None,) + shape, lambda b, j: (layer,) + (0,) * len(shape))


def _residual_block(refs):
    if len(refs) == 1:
        return refs[0][...]
    return jnp.where(pl.program_id(1) == 0, refs[0][...], refs[1][...])


def _residual_shape(r):
    if isinstance(r, tuple):
        return r[1].shape[0], r[0].shape[1] + r[1].shape[1], r[1].shape[2]
    return r.shape


def _proj_kernel(*refs, nres, shift, scale, chunk, splits, has_bias, out_scale, silu_out=()):
    res, (mod_ref, g_ref, w_ref), rest = refs[:nres], refs[nres:nres + 3], refs[nres + 3:]
    if has_bias:
        b_ref, out_refs = rest[0], rest[1:]
    else:
        b_ref, out_refs = None, rest
    h = _norm_mod(_residual_block(res), g_ref[...], mod_ref[shift:shift + 1, :],
                  mod_ref[scale:scale + 1, :]).astype(BF16)
    n = w_ref.shape[1]
    per_out = n // len(out_refs)
    for c in range(n // chunk):
        y = _dot(h, w_ref[:, c * chunk:(c + 1) * chunk])
        if has_bias:
            y = y + b_ref[:, c * chunk:(c + 1) * chunk]
        oi, off = divmod(c * chunk, per_out)
        if out_scale[oi] != 1.0:
            y = y * out_scale[oi]
        if oi in silu_out:
            y = _silu(y)
        y = y.astype(BF16)
        o_ref = out_refs[oi]
        if splits:
            for t in range(chunk // splits):
                o_ref[(off + t * splits) // splits] = y[:, t * splits:(t + 1) * splits]
        else:
            o_ref[:, off:off + chunk] = y


def _ml_in(rows, r, mods, g, w_in):
    bsz, t, d = _residual_shape(r)
    n = w_in.shape[1]
    half = n // 2
    res_specs, res_args = rows.residual(r)
    kern = functools.partial(_proj_kernel, nres=len(res_args), shift=0, scale=1, chunk=512, splits=0, has_bias=False,
                             out_scale=(1.0, 1.0), silu_out=(1,))
    return pl.pallas_call(
        kern,
        grid=rows.grid,
        in_specs=res_specs + [rows.mod(d), _const2((1, d)), _const2((d, n))],
        out_specs=[rows.rows(half), rows.rows(half)],
        out_shape=[jax.ShapeDtypeStruct((bsz, t, half), BF16)] * 2,
        compiler_params=_params(("arbitrary", "arbitrary")),
        name="ml_in",
    )(*res_args, mods, g, w_in)


def _na_qkv(rows, r, mods, g, w, b, q_scale):
    bsz, t, d = r.shape
    groups = d // NA_GROUP
    f = rows.first
    out_spec = pl.BlockSpec((None, groups, ROW_BLOCK, NA_GROUP), lambda bb, j: (bb, 0, j + f, 0))
    kern = functools.partial(_proj_kernel, nres=1, shift=0, scale=1, chunk=NA_GROUP, splits=NA_GROUP, has_bias=True,
                             out_scale=(q_scale, 1.0, 1.0))
    return pl.pallas_call(
        kern,
        grid=rows.grid,
        in_specs=[rows.rows(d), rows.mod(d), _const2((1, d)), _const2((d, 3 * d)), _const2((1, 3 * d))],
        out_specs=[out_spec] * 3,
        out_shape=[jax.ShapeDtypeStruct((bsz, groups, t, NA_GROUP), BF16)] * 3,
        compiler_params=_params(("arbitrary", "arbitrary")),
        name="na_qkv",
    )(r, mods, g, w, b)


def _ml_proj_kernel(u_ref, cw_ref, cb_ref, wq_ref, wk_ref, wv_ref, gq_ref, gk_ref, gv_ref, gb_ref,
                    uc_ref, q_ref, k_ref, v_ref, g_ref, *, nblk, kscale, ngates):
    @pl.when(pl.program_id(1) == 0)
    def _():
        g_ref[...] = jnp.broadcast_to(gb_ref[...], g_ref.shape)

    cw = cw_ref[...]
    cb = cb_ref[...]
    taps = cw.shape[0]
    width = u_ref.shape[1]
    pad = 2 * HALO
    for j in range(nblk):
        r0 = j * ROW_BLOCK
        cur_b = u_ref[r0:r0 + ROW_BLOCK, :]
        cur = cur_b.astype(F32)
        if j in (0, 1):
            prev = jnp.zeros((HALO, width), F32)
        else:
            prev = u_ref[r0 - pad:r0, :].astype(F32)[HALO:, :]
        if j in (0, nblk - 1):
            nxt = jnp.zeros((HALO, width), F32)
        else:
            nxt = u_ref[r0 + ROW_BLOCK:r0 + ROW_BLOCK + pad, :].astype(F32)[:HALO, :]
        ext = jnp.concatenate([prev, cur, nxt], axis=0)
        acc = cb
        for t in range(taps):
            shifted = ext if t == taps // 2 else pltpu.roll(ext, (taps // 2 - t) % ext.shape[0], axis=0)
            acc = acc + cw[t:t + 1, :] * shifted[HALO:HALO + ROW_BLOCK, :]
        uc_b = _silu(acc).astype(BF16)
        uc_ref[r0:r0 + ROW_BLOCK, :] = uc_b
        qs, ks, vs = [], [], []
        for t in range(width // BD_TILE):
            sl = slice(t * BD_TILE, (t + 1) * BD_TILE)
            qs.append(_dot(uc_b[:, sl], wq_ref[t]))
            ks.append(_dot(uc_b[:, sl], wk_ref[t]) * kscale)
            vs.append(_dot(cur_b[:, sl], wv_ref[t]))
        q_b = jnp.concatenate(qs, axis=1).astype(BF16)
        k_b = jnp.concatenate(ks, axis=1).astype(BF16)
        v_b = jnp.concatenate(vs, axis=1).astype(BF16)
        q_ref[r0:r0 + ROW_BLOCK, :] = q_b
        k_ref[r0:r0 + ROW_BLOCK, :] = k_b
        v_ref[r0:r0 + ROW_BLOCK, :] = v_b
        g_ref[r0:r0 + ROW_BLOCK, :] += _dot(q_b, gq_ref[...]) + _dot(k_b, gk_ref[...]) + _dot(v_b, gv_ref[...])

    @pl.when(pl.program_id(1) == pl.num_programs(1) - 1)
    def _():
        lane = lax.broadcasted_iota(jnp.int32, (ROW_BLOCK, g_ref.shape[1]), 1)
        forget = (lane < ngates) & (lane % (ngates // 2) >= ngates // 4)
        for j in range(nblk):
            pre = g_ref[j * ROW_BLOCK:(j + 1) * ROW_BLOCK, :]
            g_ref[j * ROW_BLOCK:(j + 1) * ROW_BLOCK, :] = jnp.where(forget, _log_sigmoid(pre), pre) * LOG2E


def _ml_proj(u, conv_w, conv_b, wq_bd, wk_bd, wv_bd, gq, gk, gv, gb, dh, kscale, ngates):
    bsz, t, inner = u.shape
    heads = inner // dh
    tiles = dh // BD_TILE
    taps = conv_w.shape[0]
    col = lambda shape: pl.BlockSpec(shape, lambda b, h: (0, h))
    seq = pl.BlockSpec((None, t, dh), lambda b, h: (b, 0, h))
    bd = pl.BlockSpec((tiles, BD_TILE, BD_TILE), lambda b, h: (h, 0, 0))
    gw = pl.BlockSpec((dh, LANE), lambda b, h: (h, 0))
    kern = functools.partial(_ml_proj_kernel, nblk=t // ROW_BLOCK, kscale=kscale, ngates=ngates)
    return pl.pallas_call(
        kern,
        grid=(bsz, heads),
        in_specs=[seq, col((taps, dh)), col((1, dh)), bd, bd, bd, gw, gw, gw,
                  pl.BlockSpec((1, LANE), lambda b, h: (0, 0))],
        out_specs=[seq, seq, seq, seq, pl.BlockSpec((None, t, LANE), lambda b, h: (b, 0, 0))],
        out_shape=[jax.ShapeDtypeStruct((bsz, t, inner), BF16)] * 4 + [jax.ShapeDtypeStruct((bsz, t, LANE), F32)],
        compiler_params=_params(("arbitrary", "arbitrary")),
        name="ml_proj",
    )(u, conv_w, conv_b, wq_bd, wk_bd, wv_bd, gq, gk, gv, gb)


def _ml_chunk(d, q, k, v, g, ct_ref, n_ref, m_ref, ci, cf):
    L = q.shape[0]
    gt = g.T
    lane = lax.broadcasted_iota(jnp.int32, g.shape, 1)
    sub = lax.broadcasted_iota(jnp.int32, gt.shape, 0)
    li_col = jnp.sum(jnp.where(lane == ci, g, 0.0), axis=1, keepdims=True)
    lf_col = jnp.sum(jnp.where(lane == cf, g, 0.0), axis=1, keepdims=True)
    li_row = jnp.sum(jnp.where(sub == ci, gt, 0.0), axis=0, keepdims=True)
    lf_row = jnp.sum(jnp.where(sub == cf, gt, 0.0), axis=0, keepdims=True)

    tt = lax.broadcasted_iota(jnp.int32, (L, L), 0)
    ss = lax.broadcasted_iota(jnp.int32, (L, L), 1)
    seen = (ss >= tt) if d else (ss <= tt)
    seen_t = (tt >= ss) if d else (tt <= ss)
    b_col = jnp.sum(jnp.where(seen, lf_row, 0.0), axis=1, keepdims=True)
    b_row = jnp.sum(jnp.where(seen_t, lf_col, 0.0), axis=0, keepdims=True)
    b_last = jnp.sum(lf_row, axis=1, keepdims=True)

    m_prev = m_ref[...]
    r_row = li_row - b_row
    m_t = b_col + jnp.maximum(m_prev, jnp.max(jnp.where(seen, r_row, -jnp.inf), axis=1, keepdims=True))
    sc = _dot_nt(q, k) * jnp.exp2(jnp.where(seen, (b_col - m_t) + r_row, -jnp.inf))
    w_inter = jnp.exp2(b_col + m_prev - m_t)
    n_rows = jnp.broadcast_to(n_ref[...], (2 * HALO, n_ref.shape[1])).astype(BF16)
    qn = _dot_nt(q, n_rows)[:, :1]
    num = _dot(sc.astype(BF16), v) + w_inter * _dot(q, ct_ref[...].astype(BF16))
    den = jnp.sum(sc, axis=1, keepdims=True) + w_inter * qn
    hh = num * (1.0 / jnp.maximum(jnp.abs(den), jnp.exp2(-m_t)))

    g_row = b_last + r_row
    g_col = b_last - b_col + li_col
    m_new = jnp.maximum(b_last + m_prev, jnp.max(g_row, axis=1, keepdims=True))
    decay = jnp.exp2(b_last + m_prev - m_new)
    vw = (v.astype(F32) * jnp.exp2(g_col - m_new)).astype(BF16)
    wk_rows = jnp.broadcast_to(jnp.exp2(g_row - m_new), (2 * HALO, L)).astype(BF16)
    ct_ref[...] = decay * ct_ref[...] + _dot_tn(k, vw)
    n_ref[...] = decay * n_ref[...] + _dot(wk_rows, k)[:1, :]
    m_ref[...] = m_new
    return hh


def _bwd_block(s, nblk):
    return jnp.where(s == 0, 0, nblk - s)


def _ml_core_kernel(qf_ref, kf_ref, vf_ref, gf_ref, qb_ref, kb_ref, vb_ref, gb_ref, o_ref, hacc_ref, *state,
                    heads, nblk, dh):
    s = pl.program_id(2)
    L = qf_ref.shape[0]
    per_step = len(state) // 6

    @pl.when(s == 0)
    def _():
        for ref in (hacc_ref,) + state:
            ref[...] = jnp.zeros_like(ref)

    rows_f = pl.ds(pl.multiple_of(s * L, L), L)
    rows_b = pl.ds(pl.multiple_of(_bwd_block(s, nblk) * L, L), L)
    gf, gb = gf_ref[...], gb_ref[...]
    for i in range(per_step):
        h = pl.program_id(1) * per_step + i
        cols = slice(i * dh, (i + 1) * dh)
        ctf, nf, mf, ctb, nb, mb = state[6 * i:6 * i + 6]
        hacc_ref[rows_f, cols] += _ml_chunk(0, qf_ref[:, cols], kf_ref[:, cols], vf_ref[:, cols], gf,
                                            ctf, nf, mf, h, h + heads)
        hacc_ref[rows_b, cols] += _ml_chunk(1, qb_ref[:, cols], kb_ref[:, cols], vb_ref[:, cols], gb,
                                            ctb, nb, mb, 2 * heads + h, 3 * heads + h)

    @pl.when(s == nblk - 1)
    def _():
        for j in range(nblk):
            for i in range(per_step):
                tot = hacc_ref[j * L:(j + 1) * L, i * dh:(i + 1) * dh]
                cen = tot - jnp.mean(tot, axis=-1, keepdims=True)
                var = jnp.mean(cen * cen, axis=-1, keepdims=True)
                o_ref[j * L:(j + 1) * L, i * dh:(i + 1) * dh] = (cen * lax.rsqrt(var + EPS)).astype(BF16)


def _ml_core(q, k, v, gates, dh):
    bsz, t, inner = q.shape
    heads = inner // dh
    nblk = t // ROW_BLOCK
    per_step = ML_HEADS_PER_STEP
    assert heads % per_step == 0
    width = per_step * dh
    fwd = pl.BlockSpec((None, ROW_BLOCK, width), lambda b, h, s: (b, s, h))
    bwd = pl.BlockSpec((None, ROW_BLOCK, width), lambda b, h, s: (b, _bwd_block(s, nblk), h))
    gate_f = pl.BlockSpec((None, ROW_BLOCK, LANE), lambda b, h, s: (b, s, 0))
    gate_b = pl.BlockSpec((None, ROW_BLOCK, LANE), lambda b, h, s: (b, _bwd_block(s, nblk), 0))
    state = [pltpu.VMEM((dh, dh), F32), pltpu.VMEM((1, dh), F32), pltpu.VMEM((1, 1), F32)]
    kern = functools.partial(_ml_core_kernel, heads=heads, nblk=nblk, dh=dh)
    return pl.pallas_call(
        kern,
        grid=(bsz, heads // per_step, nblk),
        in_specs=[fwd, fwd, fwd, gate_f, bwd, bwd, bwd, gate_b],
        out_specs=pl.BlockSpec((None, t, width), lambda b, h, s: (b, 0, h)),
        out_shape=jax.ShapeDtypeStruct((bsz, t, inner), BF16),
        scratch_shapes=[pltpu.VMEM((t, width), F32)] + state * (2 * per_step),
        compiler_params=_params(("arbitrary",) * 3),
        name="ml_core",
    )(q, k, v, gates, q, k, v, gates)


def _ml_out_kernel(hn_ref, uc_ref, sz_ref, *refs, nres):
    res, (mod_ref, hg_ref, sk_ref, w_ref, o_ref) = refs[:nres], refs[nres:]
    y = (hn_ref[...].astype(F32) * hg_ref[...] + sk_ref[...] * uc_ref[...].astype(F32)) * sz_ref[...].astype(F32)
    o_ref[...] = _residual_block(res) + mod_ref[2:3, :] * _dot(y.astype(BF16), w_ref[...])


def _ml_out(rows, hn, uc, sz, r, mods, hnorm_g, skip, w_out):
    bsz, t, d = _residual_shape(r)
    inner = hn.shape[-1]
    res_specs, res_args = rows.residual(r)
    return pl.pallas_call(
        functools.partial(_ml_out_kernel, nres=len(res_args)),
        grid=rows.grid,
        in_specs=[rows.rows(inner), rows.rows(inner), rows.rows(inner)] + res_specs
        + [rows.mod(d), _const2((1, inner)), _const2((1, inner)), _const2((inner, d))],
        out_specs=rows.rows(d),
        out_shape=jax.ShapeDtypeStruct((bsz, t, d), F32),
        compiler_params=_params(("arbitrary", "arbitrary")),
        name="ml_out",
    )(hn, uc, sz, *res_args, mods, hnorm_g, skip, w_out)


def _na_attn_kernel(q_ref, k_ref, v_ref, *rest, first, ctx_len, rows, groups):
    b_refs = rest[:NA_ROWS_PER_STEP]
    r_ref, mod_ref, w_ref, bo_ref, o_ref, a_buf = rest[NA_ROWS_PER_STEP:]
    gw = GRID_W
    per = NA_GROUP // NA_HEAD_DIM
    nloc = NA_KH * gw
    lane = lax.broadcasted_iota(jnp.int32, (gw, NA_GROUP), 1)
    mine = [(lane >= h * NA_HEAD_DIM) & (lane < (h + 1) * NA_HEAD_DIM) for h in range(per)]

    def key_start(sub):
        t = (pl.program_id(1) + first) * NA_ROWS_PER_STEP + sub
        r = t - ctx_len // gw
        rs = jnp.clip(r - NA_KH // 2, 0, rows - NA_KH)
        return pl.multiple_of(ctx_len + rs * gw, gw)

    kstart = [key_start(sub) for sub in range(NA_ROWS_PER_STEP)]

    def scores(sub, g):
        qg = q_ref[g, sub * gw:(sub + 1) * gw, :].astype(F32)
        qm = jnp.concatenate([jnp.where(mine[h], qg, 0.0) for h in range(per)], axis=0).astype(BF16)
        return _dot_nt(qm, k_ref[g, pl.ds(kstart[sub], nloc), :]), _dot_nt(qm, k_ref[g, 0:ctx_len, :])

    work = [(sub, g) for sub in range(NA_ROWS_PER_STEP) for g in range(groups)]
    ahead = scores(*work[0])
    for i, (sub, g) in enumerate(work):
        s_loc, s_ctx = ahead
        if i + 1 < len(work):
            ahead = scores(*work[i + 1])
        s_loc = s_loc + b_refs[sub][per * g:per * (g + 1)].reshape(per * gw, nloc)
        m = jnp.maximum(jnp.max(s_loc, axis=1, keepdims=True), jnp.max(s_ctx, axis=1, keepdims=True))
        p_loc = jnp.exp2(s_loc - m)
        p_ctx = jnp.exp2(s_ctx - m)
        den = jnp.sum(p_loc, axis=1, keepdims=True) + jnp.sum(p_ctx, axis=1, keepdims=True)
        o = (_dot(p_loc.astype(BF16), v_ref[g, pl.ds(kstart[sub], nloc), :])
             + _dot(p_ctx.astype(BF16), v_ref[g, 0:ctx_len, :])) * (1.0 / den)
        og = jnp.where(mine[0], o[0:gw], 0.0)
        for h in range(1, per):
            og = jnp.where(mine[h], o[h * gw:(h + 1) * gw], og)
        a_buf[sub * gw:(sub + 1) * gw, g * NA_GROUP:(g + 1) * NA_GROUP] = og.astype(BF16)

    o_ref[...] = r_ref[...] + mod_ref[2:3, :] * (_dot(a_buf[...], w_ref[...]) + bo_ref[...])


def _na_attn(live_rows, q, k, v, bias, r, mods, w_out, b_out, ctx_len):
    bsz, groups, t, _ = q.shape
    d = groups * NA_GROUP
    rows = (t - ctx_len) // GRID_W
    blk = NA_ROWS_PER_STEP * GRID_W
    assert blk == ROW_BLOCK and ctx_len % blk == 0 and t % blk == 0
    first = live_rows.first
    steps = t // blk - first
    heads, nvar = bias.shape[:2]

    def variant(tq):
        r = tq - ctx_len // GRID_W
        rs = jnp.clip(r - NA_KH // 2, 0, rows - NA_KH)
        return jnp.where(r < 0, nvar - 1, rs - r + NA_KH - 1)

    def bias_spec(sub):
        return pl.BlockSpec((heads, None) + bias.shape[2:],
                            lambda b, j: (0, variant((j + first) * NA_ROWS_PER_STEP + sub), 0, 0))

    whole = pl.BlockSpec((None, groups, t, NA_GROUP), lambda b, j: (b, 0, 0, 0))
    kern = functools.partial(_na_attn_kernel, first=first, ctx_len=ctx_len, rows=rows, groups=groups)
    return pl.pallas_call(
        kern,
        grid=(bsz, steps),
        in_specs=[pl.BlockSpec((None, groups, blk, NA_GROUP), lambda b, j: (b, 0, j + first, 0)), whole, whole]
        + [bias_spec(sub) for sub in range(NA_ROWS_PER_STEP)]
        + [live_rows.rows(d), live_rows.mod(d), _const2((d, d)), _const2((1, d))],
        out_specs=live_rows.rows(d),
        out_shape=jax.ShapeDtypeStruct((bsz, t, d), F32),
        scratch_shapes=[pltpu.VMEM((blk, d), BF16)],
        compiler_params=_params(("arbitrary", "arbitrary")),
        name="na_attn",
    )(q, k, v, *([bias] * NA_ROWS_PER_STEP), r, mods, w_out, b_out)


def _na_bias_table(rpb):
    heads, nrow, ncol = rpb.shape
    assert 2 * GRID_W == LANE and nrow == 2 * NA_KH - 1 and ncol == 2 * NA_KW - 1
    left = GRID_W - NA_KW
    padded = jnp.pad(rpb.astype(F32), ((0, 0), (0, 0), (left, LANE - left - ncol)))
    return pl.pallas_call(
        _na_bias_kernel,
        grid=(heads,),
        in_specs=[pl.BlockSpec((None, nrow, LANE), lambda h: (h, 0, 0))],
        out_specs=pl.BlockSpec((None, NA_KH + 1, GRID_W, NA_KH * GRID_W), lambda h: (h, 0, 0, 0)),
        out_shape=jax.ShapeDtypeStruct((heads, NA_KH + 1, GRID_W, NA_KH * GRID_W), F32),
        compiler_params=_params(("arbitrary",)),
        name="na_bias",
    )(padded)


def _na_bias_kernel(rpb_ref, o_ref):
    w = GRID_W
    q = lax.broadcasted_iota(jnp.int32, (w, LANE), 0)
    lane = lax.broadcasted_iota(jnp.int32, (w, LANE), 1)
    k = jnp.bitwise_and(lane, w - 1)
    start = jnp.clip(q - NA_KW // 2, 0, w - NA_KW)
    in_window = (k >= start) & (k < start + NA_KW)

    def toeplitz(d):
        row = jnp.broadcast_to(rpb_ref[d:d + 1, :], (w, LANE))
        return pltpu.roll(row, LANE - (w - 1), axis=1, stride=1, stride_axis=0)

    toep = [toeplitz(d) for d in range(rpb_ref.shape[0])]
    pair = [jnp.where(in_window, jnp.where(lane < w, toep[d], pltpu.roll(toep[d + 1], w, axis=1)), -jnp.inf)
            for d in range(len(toep) - 1)]
    for v in range(NA_KH):
        for i in range(NA_KH // 2):
            o_ref[v, :, i * LANE:(i + 1) * LANE] = pair[v + 2 * i]
    o_ref[NA_KH] = jnp.full(o_ref.shape[1:], -jnp.inf, F32)


def _ffn_kernel(x_ref, xp_ref, xn_ref, mod_ref, g_ref, wu_ref, bu_ref, cw_ref, cb_ref, wd_ref, bd_ref,
                *rest, latent, final, merged):
    rest = list(rest)
    fg_ref = rest.pop(0) if final else None
    if merged:
        rest.pop(0)
    o_ref, h_buf, acc_ref, *u_bufs = rest
    rows = x_ref.shape[0]
    x = x_ref[...]
    ext = jnp.concatenate([xp_ref[...], x, xn_ref[...]], axis=0)
    h_buf[...] = _norm_mod(ext, g_ref[...], mod_ref[3:4, :], mod_ref[4:5, :]).astype(BF16)
    if latent:
        has_prev = pl.program_id(1) > 0
        has_next = pl.program_id(1) < pl.num_programs(1) - 1
    else:
        has_prev = has_next = False
    nchunk = wd_ref.shape[0]
    taps = cw_ref.shape[1]
    acc_ref[...] = jnp.zeros_like(acc_ref)

    def up(c, slot):
        for half in range(2):
            cc = half * nchunk + c
            raw = _dot(h_buf[...], wu_ref[cc])
            buf = u_bufs[2 * slot + half]
            buf[HALO:HALO + rows, :] = raw[HALO:HALO + rows, :]
            buf[0:HALO, :] = jnp.where(has_prev, raw[0:HALO, :], -bu_ref[cc])
            buf[HALO + rows:, :] = jnp.where(has_next, raw[HALO + rows:, :], -bu_ref[cc])

    def down(c, slot):
        halves = []
        for half in range(2):
            cc = half * nchunk + c
            y = cb_ref[cc] + bu_ref[cc] * jnp.sum(cw_ref[cc], axis=0, keepdims=True)
            u = u_bufs[2 * slot + half][...]
            for t in range(taps):
                shifted = u if t == taps // 2 else pltpu.roll(u, (taps // 2 - t) % u.shape[0], axis=0)
                y = y + cw_ref[cc, t:t + 1, :] * shifted[HALO:HALO + rows, :]
            halves.append(y)
        act = (halves[0] * _silu(halves[1])).astype(BF16)
        acc_ref[...] += _dot(act, wd_ref[c])

    def stage(c, slot):
        up(c + 1, 1 - slot)
        down(c, slot)

    unroll = FF_UNROLL
    assert unroll % 2 == 0
    up(0, 0)

    def body(i, carry):
        for k in range(unroll):
            stage(unroll * i + k, k % 2)
        return carry

    looped = (nchunk - 1) // unroll
    lax.fori_loop(0, looped, body, 0)
    for c in range(looped * unroll, nchunk - 1):
        stage(c, c % 2)
    down(nchunk - 1, (nchunk - 1) % 2)
    o = x + mod_ref[5:6, :] * (acc_ref[...] + bd_ref[...])
    if final:
        o = o * lax.rsqrt(jnp.mean(o * o, axis=-1, keepdims=True) + EPS) * fg_ref[...]
    o_ref[...] = o


def _ffn_call(r, mods, layer, ctx_len, weights, latent, final_g=None, merge_into=None):
    bsz, t, d = r.shape
    final = final_g is not None
    merged = merge_into is not None
    if latent:
        rows = FF_ROWS
        assert (t - ctx_len) % rows == 0
        grid = (bsz, (t - ctx_len) // rows)
        start = lambda j: ctx_len + rows * j
        tile = lambda i: pl.multiple_of(i, HALO)
        cur = pl.BlockSpec((None, pl.Element(rows), pl.Element(d)), lambda b, j: (b, tile(start(j)), 0))
        prev = pl.BlockSpec((None, pl.Element(HALO), pl.Element(d)),
                            lambda b, j: (b, tile(jnp.maximum(start(j) - HALO, ctx_len)), 0))
        nxt = pl.BlockSpec((None, pl.Element(HALO), pl.Element(d)),
                           lambda b, j: (b, tile(jnp.minimum(start(j + 1), t - HALO)), 0))
        mod = pl.BlockSpec((None, None, 6, d), lambda b, j: (layer, b, 0, 0))
    else:
        rows = ctx_len
        grid = (bsz, 1)
        cur = pl.BlockSpec((None, rows, d), lambda b, j: (b, 0, 0))
        prev = nxt = pl.BlockSpec((None, HALO, d), lambda b, j: (b, 0, 0))
        mod = pl.BlockSpec((None, None, 6, d), lambda b, j: (layer, bsz, 0, 0))
    in_specs = [cur, prev, nxt, mod] + [_layer_const(w, layer) for w in weights]
    args = [r, r, r, mods, *weights]
    if final:
        in_specs.append(_const2((1, d)))
        args.append(final_g)
    aliases = {}
    if merged:
        aliases = {len(args): 0}
        in_specs.append(pl.BlockSpec(memory_space=pl.ANY))
        args.append(merge_into)
    if final:
        assert latent
        out_spec = pl.BlockSpec((None, rows, d), lambda b, j: (b, j, 0))
        out_shape = jax.ShapeDtypeStruct((bsz, t - ctx_len, d), F32)
    else:
        out_spec = cur
        out_shape = jax.ShapeDtypeStruct((bsz, t, d), F32)
    kern = functools.partial(_ffn_kernel, latent=latent, final=final, merged=merged)
    return pl.pallas_call(
        kern,
        grid=grid,
        in_specs=in_specs,
        out_specs=out_spec,
        out_shape=out_shape,
        input_output_aliases=aliases,
        scratch_shapes=[pltpu.VMEM((rows + 2 * HALO, d), BF16), pltpu.VMEM((rows, d), F32)]
        + [pltpu.VMEM((rows + 2 * HALO, FF_CHUNK), F32)] * 4,
        compiler_params=_params(("arbitrary", "arbitrary")),
        name="ffn_latent" if latent else "ffn_context",
    )(*args)


def _ffn_weights(g, w_up, b_up, conv_w, conv_b, w_down, b_down):
    layers, d, hidden2 = w_up.shape
    nc2 = hidden2 // FF_CHUNK
    taps = conv_w.shape[1]
    return (g.reshape(layers, 1, d),
            w_up.astype(BF16).reshape(layers, d, nc2, FF_CHUNK).transpose(0, 2, 1, 3),
            b_up.reshape(layers, nc2, 1, FF_CHUNK),
            conv_w.reshape(layers, taps, nc2, FF_CHUNK).transpose(0, 2, 1, 3),
            conv_b.reshape(layers, nc2, 1, FF_CHUNK),
            w_down.astype(BF16).reshape(layers, nc2 // 2, FF_CHUNK, d),
            b_down.reshape(layers, 1, d))


def _block_diag_tiles(w):
    nb, bs, _ = w.shape
    per = BD_TILE // bs
    rep = jnp.tile(w.reshape(nb * bs, bs), (1, per)).reshape(nb // per, BD_TILE, BD_TILE)
    pos = np.arange(BD_TILE) // bs
    return jnp.where(jnp.asarray(pos[:, None] == pos[None, :]), rep, 0.0).astype(BF16)


def _gate_cols(gate_w, gate_b, inner):
    dirs, _, ng = gate_w.shape
    g = gate_w.transpose(1, 0, 2).reshape(3 * inner, dirs * ng)
    g = jnp.pad(g, ((0, 0), (0, LANE - dirs * ng))).astype(BF16)
    gb = jnp.pad(gate_b.reshape(1, dirs * ng), ((0, 0), (0, LANE - dirs * ng)))
    return g[:inner], g[inner:2 * inner], g[2 * inner:], gb


def kernel(x, c, ctx, c_ctx, ada_w, ada_b, norm1_g, norm2_g, final_g, ml_w_in, ml_conv_w, ml_conv_b, ml_wq, ml_wk, ml_wv, ml_gate_w, ml_gate_b, ml_skip, ml_hnorm_g, ml_w_out, na_w_qkv, na_b_qkv, na_rpb, na_w_out, na_b_out, ff_w_up, ff_b_up, ff_conv_w, ff_conv_b, ff_w_down, ff_b_down):
    bsz, seq, d = x.shape
    ctx_len = ctx.shape[1]
    depth = ada_w.shape[0]
    assert ctx_len == ROW_BLOCK and seq % ROW_BLOCK == 0 and seq % GRID_W == 0
    assert bsz + 1 <= MOD_ROWS
    t = ctx_len + seq
    nblk = t // ROW_BLOCK

    c_all = jnp.zeros((MOD_ROWS, d), F32).at[:bsz].set(c).at[bsz].set(c_ctx)
    mods = _ada(c_all, ada_w, ada_b).reshape(depth, MOD_ROWS, 6, d)

    r = (ctx, x)
    row2 = lambda a: a.reshape(1, -1)
    ffn_weights = _ffn_weights(norm2_g, ff_w_up, ff_b_up, ff_conv_w, ff_conv_b, ff_w_down, ff_b_down)
    out = None
    for i in range(depth):
        need_ctx = i < depth - 1
        j = i // 2
        all_rows = _Rows(bsz, nblk, i, 0)
        live_rows = all_rows if need_ctx else _Rows(bsz, nblk, i, 1)
        if i % 2 == 0:
            inner = ml_w_in.shape[2] // 2
            dh = inner // ML_HEADS
            u, sz = _ml_in(all_rows, r, mods, row2(norm1_g[i]), ml_w_in[j].astype(BF16))
            gq, gk, gv, gb = _gate_cols(ml_gate_w[j], ml_gate_b[j], inner)
            uc, q, k, v, gates = _ml_proj(u, ml_conv_w[j], row2(ml_conv_b[j]), _block_diag_tiles(ml_wq[j]),
                                          _block_diag_tiles(ml_wk[j]), _block_diag_tiles(ml_wv[j]),
                                          gq, gk, gv, gb, dh, dh ** -0.5,
                                          ml_gate_w.shape[1] * ml_gate_w.shape[3])
            hn = _ml_core(q, k, v, gates, dh)
            r = _ml_out(live_rows, hn, uc, sz, r, mods, row2(ml_hnorm_g[j]), row2(ml_skip[j]),
                        ml_w_out[j].astype(BF16))
        else:
            q, k, v = _na_qkv(all_rows, r, mods, row2(norm1_g[i]), na_w_qkv[j].astype(BF16), row2(na_b_qkv[j]),
                              NA_HEAD_DIM ** -0.5 * LOG2E)
            r = _na_attn(live_rows, q, k, v, _na_bias_table(na_rpb[j] * LOG2E), r, mods,
                         na_w_out[j].astype(BF16), row2(na_b_out[j]), ctx_len)
        if need_ctx:
            r_new = _ffn_call(r, mods, i, ctx_len, ffn_weights, latent=True)
            r = _ffn_call(r, mods, i, ctx_len, ffn_weights, latent=False, merge_into=r_new)
        else:
            out = _ffn_call(r, mods, i, ctx_len, ffn_weights, latent=True, final_g=row2(final_g))
    return out
```

```python
import functools

import jax
import jax.numpy as jnp
import numpy as np
from jax import lax
from jax.experimental import pallas as pl
from jax.experimental.pallas import tpu as pltpu

F32 = jnp.float32
BF16 = jnp.bfloat16
EPS = 1e-6
LOG2E = 1.4426950408889634

GRID_W = 64
ML_HEADS = 4
NA_HEAD_DIM = 64
NA_KH = 8
NA_KW = 16

ROW_BLOCK = 256
HALO = 8
BD_TILE = 256
LANE = 128
NA_GROUP = 256
NA_ROWS_PER_STEP = 4
ML_HEADS_PER_STEP = 2
FF_CHUNK = 256
FF_UNROLL = 4
FF_ROWS = 512
MOD_ROWS = 16
V7X_VMEM_BYTES = 64 * 1024 * 1024
VMEM_LIMIT = V7X_VMEM_BYTES * 7 // 8


def _dot(a, b):
    return jnp.dot(a, b, preferred_element_type=F32)


def _dot_nt(a, b):
    return lax.dot_general(a, b, (((1,), (1,)), ((), ())), preferred_element_type=F32)


def _dot_tn(a, b):
    return lax.dot_general(a, b, (((0,), (0,)), ((), ())), preferred_element_type=F32)


def _silu(x):
    h = 0.5 * x
    return h + h * jnp.tanh(h)


def _log_sigmoid(x):
    return jnp.minimum(x, 0.0) - jnp.log1p(jnp.exp(-jnp.abs(x)))


def _norm_mod(x, g, shift, scale):
    y = x * lax.rsqrt(jnp.mean(x * x, axis=-1, keepdims=True) + EPS) * g
    return y * (1.0 + scale) + shift


def _params(sem):
    return pltpu.CompilerParams(dimension_semantics=sem, vmem_limit_bytes=VMEM_LIMIT)


def _resident(shape, index_map):
    return pl.BlockSpec(shape, index_map, pipeline_mode=pl.Buffered(1))


def _ada_kernel(c_ref, w_ref, b_ref, o_ref):
    s = _silu(c_ref[...]).astype(BF16)
    o_ref[...] = _dot(s, w_ref[...].astype(BF16)) + b_ref[...]


def _ada(c_all, ada_w, ada_b):
    depth, d, d6 = ada_w.shape
    n = d6 // d
    return pl.pallas_call(
        _ada_kernel,
        grid=(depth, n),
        in_specs=[
            pl.BlockSpec((MOD_ROWS, d), lambda l, j: (0, 0)),
            pl.BlockSpec((None, d, d), lambda l, j: (l, 0, j)),
            pl.BlockSpec((None, 1, d), lambda l, j: (l, 0, j)),
        ],
        out_specs=pl.BlockSpec((None, MOD_ROWS, d), lambda l, j: (l, 0, j)),
        out_shape=jax.ShapeDtypeStruct((depth, MOD_ROWS, d6), F32),
        compiler_params=_params(("arbitrary", "arbitrary")),
        name="ada",
    )(c_all, ada_w, ada_b.reshape(depth, 1, d6))


class _Rows:
    def __init__(self, bsz, nblk_total, layer, first_blk):
        self.bsz = bsz
        self.nblk_total = nblk_total
        self.layer = layer
        self.first = first_blk
        self.grid = (bsz, nblk_total - first_blk)

    def rows(self, width, rows=ROW_BLOCK):
        f = self.first
        return pl.BlockSpec((None, rows, width), lambda b, j: (b, j + f, 0))

    def residual(self, r):
        if not isinstance(r, tuple):
            return [self.rows(r.shape[-1])], [r]
        assert self.first == 0
        d = r[1].shape[-1]
        return ([pl.BlockSpec((None, ROW_BLOCK, d), lambda b, j: (b, 0, 0)),
                 pl.BlockSpec((None, ROW_BLOCK, d), lambda b, j: (b, jnp.maximum(j - 1, 0), 0))], list(r))

    def mod(self, d):
        f, bsz, layer = self.first, self.bsz, self.layer
        return pl.BlockSpec((None, None, 6, d), lambda b, j: (layer, jnp.where(j + f == 0, bsz, b), 0, 0))


def _const2(shape):
    return _resident(shape, lambda b, j: (0,) * len(shape))


def _layer_const(stacked, layer):
    shape = stacked.shape[1:]
    return _resident((None,) + shape, lambda b, j: (layer,) + (0,) * len(shape))


def _residual_block(refs):
    if len(refs) == 1:
        return refs[0][...]
    return jnp.where(pl.program_id(1) == 0, refs[0][...], refs[1][...])


def _residual_shape(r):
    if isinstance(r, tuple):
        return r[1].shape[0], r[0].shape[1] + r[1].shape[1], r[1].shape[2]
    return r.shape


def _proj_kernel(*refs, nres, shift, scale, chunk, splits, has_bias, out_scale, silu_out=()):
    res, (mod_ref, g_ref, w_ref), rest = refs[:nres], refs[nres:nres + 3], refs[nres + 3:]
    if has_bias:
        b_ref, out_refs = rest[0], rest[1:]
    else:
        b_ref, out_refs = None, rest
    h = _norm_mod(_residual_block(res), g_ref[...], mod_ref[shift:shift + 1, :],
                  mod_ref[scale:scale + 1, :]).astype(BF16)
    n = w_ref.shape[1]
    per_out = n // len(out_refs)
    for c in range(n // chunk):
        y = _dot(h, w_ref[:, c * chunk:(c + 1) * chunk])
        if has_bias:
            y = y + b_ref[:, c * chunk:(c + 1) * chunk]
        oi, off = divmod(c * chunk, per_out)
        if out_scale[oi] != 1.0:
            y = y * out_scale[oi]
        if oi in silu_out:
            y = _silu(y)
        y = y.astype(BF16)
        o_ref = out_refs[oi]
        if splits:
            for t in range(chunk // splits):
                o_ref[(off + t * splits) // splits] = y[:, t * splits:(t + 1) * splits]
        else:
            o_ref[:, off:off + chunk] = y


def _ml_in(rows, r, mods, g, w_in):
    bsz, t, d = _residual_shape(r)
    n = w_in.shape[1]
    half = n // 2
    res_specs, res_args = rows.residual(r)
    kern = functools.partial(_proj_kernel, nres=len(res_args), shift=0, scale=1, chunk=512, splits=0, has_bias=False,
                             out_scale=(1.0, 1.0), silu_out=(1,))
    return pl.pallas_call(
        kern,
        grid=rows.grid,
        in_specs=res_specs + [rows.mod(d), _const2((1, d)), _const2((d, n))],
        out_specs=[rows.rows(half), rows.rows(half)],
        out_shape=[jax.ShapeDtypeStruct((bsz, t, half), BF16)] * 2,
        compiler_params=_params(("arbitrary", "arbitrary")),
        name="ml_in",
    )(*res_args, mods, g, w_in)


def _na_qkv(rows, r, mods, g, w, b, q_scale):
    bsz, t, d = r.shape
    groups = d // NA_GROUP
    f = rows.first
    out_spec = pl.BlockSpec((None, groups, ROW_BLOCK, NA_GROUP), lambda bb, j: (bb, 0, j + f, 0))
    kern = functools.partial(_proj_kernel, nres=1, shift=0, scale=1, chunk=NA_GROUP, splits=NA_GROUP, has_bias=True,
                             out_scale=(q_scale, 1.0, 1.0))
    return pl.pallas_call(
        kern,
        grid=rows.grid,
        in_specs=[rows.rows(d), rows.mod(d), _const2((1, d)), _const2((d, 3 * d)), _const2((1, 3 * d))],
        out_specs=[out_spec] * 3,
        out_shape=[jax.ShapeDtypeStruct((bsz, groups, t, NA_GROUP), BF16)] * 3,
        compiler_params=_params(("arbitrary", "arbitrary")),
        name="na_qkv",
    )(r, mods, g, w, b)


def _ml_proj_kernel(u_ref, cw_ref, cb_ref, wq_ref, wk_ref, wv_ref, gq_ref, gk_ref, gv_ref, gb_ref,
                    uc_ref, q_ref, k_ref, v_ref, g_ref, *, nblk, kscale, ngates):
    @pl.when(pl.program_id(1) == 0)
    def _():
        g_ref[...] = jnp.broadcast_to(gb_ref[...], g_ref.shape)

    cw = cw_ref[...]
    cb = cb_ref[...]
    taps = cw.shape[0]
    width = u_ref.shape[1]
    pad = 2 * HALO
    for j in range(nblk):
        r0 = j * ROW_BLOCK
        cur_b = u_ref[r0:r0 + ROW_BLOCK, :]
        cur = cur_b.astype(F32)
        if j in (0, 1):
            prev = jnp.zeros((HALO, width), F32)
        else:
            prev = u_ref[r0 - pad:r0, :].astype(F32)[HALO:, :]
        if j in (0, nblk - 1):
            nxt = jnp.zeros((HALO, width), F32)
        else:
            nxt = u_ref[r0 + ROW_BLOCK:r0 + ROW_BLOCK + pad, :].astype(F32)[:HALO, :]
        ext = jnp.concatenate([prev, cur, nxt], axis=0)
        acc = cb
        for t in range(taps):
            shifted = ext if t == taps // 2 else pltpu.roll(ext, (taps // 2 - t) % ext.shape[0], axis=0)
            acc = acc + cw[t:t + 1, :] * shifted[HALO:HALO + ROW_BLOCK, :]
        uc_b = _silu(acc).astype(BF16)
        uc_ref[r0:r0 + ROW_BLOCK, :] = uc_b
        qs, ks, vs = [], [], []
        for t in range(width // BD_TILE):
            sl = slice(t * BD_TILE, (t + 1) * BD_TILE)
            qs.append(_dot(uc_b[:, sl], wq_ref[t]))
            ks.append(_dot(uc_b[:, sl], wk_ref[t]) * kscale)
            vs.append(_dot(cur_b[:, sl], wv_ref[t]))
        q_b = jnp.concatenate(qs, axis=1).astype(BF16)
        k_b = jnp.concatenate(ks, axis=1).astype(BF16)
        v_b = jnp.concatenate(vs, axis=1).astype(BF16)
        q_ref[r0:r0 + ROW_BLOCK, :] = q_b
        k_ref[r0:r0 + ROW_BLOCK, :] = k_b
        v_ref[r0:r0 + ROW_BLOCK, :] = v_b
        g_ref[r0:r0 + ROW_BLOCK, :] += _dot(q_b, gq_ref[...]) + _dot(k_b, gk_ref[...]) + _dot(v_b, gv_ref[...])

    @pl.when(pl.program_id(1) == pl.num_programs(1) - 1)
    def _():
        lane = lax.broadcasted_iota(jnp.int32, (ROW_BLOCK, g_ref.shape[1]), 1)
        forget = (lane < ngates) & (lane % (ngates // 2) >= ngates // 4)
        for j in range(nblk):
            pre = g_ref[j * ROW_BLOCK:(j + 1) * ROW_BLOCK, :]
            g_ref[j * ROW_BLOCK:(j + 1) * ROW_BLOCK, :] = jnp.where(forget, _log_sigmoid(pre), pre) * LOG2E


def _ml_proj(u, conv_w, conv_b, wq_bd, wk_bd, wv_bd, gq, gk, gv, gb, dh, kscale, ngates):
    bsz, t, inner = u.shape
    heads = inner // dh
    tiles = dh // BD_TILE
    taps = conv_w.shape[0]
    col = lambda shape: pl.BlockSpec(shape, lambda b, h: (0, h))
    seq = pl.BlockSpec((None, t, dh), lambda b, h: (b, 0, h))
    bd = pl.BlockSpec((tiles, BD_TILE, BD_TILE), lambda b, h: (h, 0, 0))
    gw = pl.BlockSpec((dh, LANE), lambda b, h: (h, 0))
    kern = functools.partial(_ml_proj_kernel, nblk=t // ROW_BLOCK, kscale=kscale, ngates=ngates)
    return pl.pallas_call(
        kern,
        grid=(bsz, heads),
        in_specs=[seq, col((taps, dh)), col((1, dh)), bd, bd, bd, gw, gw, gw,
                  pl.BlockSpec((1, LANE), lambda b, h: (0, 0))],
        out_specs=[seq, seq, seq, seq, pl.BlockSpec((None, t, LANE), lambda b, h: (b, 0, 0))],
        out_shape=[jax.ShapeDtypeStruct((bsz, t, inner), BF16)] * 4 + [jax.ShapeDtypeStruct((bsz, t, LANE), F32)],
        compiler_params=_params(("arbitrary", "arbitrary")),
        name="ml_proj",
    )(u, conv_w, conv_b, wq_bd, wk_bd, wv_bd, gq, gk, gv, gb)


def _ml_chunk(d, q, k, v, g, ct_ref, n_ref, m_ref, ci, cf):
    L = q.shape[0]
    gt = g.T
    lane = lax.broadcasted_iota(jnp.int32, g.shape, 1)
    sub = lax.broadcasted_iota(jnp.int32, gt.shape, 0)
    li_col = jnp.sum(jnp.where(lane == ci, g, 0.0), axis=1, keepdims=True)
    lf_col = jnp.sum(jnp.where(lane == cf, g, 0.0), axis=1, keepdims=True)
    li_row = jnp.sum(jnp.where(sub == ci, gt, 0.0), axis=0, keepdims=True)
    lf_row = jnp.sum(jnp.where(sub == cf, gt, 0.0), axis=0, keepdims=True)

    tt = lax.broadcasted_iota(jnp.int32, (L, L), 0)
    ss = lax.broadcasted_iota(jnp.int32, (L, L), 1)
    seen = (ss >= tt) if d else (ss <= tt)
    seen_t = (tt >= ss) if d else (tt <= ss)
    b_col = jnp.sum(jnp.where(seen, lf_row, 0.0), axis=1, keepdims=True)
    b_row = jnp.sum(jnp.where(seen_t, lf_col, 0.0), axis=0, keepdims=True)
    b_last = jnp.sum(lf_row, axis=1, keepdims=True)

    m_prev = m_ref[...]
    r_row = li_row - b_row
    m_t = b_col + jnp.maximum(m_prev, jnp.max(jnp.where(seen, r_row, -jnp.inf), axis=1, keepdims=True))
    sc = _dot_nt(q, k) * jnp.exp2(jnp.where(seen, (b_col - m_t) + r_row, -jnp.inf))
    w_inter = jnp.exp2(b_col + m_prev - m_t)
    n_rows = jnp.broadcast_to(n_ref[...], (2 * HALO, n_ref.shape[1])).astype(BF16)
    qn = _dot_nt(q, n_rows)[:, :1]
    num = _dot(sc.astype(BF16), v) + w_inter * _dot(q, ct_ref[...].astype(BF16))
    den = jnp.sum(sc, axis=1, keepdims=True) + w_inter * qn
    hh = num * (1.0 / jnp.maximum(jnp.abs(den), jnp.exp2(-m_t)))

    g_row = b_last + r_row
    g_col = b_last - b_col + li_col
    m_new = jnp.maximum(b_last + m_prev, jnp.max(g_row, axis=1, keepdims=True))
    decay = jnp.exp2(b_last + m_prev - m_new)
    vw = (v.astype(F32) * jnp.exp2(g_col - m_new)).astype(BF16)
    wk_rows = jnp.broadcast_to(jnp.exp2(g_row - m_new), (2 * HALO, L)).astype(BF16)
    ct_ref[...] = decay * ct_ref[...] + _dot_tn(k, vw)
    n_ref[...] = decay * n_ref[...] + _dot(wk_rows, k)[:1, :]
    m_ref[...] = m_new
    return hh


def _bwd_block(s, nblk):
    return jnp.where(s == 0, 0, nblk - s)


def _ml_core_kernel(qf_ref, kf_ref, vf_ref, gf_ref, qb_ref, kb_ref, vb_ref, gb_ref, o_ref, hacc_ref, *state,
                    heads, nblk, dh):
    s = pl.program_id(2)
    L = qf_ref.shape[0]
    per_step = len(state) // 6

    @pl.when(s == 0)
    def _():
        for ref in (hacc_ref,) + state:
            ref[...] = jnp.zeros_like(ref)

    rows_f = pl.ds(pl.multiple_of(s * L, L), L)
    rows_b = pl.ds(pl.multiple_of(_bwd_block(s, nblk) * L, L), L)
    gf, gb = gf_ref[...], gb_ref[...]
    for i in range(per_step):
        h = pl.program_id(1) * per_step + i
        cols = slice(i * dh, (i + 1) * dh)
        ctf, nf, mf, ctb, nb, mb = state[6 * i:6 * i + 6]
        hacc_ref[rows_f, cols] += _ml_chunk(0, qf_ref[:, cols], kf_ref[:, cols], vf_ref[:, cols], gf,
                                            ctf, nf, mf, h, h + heads)
        hacc_ref[rows_b, cols] += _ml_chunk(1, qb_ref[:, cols], kb_ref[:, cols], vb_ref[:, cols], gb,
                                            ctb, nb, mb, 2 * heads + h, 3 * heads + h)

    @pl.when(s == nblk - 1)
    def _():
        for j in range(nblk):
            for i in range(per_step):
                tot = hacc_ref[j * L:(j + 1) * L, i * dh:(i + 1) * dh]
                cen = tot - jnp.mean(tot, axis=-1, keepdims=True)
                var = jnp.mean(cen * cen, axis=-1, keepdims=True)
                o_ref[j * L:(j + 1) * L, i * dh:(i + 1) * dh] = (cen * lax.rsqrt(var + EPS)).astype(BF16)


def _ml_core(q, k, v, gates, dh):
    bsz, t, inner = q.shape
    heads = inner // dh
    nblk = t // ROW_BLOCK
    per_step = ML_HEADS_PER_STEP
    assert heads % per_step == 0
    width = per_step * dh
    fwd = pl.BlockSpec((None, ROW_BLOCK, width), lambda b, h, s: (b, s, h))
    bwd = pl.BlockSpec((None, ROW_BLOCK, width), lambda b, h, s: (b, _bwd_block(s, nblk), h))
    gate_f = pl.BlockSpec((None, ROW_BLOCK, LANE), lambda b, h, s: (b, s, 0))
    gate_b = pl.BlockSpec((None, ROW_BLOCK, LANE), lambda b, h, s: (b, _bwd_block(s, nblk), 0))
    state = [pltpu.VMEM((dh, dh), F32), pltpu.VMEM((1, dh), F32), pltpu.VMEM((1, 1), F32)]
    kern = functools.partial(_ml_core_kernel, heads=heads, nblk=nblk, dh=dh)
    return pl.pallas_call(
        kern,
        grid=(bsz, heads // per_step, nblk),
        in_specs=[fwd, fwd, fwd, gate_f, bwd, bwd, bwd, gate_b],
        out_specs=pl.BlockSpec((None, t, width), lambda b, h, s: (b, 0, h)),
        out_shape=jax.ShapeDtypeStruct((bsz, t, inner), BF16),
        scratch_shapes=[pltpu.VMEM((t, width), F32)] + state * (2 * per_step),
        compiler_params=_params(("arbitrary",) * 3),
        name="ml_core",
    )(q, k, v, gates, q, k, v, gates)


def _ml_out_kernel(hn_ref, uc_ref, sz_ref, *refs, nres):
    res, (mod_ref, hg_ref, sk_ref, w_ref, o_ref) = refs[:nres], refs[nres:]
    y = (hn_ref[...].astype(F32) * hg_ref[...] + sk_ref[...] * uc_ref[...].astype(F32)) * sz_ref[...].astype(F32)
    o_ref[...] = _residual_block(res) + mod_ref[2:3, :] * _dot(y.astype(BF16), w_ref[...])


def _ml_out(rows, hn, uc, sz, r, mods, hnorm_g, skip, w_out):
    bsz, t, d = _residual_shape(r)
    inner = hn.shape[-1]
    res_specs, res_args = rows.residual(r)
    return pl.pallas_call(
        functools.partial(_ml_out_kernel, nres=len(res_args)),
        grid=rows.grid,
        in_specs=[rows.rows(inner), rows.rows(inner), rows.rows(inner)] + res_specs
        + [rows.mod(d), _const2((1, inner)), _const2((1, inner)), _const2((inner, d))],
        out_specs=rows.rows(d),
        out_shape=jax.ShapeDtypeStruct((bsz, t, d), F32),
        compiler_params=_params(("arbitrary", "arbitrary")),
        name="ml_out",
    )(hn, uc, sz, *res_args, mods, hnorm_g, skip, w_out)


def _na_attn_kernel(q_ref, k_ref, v_ref, *rest, first, ctx_len, rows, groups):
    b_refs = rest[:NA_ROWS_PER_STEP]
    r_ref, mod_ref, w_ref, bo_ref, o_ref, a_buf = rest[NA_ROWS_PER_STEP:]
    gw = GRID_W
    per = NA_GROUP // NA_HEAD_DIM
    nloc = NA_KH * gw
    lane = lax.broadcasted_iota(jnp.int32, (gw, NA_GROUP), 1)
    mine = [(lane >= h * NA_HEAD_DIM) & (lane < (h + 1) * NA_HEAD_DIM) for h in range(per)]

    def key_start(sub):
        t = (pl.program_id(1) + first) * NA_ROWS_PER_STEP + sub
        r = t - ctx_len // gw
        rs = jnp.clip(r - NA_KH // 2, 0, rows - NA_KH)
        return pl.multiple_of(ctx_len + rs * gw, gw)

    kstart = [key_start(sub) for sub in range(NA_ROWS_PER_STEP)]

    def scores(sub, g):
        qg = q_ref[g, sub * gw:(sub + 1) * gw, :].astype(F32)
        qm = jnp.concatenate([jnp.where(mine[h], qg, 0.0) for h in range(per)], axis=0).astype(BF16)
        return _dot_nt(qm, k_ref[g, pl.ds(kstart[sub], nloc), :]), _dot_nt(qm, k_ref[g, 0:ctx_len, :])

    work = [(sub, g) for sub in range(NA_ROWS_PER_STEP) for g in range(groups)]
    ahead = scores(*work[0])
    for i, (sub, g) in enumerate(work):
        s_loc, s_ctx = ahead
        if i + 1 < len(work):
            ahead = scores(*work[i + 1])
        s_loc = s_loc + b_refs[sub][per * g:per * (g + 1)].reshape(per * gw, nloc)
        m = jnp.maximum(jnp.max(s_loc, axis=1, keepdims=True), jnp.max(s_ctx, axis=1, keepdims=True))
        p_loc = jnp.exp2(s_loc - m)
        p_ctx = jnp.exp2(s_ctx - m)
        den = jnp.sum(p_loc, axis=1, keepdims=True) + jnp.sum(p_ctx, axis=1, keepdims=True)
        o = (_dot(p_loc.astype(BF16), v_ref[g, pl.ds(kstart[sub], nloc), :])
             + _dot(p_ctx.astype(BF16), v_ref[g, 0:ctx_len, :])) * (1.0 / den)
        og = jnp.where(mine[0], o[0:gw], 0.0)
        for h in range(1, per):
            og = jnp.where(mine[h], o[h * gw:(h + 1) * gw], og)
        a_buf[sub * gw:(sub + 1) * gw, g * NA_GROUP:(g + 1) * NA_GROUP] = og.astype(BF16)

    o_ref[...] = r_ref[...] + mod_ref[2:3, :] * (_dot(a_buf[...], w_ref[...]) + bo_ref[...])


def _na_attn(live_rows, q, k, v, bias, r, mods, w_out, b_out, ctx_len):
    bsz, groups, t, _ = q.shape
    d = groups * NA_GROUP
    rows = (t - ctx_len) // GRID_W
    blk = NA_ROWS_PER_STEP * GRID_W
    assert blk == ROW_BLOCK and ctx_len % blk == 0 and t % blk == 0
    first = live_rows.first
    steps = t // blk - first
    heads, nvar = bias.shape[:2]

    def variant(tq):
        r = tq - ctx_len // GRID_W
        rs = jnp.clip(r - NA_KH // 2, 0, rows - NA_KH)
        return jnp.where(r < 0, nvar - 1, rs - r + NA_KH - 1)

    def bias_spec(sub):
        return pl.BlockSpec((heads, None) + bias.shape[2:],
                            lambda b, j: (0, variant((j + first) * NA_ROWS_PER_STEP + sub), 0, 0))

    whole = pl.BlockSpec((None, groups, t, NA_GROUP), lambda b, j: (b, 0, 0, 0))
    kern = functools.partial(_na_attn_kernel, first=first, ctx_len=ctx_len, rows=rows, groups=groups)
    return pl.pallas_call(
        kern,
        grid=(bsz, steps),
        in_specs=[pl.BlockSpec((None, groups, blk, NA_GROUP), lambda b, j: (b, 0, j + first, 0)), whole, whole]
        + [bias_spec(sub) for sub in range(NA_ROWS_PER_STEP)]
        + [live_rows.rows(d), live_rows.mod(d), _const2((d, d)), _const2((1, d))],
        out_specs=live_rows.rows(d),
        out_shape=jax.ShapeDtypeStruct((bsz, t, d), F32),
        scratch_shapes=[pltpu.VMEM((blk, d), BF16)],
        compiler_params=_params(("arbitrary", "arbitrary")),
        name="na_attn",
    )(q, k, v, *([bias] * NA_ROWS_PER_STEP), r, mods, w_out, b_out)


def _na_bias_table(rpb):
    heads, nrow, ncol = rpb.shape
    assert 2 * GRID_W == LANE and nrow == 2 * NA_KH - 1 and ncol == 2 * NA_KW - 1
    left = GRID_W - NA_KW
    padded = jnp.pad(rpb.astype(F32), ((0, 0), (0, 0), (left, LANE - left - ncol)))
    return pl.pallas_call(
        _na_bias_kernel,
        grid=(heads,),
        in_specs=[pl.BlockSpec((None, nrow, LANE), lambda h: (h, 0, 0))],
        out_specs=pl.BlockSpec((None, NA_KH + 1, GRID_W, NA_KH * GRID_W), lambda h: (h, 0, 0, 0)),
        out_shape=jax.ShapeDtypeStruct((heads, NA_KH + 1, GRID_W, NA_KH * GRID_W), F32),
        compiler_params=_params(("arbitrary",)),
        name="na_bias",
    )(padded)


def _na_bias_kernel(rpb_ref, o_ref):
    w = GRID_W
    q = lax.broadcasted_iota(jnp.int32, (w, LANE), 0)
    lane = lax.broadcasted_iota(jnp.int32, (w, LANE), 1)
    k = jnp.bitwise_and(lane, w - 1)
    start = jnp.clip(q - NA_KW // 2, 0, w - NA_KW)
    in_window = (k >= start) & (k < start + NA_KW)

    def toeplitz(d):
        row = jnp.broadcast_to(rpb_ref[d:d + 1, :], (w, LANE))
        return pltpu.roll(row, LANE - (w - 1), axis=1, stride=1, stride_axis=0)

    toep = [toeplitz(d) for d in range(rpb_ref.shape[0])]
    pair = [jnp.where(in_window, jnp.where(lane < w, toep[d], pltpu.roll(toep[d + 1], w, axis=1)), -jnp.inf)
            for d in range(len(toep) - 1)]
    for v in range(NA_KH):
        for i in range(NA_KH // 2):
            o_ref[v, :, i * LANE:(i + 1) * LANE] = pair[v + 2 * i]
    o_ref[NA_KH] = jnp.full(o_ref.shape[1:], -jnp.inf, F32)


def _ffn_kernel(x_ref, xp_ref, xn_ref, mod_ref, g_ref, wu_ref, bu_ref, cw_ref, cb_ref, wd_ref, bd_ref,
                *rest, latent, final, merged):
    rest = list(rest)
    fg_ref = rest.pop(0) if final else None
    if merged:
        rest.pop(0)
    o_ref, h_buf, acc_ref, *u_bufs = rest
    rows = x_ref.shape[0]
    x = x_ref[...]
    ext = jnp.concatenate([xp_ref[...], x, xn_ref[...]], axis=0)
    h_buf[...] = _norm_mod(ext, g_ref[...], mod_ref[3:4, :], mod_ref[4:5, :]).astype(BF16)
    if latent:
        has_prev = pl.program_id(1) > 0
        has_next = pl.program_id(1) < pl.num_programs(1) - 1
    else:
        has_prev = has_next = False
    nchunk = wd_ref.shape[0]
    taps = cw_ref.shape[1]
    acc_ref[...] = jnp.zeros_like(acc_ref)

    def up(c, slot):
        for half in range(2):
            cc = half * nchunk + c
            raw = _dot(h_buf[...], wu_ref[cc])
            buf = u_bufs[2 * slot + half]
            buf[HALO:HALO + rows, :] = raw[HALO:HALO + rows, :]
            buf[0:HALO, :] = jnp.where(has_prev, raw[0:HALO, :], -bu_ref[cc])
            buf[HALO + rows:, :] = jnp.where(has_next, raw[HALO + rows:, :], -bu_ref[cc])

    def down(c, slot):
        halves = []
        for half in range(2):
            cc = half * nchunk + c
            y = cb_ref[cc] + bu_ref[cc] * jnp.sum(cw_ref[cc], axis=0, keepdims=True)
            u = u_bufs[2 * slot + half][...]
            for t in range(taps):
                shifted = u if t == taps // 2 else pltpu.roll(u, (taps // 2 - t) % u.shape[0], axis=0)
                y = y + cw_ref[cc, t:t + 1, :] * shifted[HALO:HALO + rows, :]
            halves.append(y)
        act = (halves[0] * _silu(halves[1])).astype(BF16)
        acc_ref[...] += _dot(act, wd_ref[c])

    def stage(c, slot):
        up(c + 1, 1 - slot)
        down(c, slot)

    unroll = FF_UNROLL
    assert unroll % 2 == 0
    up(0, 0)

    def body(i, carry):
        for k in range(unroll):
            stage(unroll * i + k, k % 2)
        return carry

    looped = (nchunk - 1) // unroll
    lax.fori_loop(0, looped, body, 0)
    for c in range(looped * unroll, nchunk - 1):
        stage(c, c % 2)
    down(nchunk - 1, (nchunk - 1) % 2)
    o = x + mod_ref[5:6, :] * (acc_ref[...] + bd_ref[...])
    if final:
        o = o * lax.rsqrt(jnp.mean(o * o, axis=-1, keepdims=True) + EPS) * fg_ref[...]
    o_ref[...] = o


def _ffn_call(r, mods, layer, ctx_len, weights, latent, final_g=None, merge_into=None):
    bsz, t, d = r.shape
    final = final_g is not None
    merged = merge_into is not None
    if latent:
        rows = FF_ROWS
        assert (t - ctx_len) % rows == 0
        grid = (bsz, (t - ctx_len) // rows)
        start = lambda j: ctx_len + rows * j
        tile = lambda i: pl.multiple_of(i, HALO)
        cur = pl.BlockSpec((None, pl.Element(rows), pl.Element(d)), lambda b, j: (b, tile(start(j)), 0))
        prev = pl.BlockSpec((None, pl.Element(HALO), pl.Element(d)),
                            lambda b, j: (b, tile(jnp.maximum(start(j) - HALO, ctx_len)), 0))
        nxt = pl.BlockSpec((None, pl.Element(HALO), pl.Element(d)),
                           lambda b, j: (b, tile(jnp.minimum(start(j + 1), t - HALO)), 0))
        mod = pl.BlockSpec((None, None, 6, d), lambda b, j: (layer, b, 0, 0))
    else:
        rows = ctx_len
        grid = (bsz, 1)
        cur = pl.BlockSpec((None, rows, d), lambda b, j: (b, 0, 0))
        prev = nxt = pl.BlockSpec((None, HALO, d), lambda b, j: (b, 0, 0))
        mod = pl.BlockSpec((None, None, 6, d), lambda b, j: (layer, bsz, 0, 0))
    in_specs = [cur, prev, nxt, mod] + [_layer_const(w, layer) for w in weights]
    args = [r, r, r, mods, *weights]
    if final:
        in_specs.append(_const2((1, d)))
        args.append(final_g)
    aliases = {}
    if merged:
        aliases = {len(args): 0}
        in_specs.append(pl.BlockSpec(memory_space=pl.ANY))
        args.append(merge_into)
    if final:
        assert latent
        out_spec = pl.BlockSpec((None, rows, d), lambda b, j: (b, j, 0))
        out_shape = jax.ShapeDtypeStruct((bsz, t - ctx_len, d), F32)
    else:
        out_spec = cur
        out_shape = jax.ShapeDtypeStruct((bsz, t, d), F32)
    kern = functools.partial(_ffn_kernel, latent=latent, final=final, merged=merged)
    return pl.pallas_call(
        kern,
        grid=grid,
        in_specs=in_specs,
        out_specs=out_spec,
        out_shape=out_shape,
        input_output_aliases=aliases,
        scratch_shapes=[pltpu.VMEM((rows + 2 * HALO, d), BF16), pltpu.VMEM((rows, d), F32)]
        + [pltpu.VMEM((rows + 2 * HALO, FF_CHUNK), F32)] * 4,
        compiler_params=_params(("arbitrary", "arbitrary")),
        name="ffn_latent" if latent else "ffn_context",
    )(*args)


def _ffn_weights(g, w_up, b_up, conv_w, conv_b, w_down, b_down):
    layers, d, hidden2 = w_up.shape
    nc2 = hidden2 // FF_CHUNK
    taps = conv_w.shape[1]
    return (g.reshape(layers, 1, d),
            w_up.astype(BF16).reshape(layers, d, nc2, FF_CHUNK).transpose(0, 2, 1, 3),
            b_up.reshape(layers, nc2, 1, FF_CHUNK),
            conv_w.reshape(layers, taps, nc2, FF_CHUNK).transpose(0, 2, 1, 3),
            conv_b.reshape(layers, nc2, 1, FF_CHUNK),
            w_down.astype(BF16).reshape(layers, nc2 // 2, FF_CHUNK, d),
            b_down.reshape(layers, 1, d))


def _block_diag_tiles(w):
    nb, bs, _ = w.shape
    per = BD_TILE // bs
    rep = jnp.tile(w.reshape(nb * bs, bs), (1, per)).reshape(nb // per, BD_TILE, BD_TILE)
    pos = np.arange(BD_TILE) // bs
    return jnp.where(jnp.asarray(pos[:, None] == pos[None, :]), rep, 0.0).astype(BF16)


def _gate_cols(gate_w, gate_b, inner):
    dirs, _, ng = gate_w.shape
    g = gate_w.transpose(1, 0, 2).reshape(3 * inner, dirs * ng)
    g = jnp.pad(g, ((0, 0), (0, LANE - dirs * ng))).astype(BF16)
    gb = jnp.pad(gate_b.reshape(1, dirs * ng), ((0, 0), (0, LANE - dirs * ng)))
    return g[:inner], g[inner:2 * inner], g[2 * inner:], gb


def kernel(x, c, ctx, c_ctx, ada_w, ada_b, norm1_g, norm2_g, final_g, ml_w_in, ml_conv_w, ml_conv_b, ml_wq, ml_wk, ml_wv, ml_gate_w, ml_gate_b, ml_skip, ml_hnorm_g, ml_w_out, na_w_qkv, na_b_qkv, na_rpb, na_w_out, na_b_out, ff_w_up, ff_b_up, ff_conv_w, ff_conv_b, ff_w_down, ff_b_down):
    bsz, seq, d = x.shape
    ctx_len = ctx.shape[1]
    depth = ada_w.shape[0]
    assert ctx_len == ROW_BLOCK and seq % ROW_BLOCK == 0 and seq % GRID_W == 0
    assert bsz + 1 <= MOD_ROWS
    t = ctx_len + seq
    nblk = t // ROW_BLOCK

    c_all = jnp.zeros((MOD_ROWS, d), F32).at[:bsz].set(c).at[bsz].set(c_ctx)
    mods = _ada(c_all, ada_w, ada_b).reshape(depth, MOD_ROWS, 6, d)

    r = (ctx, x)
    row2 = lambda a: a.reshape(1, -1)
    ffn_weights = _ffn_weights(norm2_g, ff_w_up, ff_b_up, ff_conv_w, ff_conv_b, ff_w_down, ff_b_down)
    out = None
    for i in range(depth):
        need_ctx = i < depth - 1
        j = i // 2
        all_rows = _Rows(bsz, nblk, i, 0)
        live_rows = all_rows if need_ctx else _Rows(bsz, nblk, i, 1)
        if i % 2 == 0:
            inner = ml_w_in.shape[2] // 2
            dh = inner // ML_HEADS
            u, sz = _ml_in(all_rows, r, mods, row2(norm1_g[i]), ml_w_in[j].astype(BF16))
            gq, gk, gv, gb = _gate_cols(ml_gate_w[j], ml_gate_b[j], inner)
            uc, q, k, v, gates = _ml_proj(u, ml_conv_w[j], row2(ml_conv_b[j]), _block_diag_tiles(ml_wq[j]),
                                          _block_diag_tiles(ml_wk[j]), _block_diag_tiles(ml_wv[j]),
                                          gq, gk, gv, gb, dh, dh ** -0.5,
                                          ml_gate_w.shape[1] * ml_gate_w.shape[3])
            hn = _ml_core(q, k, v, gates, dh)
            r = _ml_out(live_rows, hn, uc, sz, r, mods, row2(ml_hnorm_g[j]), row2(ml_skip[j]),
                        ml_w_out[j].astype(BF16))
        else:
            q, k, v = _na_qkv(all_rows, r, mods, row2(norm1_g[i]), na_w_qkv[j].astype(BF16), row2(na_b_qkv[j]),
                              NA_HEAD_DIM ** -0.5 * LOG2E)
            r = _na_attn(live_rows, q, k, v, _na_bias_table(na_rpb[j] * LOG2E), r, mods,
                         na_w_out[j].astype(BF16), row2(na_b_out[j]), ctx_len)
        if need_ctx:
            r_new = _ffn_call(r, mods, i, ctx_len, ffn_weights, latent=True)
            r = _ffn_call(r, mods, i, ctx_len, ffn_weights, latent=False, merge_into=r_new)
        else:
            out = _ffn_call(r, mods, i, ctx_len, ffn_weights, latent=True, final_g=row2(final_g))
    return out
```
